```python
import math
import jax
import jax.numpy as jnp
from jax import lax
import numpy as np

D_MODEL = 1024
BATCH = 4
SEQ = 4096
DEPTH = 2
DEC_BATCH = 32
DEC_SEQ = 1
PAST_LEN = 16384
PAGE_SIZE = 128

A_HEAD_DIM = 64
A_INNER = D_MODEL
A_HEADS = A_INNER // A_HEAD_DIM
A_GROUPS = 2
A_STATE = 128
A_CONV = 4
A_CONV_DIM = A_INNER + 2 * A_GROUPS * A_STATE
A_CHUNK = 64
B_KEY_DIM = 128
B_VAL_DIM = 128
B_HEADS = D_MODEL // B_VAL_DIM
B_FORGET = B_HEADS * B_KEY_DIM
B_INNER = B_HEADS * B_VAL_DIM
B_CHUNK = 64
C_GROUPS = ((128, 1), (512, 4), (2048, 16))
C_HEADS_PER_GROUP = 4
C_HEADS = C_HEADS_PER_GROUP * len(C_GROUPS)
C_HEAD_DIM = 64
C_WIDTH = C_HEADS * C_HEAD_DIM
C_OUT = C_HEADS_PER_GROUP * C_HEAD_DIM
REL_BUCKETS = 32
REL_MAX_DIST = 2048
N_EXPERTS = 32
TOP_K = 4
D_FF = D_MODEL
SWIGLU_ALPHA = 1.702
SWIGLU_LIMIT = 7.0
MOE_BLOCK = 128
MOE_BLOCK_SMALL = 8
DEEPNORM_ALPHA = (2.0 * DEPTH) ** 0.25
DEEPNORM_BETA = (8.0 * DEPTH) ** -0.25
LN_EPS = 1e-5
RMS_EPS = 1e-5
NEG_BIG = -1e30
LOG_FLOOR = 1e-30

IN_WIDTHS = (A_INNER, A_CONV_DIM, A_HEADS, B_FORGET, B_FORGET, B_INNER, B_INNER, C_WIDTH, C_WIDTH, C_WIDTH, 3 * D_MODEL)
IN_DIM = sum(IN_WIDTHS)

kernel_name = 'hybrid_ssd_hgrn2_dilated_moe_step'


def _layernorm(x, g, b):
    xf = x.astype(jnp.float32)
    mu = jnp.mean(xf, -1, keepdims=True)
    var = jnp.mean(jnp.square(xf - mu), -1, keepdims=True)
    return ((xf - mu) * lax.rsqrt(var + LN_EPS) * g.astype(jnp.float32) + b.astype(jnp.float32)).astype(x.dtype)


def _rms(xf):
    return xf * lax.rsqrt(jnp.mean(jnp.square(xf), -1, keepdims=True) + RMS_EPS)


def _pad_time(x, length, axis=1):
    pad = length - x.shape[axis]
    if pad == 0:
        return x
    widths = [(0, 0)] * x.ndim
    widths[axis] = (0, pad)
    return jnp.pad(x, widths)


def _segsum(a):
    t = a.shape[-1]
    c = jnp.cumsum(a, axis=-1)
    d = c[..., :, None] - c[..., None, :]
    return jnp.where(jnp.tril(jnp.ones((t, t), dtype=bool)), d, NEG_BIG)


def _ssd(x, dt, a, bm, cm, h0):
    n, L, H, P = x.shape
    G, NS = bm.shape[2], bm.shape[3]
    R = H // G
    Q = min(A_CHUNK, L)
    nc = -(-L // Q)
    Lp = nc * Q
    xd = _pad_time(x.astype(jnp.float32) * dt[..., None], Lp).reshape(n, nc, Q, G, R, P)
    la = _pad_time(dt * a, Lp).reshape(n, nc, Q, G, R).transpose(0, 3, 4, 1, 2)
    bm = _pad_time(bm.astype(jnp.float32), Lp).reshape(n, nc, Q, G, NS)
    cm = _pad_time(cm.astype(jnp.float32), Lp).reshape(n, nc, Q, G, NS)
    acum = jnp.cumsum(la, axis=-1)
    lmat = jnp.exp(_segsum(la))
    y_diag = jnp.einsum('bclgn,bcsgn,bgrcls,bcsgrp->bclgrp', cm, bm, lmat, xd)
    decay_in = jnp.exp(acum[..., -1:] - acum)
    local = jnp.einsum('bclgn,bgrcl,bclgrp->bcgrpn', bm, decay_in, xd)
    h0r = h0.astype(jnp.float32).reshape(n, 1, G, R, P, NS)
    states = jnp.concatenate([h0r, local], axis=1)
    tot = jnp.pad(acum[..., -1], ((0, 0), (0, 0), (0, 0), (1, 0)))
    states = jnp.einsum('bgrzc,bcgrpn->bzgrpn', jnp.exp(_segsum(tot)), states)
    y_off = jnp.einsum('bclgn,bcgrpn,bgrcl->bclgrp', cm, states[:, :-1], jnp.exp(acum))
    y = (y_diag + y_off).reshape(n, Lp, H, P)[:, :L]
    return y, states[:, -1].reshape(n, H, P, NS)


def _mamba2(z, xbc, dt_raw, conv_buf, h0, conv_w, conv_b, dt_bias, a_log, d_skip, norm_g):
    n, L, _ = xbc.shape
    xpad = jnp.concatenate([conv_buf.astype(xbc.dtype), xbc], axis=1)
    conv = conv_b
    for j in range(A_CONV):
        conv = conv + xpad[:, j:j + L] * conv_w[j]
    conv = jax.nn.silu(conv)
    new_buf = xpad[:, L:]
    gn = A_GROUPS * A_STATE
    xs = conv[..., :A_INNER].reshape(n, L, A_HEADS, A_HEAD_DIM)
    bm = conv[..., A_INNER:A_INNER + gn].reshape(n, L, A_GROUPS, A_STATE)
    cm = conv[..., A_INNER + gn:].reshape(n, L, A_GROUPS, A_STATE)
    dt = jax.nn.softplus(dt_raw.astype(jnp.float32) + dt_bias.astype(jnp.float32))
    a = -jnp.exp(a_log.astype(jnp.float32))
    y, h_new = _ssd(xs, dt, a, bm, cm, h0)
    y = y + d_skip.astype(jnp.float32)[:, None] * xs.astype(jnp.float32)
    y = y.reshape(n, L, A_INNER) * jax.nn.silu(z.astype(jnp.float32))
    y = _rms(y.reshape(n, L, A_GROUPS, A_INNER // A_GROUPS)).reshape(n, L, A_INNER) * norm_g.astype(jnp.float32)
    return y.astype(z.dtype), new_buf, h_new


def _hgrn2_chunk(S, inp):
    q, k, v, g = inp
    b = jnp.cumsum(g, axis=2)
    Q = q.shape[2]
    causal = jnp.tril(jnp.ones((Q, Q), dtype=bool))[:, :, None]
    rel = jnp.exp(jnp.where(causal, b[:, :, :, None, :] - b[:, :, None, :, :], NEG_BIG))
    att = jnp.einsum('nhtk,nhsk,nhtsk->nhts', q, k, rel)
    o = jnp.einsum('nhts,nhsv->nhtv', att, v) + jnp.einsum('nhtk,nhkv->nhtv', q * jnp.exp(b), S)
    b_last = b[:, :, -1]
    S_new = jnp.exp(b_last)[..., None] * S + jnp.einsum('nhsk,nhsv->nhkv', k * jnp.exp(b_last[:, :, None] - b), v)
    return S_new, o


def _hgrn2(q_raw, f_raw, i_raw, g_raw, S0, lb, norm_g):
    f32 = jnp.float32
    n, L, _ = q_raw.shape
    H, K, V = B_HEADS, B_KEY_DIM, B_VAL_DIM
    lbf = lb.astype(f32).reshape(H, K)
    f = f_raw.astype(f32).reshape(n, L, H, K)
    q = jax.nn.silu(q_raw.astype(f32)).reshape(n, L, H, K) * (K ** -0.5)
    logf = jnp.log(jnp.maximum(lbf + (1.0 - lbf) * jax.nn.sigmoid(f), LOG_FLOOR))
    k = (1.0 - lbf) * jax.nn.sigmoid(-f)
    v = i_raw.astype(f32).reshape(n, L, H, V)
    Q = min(B_CHUNK, L)
    nc = -(-L // Q)
    Lp = nc * Q

    def chunks(t):
        return _pad_time(t, Lp).reshape(n, nc, Q, H, t.shape[-1]).transpose(1, 0, 3, 2, 4)

    S_new, o = lax.scan(_hgrn2_chunk, S0.astype(f32), (chunks(q), chunks(k), chunks(v), chunks(logf)))
    o = o.transpose(1, 0, 3, 2, 4).reshape(n, Lp, H, V)[:, :L]
    gate = jax.nn.silu(g_raw.astype(f32)).reshape(n, L, H, V)
    y = _rms(o) * norm_g.astype(f32) * gate
    return y.reshape(n, L, B_INNER).astype(q_raw.dtype), S_new


def _t5_bucket(dist):
    exact = REL_BUCKETS // 2
    d = jnp.maximum(dist, 1).astype(jnp.float32)
    large = exact + (jnp.log(d / exact) / math.log(REL_MAX_DIST / exact) * (REL_BUCKETS - exact)).astype(jnp.int32)
    large = jnp.clip(large, 0, REL_BUCKETS - 1)
    return jnp.where(dist < exact, dist, large)


def _dilated_prompt(q, k, v, tab, window, dilation):
    f32 = jnp.float32
    n, S, h, d = q.shape
    span = window // dilation
    lq = -(-S // dilation)
    sp = lq * dilation
    nb = -(-lq // span)
    lb = nb * span

    def to_blocks(t):
        t = _pad_time(t.astype(f32), sp).reshape(n, lq, dilation, h, d).transpose(0, 2, 1, 3, 4)
        return _pad_time(t, lb, axis=2).reshape(n, dilation, nb, span, h, d)

    def with_prev(t):
        prev = jnp.pad(t[:, :, :-1], ((0, 0), (0, 0), (1, 0), (0, 0), (0, 0), (0, 0)))
        return jnp.concatenate([prev, t], axis=3)

    qb = to_blocks(q)
    kk = with_prev(to_blocks(k))
    vv = with_prev(to_blocks(v))
    qi = jnp.arange(span)[:, None]
    kj = jnp.arange(2 * span)[None, :]
    rel = qi + span - kj
    band = (rel >= 0) & (rel <= span)
    bias = tab.astype(f32)[_t5_bucket(jnp.maximum(rel, 0) * dilation)].transpose(2, 0, 1)
    after_start = (jnp.arange(nb)[:, None, None] > 0) | (kj[None] >= span)
    mask = (band[None] & after_start)[None, None, :, None]
    s = jnp.einsum('bcnqhd,bcnkhd->bcnhqk', qb, kk) * (d ** -0.5) + bias
    s = jnp.where(mask, s, NEG_BIG)
    m = jnp.max(s, -1, keepdims=True)
    p = jnp.exp(s - m)
    den = jnp.sum(p, -1, keepdims=True)
    o = jnp.einsum('bcnhqk,bcnkhd->bcnqhd', p / den, vv)
    lse = (m + jnp.log(den))[..., 0].swapaxes(3, 4)

    def from_blocks(t):
        t = t.reshape(n, dilation, lb, *t.shape[4:])[:, :, :lq]
        t = jnp.moveaxis(t, 1, 2)
        return t.reshape(n, sp, *t.shape[3:])[:, :S]

    return from_blocks(o), from_blocks(lse)


def _dilated_sample(q, k, v, kv_buf, tab, window, dilation):
    f32 = jnp.float32
    n, T, h, d = q.shape
    wb = kv_buf.shape[1]
    kall = jnp.concatenate([kv_buf[:, :, 0].astype(f32), k.astype(f32)], axis=1)
    vall = jnp.concatenate([kv_buf[:, :, 1].astype(f32), v.astype(f32)], axis=1)
    offs = jnp.arange(window // dilation + 1) * dilation
    idx = wb + jnp.arange(T)[:, None] - offs[None, :]
    valid = idx >= 0
    idx = jnp.maximum(idx, 0)
    kg = kall[:, idx]
    vg = vall[:, idx]
    bias = tab.astype(f32)[_t5_bucket(offs)].T
    s = jnp.einsum('bthd,btjhd->bthj', q.astype(f32), kg) * (d ** -0.5) + bias
    s = jnp.where(valid[None, :, None, :], s, NEG_BIG)
    m = jnp.max(s, -1, keepdims=True)
    p = jnp.exp(s - m)
    den = jnp.sum(p, -1, keepdims=True)
    o = jnp.einsum('bthj,btjhd->bthd', p / den, vg)
    return o, (m + jnp.log(den))[..., 0]


def _dilated_mixer(cq, ck, cv, rel_bias, kv_bufs):
    n, L, _ = cq.shape
    q = cq.reshape(n, L, C_HEADS, C_HEAD_DIM)
    k = ck.reshape(n, L, C_HEADS, C_HEAD_DIM)
    v = cv.reshape(n, L, C_HEADS, C_HEAD_DIM)
    outs, lses, kv_new = [], [], []
    for g, (window, dilation) in enumerate(C_GROUPS):
        hs = slice(g * C_HEADS_PER_GROUP, (g + 1) * C_HEADS_PER_GROUP)
        qg, kg, vg = q[:, :, hs], k[:, :, hs], v[:, :, hs]
        tab = rel_bias[:, hs]
        rows = jnp.stack([kg, vg], axis=2)
        if kv_bufs is None:
            o, lse = _dilated_prompt(qg, kg, vg, tab, window, dilation)
            kv_new.append(rows[:, L - min(window, L):])
        else:
            o, lse = _dilated_sample(qg, kg, vg, kv_bufs[g], tab, window, dilation)
            kv_new.append(rows)
        outs.append(o)
        lses.append(lse)
    wts = jax.nn.softmax(jnp.stack(lses, 0), axis=0)
    o = wts[0][..., None] * outs[0] + wts[1][..., None] * outs[1] + wts[2][..., None] * outs[2]
    return o.reshape(n, L, C_OUT).astype(cq.dtype), kv_new


def _moe(h, router_w, router_b, w1, b1, w2, b2):
    n_tok, d = h.shape
    logits = (h @ router_w + router_b).astype(jnp.float32)
    top_v, top_e = lax.top_k(logits, TOP_K)
    gates = jax.nn.softmax(top_v, axis=-1)
    m = n_tok * TOP_K
    blk = MOE_BLOCK if m >= MOE_BLOCK * N_EXPERTS else MOE_BLOCK_SMALL
    n_blocks = -(-m // blk) + N_EXPERTS
    rows = n_blocks * blk
    flat_e = top_e.reshape(-1)
    order = jnp.argsort(flat_e)
    se = flat_e[order]
    counts = jnp.zeros((N_EXPERTS,), jnp.int32).at[flat_e].add(1)
    padded = (counts + blk - 1) // blk * blk
    start = jnp.cumsum(counts) - counts
    pend = jnp.cumsum(padded)
    pstart = pend - padded
    dest = pstart[se] + jnp.arange(m, dtype=jnp.int32) - start[se]
    tok = (jnp.arange(m, dtype=jnp.int32) // TOP_K)[order]
    buf_tok = jnp.zeros((rows,), jnp.int32).at[dest].set(tok)
    buf_gate = jnp.zeros((rows,), jnp.float32).at[dest].set(gates.reshape(-1)[order])
    block_e = jnp.minimum(jnp.searchsorted(pend, jnp.arange(n_blocks, dtype=jnp.int32) * blk, side='right'), N_EXPERTS - 1)
    xb = h[buf_tok].reshape(n_blocks, blk, d)

    def expert_block(args):
        xe, e = args
        u = xe @ w1[e] + b1[e]
        glu = jnp.minimum(u[..., ::2], SWIGLU_LIMIT)
        lin = jnp.clip(u[..., 1::2], -SWIGLU_LIMIT, SWIGLU_LIMIT)
        act = glu * jax.nn.sigmoid(SWIGLU_ALPHA * glu) * (lin + 1.0)
        return act @ w2[e] + b2[e]

    yb = lax.map(expert_block, (xb, block_e))
    y = jnp.zeros((n_tok, d), jnp.float32).at[buf_tok].add(yb.reshape(rows, d).astype(jnp.float32) * buf_gate[:, None])
    return y.astype(h.dtype)


def _layer(x, prm, lb, rel_bias, conv_buf, ssm0, hgrn0, kv_bufs):
    n, L, D = x.shape
    u = x @ prm['w_in']
    (z, xbc, dt_raw, bq, bf, bi, bg, cq, ck, cv, gates) = jnp.split(u, np.cumsum(IN_WIDTHS)[:-1].tolist(), axis=-1)
    ya, conv_new, ssm_new = _mamba2(z, xbc, dt_raw, conv_buf, ssm0, prm['conv_w'], prm['conv_b'], prm['dt_bias'],
                                    prm['a_log'], prm['d_skip'], prm['ssm_norm_g'])
    yb, hgrn_new = _hgrn2(bq, bf, bi, bg, hgrn0, lb, prm['hgrn_norm_g'])
    yc, kv_new = _dilated_mixer(cq, ck, cv, rel_bias, kv_bufs)
    ga, gb, gc = jnp.split(jax.nn.sigmoid(gates), 3, axis=-1)
    merged = ga * (ya @ prm['w_branch_a']) + gb * (yb @ prm['w_branch_b']) + gc * (yc @ prm['w_branch_c'])
    x = _layernorm(DEEPNORM_ALPHA * x + merged @ prm['w_out'], prm['ln1_g'], prm['ln1_b'])
    f = _moe(x.reshape(n * L, D), prm['router_w'], prm['router_b'], prm['moe_w1'], prm['moe_b1'],
             prm['moe_w2'], prm['moe_b2']).reshape(n, L, D)
    x = _layernorm(DEEPNORM_ALPHA * x + f, prm['ln2_g'], prm['ln2_b'])
    return x, (conv_new, ssm_new, hgrn_new, kv_new[0], kv_new[1], kv_new[2])


def _stack(states, i):
    return jnp.stack([s[i] for s in states], axis=0)


def setup_inputs(seed: int = 0) -> dict:
    key = jax.random.key(seed)
    keys = iter(jax.random.split(key, 48))
    f32 = jnp.float32

    def normal(shape, scale):
        return jax.random.normal(next(keys), shape, f32) * scale

    def gain(shape):
        return 1.0 + normal(shape, 0.02)

    def kv_shape(w):
        return (DEPTH, DEC_BATCH, min(w, PAST_LEN), 2, C_HEADS_PER_GROUP, C_HEAD_DIM)

    dt0 = jnp.exp(jax.random.uniform(next(keys), (DEPTH, A_HEADS), f32, math.log(1e-3), math.log(1e-1)))
    a_log = jnp.log(jax.random.uniform(next(keys), (DEPTH, A_HEADS), f32, 1.0, 16.0))
    return {
        'x_prompt': normal((BATCH, SEQ, D_MODEL), 1.0),
        'x_sample': normal((DEC_BATCH, DEC_SEQ, D_MODEL), 1.0),
        'state_conv': normal((DEPTH, DEC_BATCH, A_CONV - 1, A_CONV_DIM), 1.0),
        'state_ssm': normal((DEPTH, DEC_BATCH, A_HEADS, A_HEAD_DIM, A_STATE), 0.1),
        'state_hgrn': normal((DEPTH, DEC_BATCH, B_HEADS, B_KEY_DIM, B_VAL_DIM), 0.5),
        'cache_kv_w128': normal(kv_shape(C_GROUPS[0][0]), 1.0),
        'cache_kv_w512': normal(kv_shape(C_GROUPS[1][0]), 1.0),
        'cache_kv_w2048': normal(kv_shape(C_GROUPS[2][0]), 1.0),
        'w_in': normal((DEPTH, D_MODEL, IN_DIM), D_MODEL ** -0.5),
        'conv_w': normal((DEPTH, A_CONV, A_CONV_DIM), A_CONV ** -0.5),
        'conv_b': normal((DEPTH, A_CONV_DIM), 0.01),
        'dt_bias': dt0 + jnp.log(-jnp.expm1(-dt0)),
        'a_log': a_log,
        'd_skip': gain((DEPTH, A_HEADS)),
        'ssm_norm_g': gain((DEPTH, A_INNER)),
        'hgrn_lb': normal((DEPTH, B_FORGET), 0.5),
        'hgrn_norm_g': gain((DEPTH, B_VAL_DIM)),
        'rel_bias': normal((REL_BUCKETS, C_HEADS), 0.2),
        'w_branch_a': normal((DEPTH, A_INNER, D_MODEL), A_INNER ** -0.5),
        'w_branch_b': normal((DEPTH, B_INNER, D_MODEL), B_INNER ** -0.5),
        'w_branch_c': normal((DEPTH, C_OUT, D_MODEL), C_OUT ** -0.5),
        'w_out': normal((DEPTH, D_MODEL, D_MODEL), DEEPNORM_BETA * D_MODEL ** -0.5),
        'ln1_g': gain((DEPTH, D_MODEL)),
        'ln1_b': normal((DEPTH, D_MODEL), 0.01),
        'router_w': normal((DEPTH, D_MODEL, N_EXPERTS), D_MODEL ** -0.5),
        'router_b': normal((DEPTH, N_EXPERTS), 0.01),
        'moe_w1': normal((DEPTH, N_EXPERTS, D_MODEL, 2 * D_FF), D_MODEL ** -0.5),
        'moe_b1': normal((DEPTH, N_EXPERTS, 2 * D_FF), 0.01),
        'moe_w2': normal((DEPTH, N_EXPERTS, D_FF, D_MODEL), DEEPNORM_BETA * D_FF ** -0.5),
        'moe_b2': normal((DEPTH, N_EXPERTS, D_MODEL), 0.01),
        'ln2_g': gain((DEPTH, D_MODEL)),
        'ln2_b': normal((DEPTH, D_MODEL), 0.01),
    }


def reference(x_prompt, x_sample, state_conv, state_ssm, state_hgrn, cache_kv_w128, cache_kv_w512, cache_kv_w2048,
              w_in, conv_w, conv_b, dt_bias, a_log, d_skip, ssm_norm_g, hgrn_lb, hgrn_norm_g, rel_bias,
              w_branch_a, w_branch_b, w_branch_c, w_out, ln1_g, ln1_b, router_w, router_b,
              moe_w1, moe_b1, moe_w2, moe_b2, ln2_g, ln2_b):
    p_lb = jax.nn.softmax(hgrn_lb.astype(jnp.float32), axis=0)
    lower_bounds = jnp.cumsum(p_lb, axis=0) - p_lb[0]
    yp, ys = x_prompt, x_sample
    st_p, st_s = [], []
    for l in range(DEPTH):
        prm = {'w_in': w_in[l], 'conv_w': conv_w[l], 'conv_b': conv_b[l], 'dt_bias': dt_bias[l], 'a_log': a_log[l],
               'd_skip': d_skip[l], 'ssm_norm_g': ssm_norm_g[l], 'hgrn_norm_g': hgrn_norm_g[l],
               'w_branch_a': w_branch_a[l], 'w_branch_b': w_branch_b[l], 'w_branch_c': w_branch_c[l],
               'w_out': w_out[l], 'ln1_g': ln1_g[l], 'ln1_b': ln1_b[l], 'router_w': router_w[l],
               'router_b': router_b[l], 'moe_w1': moe_w1[l], 'moe_b1': moe_b1[l], 'moe_w2': moe_w2[l],
               'moe_b2': moe_b2[l], 'ln2_g': ln2_g[l], 'ln2_b': ln2_b[l]}
        nb = yp.shape[0]
        conv0 = jnp.zeros((nb, A_CONV - 1, A_CONV_DIM), yp.dtype)
        ssm0 = jnp.zeros((nb, A_HEADS, A_HEAD_DIM, A_STATE), jnp.float32)
        hgrn0 = jnp.zeros((nb, B_HEADS, B_KEY_DIM, B_VAL_DIM), jnp.float32)
        yp, sp = _layer(yp, prm, lower_bounds[l], rel_bias, conv0, ssm0, hgrn0, None)
        ys, ss = _layer(ys, prm, lower_bounds[l], rel_bias, state_conv[l], state_ssm[l], state_hgrn[l],
                        (cache_kv_w128[l], cache_kv_w512[l], cache_kv_w2048[l]))
        st_p.append(sp)
        st_s.append(ss)
    return (yp, ys, _stack(st_p, 0), _stack(st_s, 0), _stack(st_p, 1), _stack(st_s, 1), _stack(st_p, 2), _stack(st_s, 2),
            _stack(st_p, 3), _stack(st_s, 3), _stack(st_p, 4), _stack(st_s, 4), _stack(st_p, 5), _stack(st_s, 5))
```

```python
import functools
import math

import jax
import jax.numpy as jnp
import numpy as np
from jax import lax
from jax.experimental import pallas as pl
from jax.experimental.pallas import tpu as pltpu

F32 = jnp.float32
BF16 = jnp.bfloat16
HI = lax.Precision.HIGHEST

D_MODEL = 1024
A_HEADS = 16
A_HEAD_DIM = 64
A_STATE = 128
A_CONV = 4
A_CONV_DIM = 1536
B_HEADS = 8
B_KEY_DIM = 128
C_GROUPS = ((128, 1), (512, 4), (2048, 16))
C_SPAN = 128
C_GROUP_WIDTH = 256
REL_BUCKETS = 32
REL_MAX_DIST = 2048
N_EXPERTS = 32
TOP_K = 4
SWIGLU_ALPHA = 1.702
SWIGLU_LIMIT = 7.0
DEEPNORM_ALPHA = (2.0 * 2) ** 0.25
LN_EPS = 1e-5
RMS_EPS = 1e-5
NEG_BIG = -1e30
LOG_FLOOR = 1e-30

LANE = 128
SUBLANE = 8
CHUNK = 128
SUB = 16
VMEM_LIMIT = 56 * 1024 * 1024

IN_PIECES = (('z', 1024), ('xbc', 1536), ('bq', 1024), ('bf', 1024), ('bi', 1024), ('bg', 1024),
             ('cq', 768), ('ck', 768), ('cv', 768), ('gates', 3072), ('dt', 256))
IN_PACKED = sum(w for _, w in IN_PIECES)


def _params(sem):
    return pltpu.CompilerParams(dimension_semantics=sem, vmem_limit_bytes=VMEM_LIMIT)


def _sigmoid(x):
    return 1.0 / (1.0 + jnp.exp(-x))


def _softplus(x):
    return jnp.maximum(x, 0.0) + jnp.log(1.0 + jnp.exp(-jnp.abs(x)))


def _dot(a, b):
    return jnp.dot(a.astype(BF16), b.astype(BF16), preferred_element_type=F32)


def _dot_nt(a, b):
    return lax.dot_general(a.astype(BF16), b.astype(BF16), (((1,), (1,)), ((), ())),
                           preferred_element_type=F32)


def _dot_hi(a, b):
    return jnp.dot(a, b, preferred_element_type=F32, precision=HI)


def _layernorm(h, g, b):
    mu = jnp.mean(h, -1, keepdims=True)
    c = h - mu
    var = jnp.mean(c * c, -1, keepdims=True)
    return c * lax.rsqrt(var + LN_EPS) * g + b


def _in_proj_kernel(x_ref, w_ref, *o_refs):
    xb = x_ref[...].astype(BF16)
    off = 0
    for o_ref in o_refs:
        wd = o_ref.shape[1]
        o_ref[...] = jnp.dot(xb, w_ref[:, off:off + wd], preferred_element_type=F32)
        off += wd


def _in_proj(x, w_packed, tm):
    t = x.shape[0]
    outs = pl.pallas_call(
        _in_proj_kernel,
        grid=(t // tm,),
        in_specs=[pl.BlockSpec((tm, D_MODEL), lambda i: (i, 0)),
                  pl.BlockSpec(memory_space=pltpu.VMEM)],
        out_specs=[pl.BlockSpec((tm, w), lambda i: (i, 0)) for _, w in IN_PIECES],
        out_shape=[jax.ShapeDtypeStruct((t, w), F32) for _, w in IN_PIECES],
        compiler_params=_params(("arbitrary",)),
        name="in_proj",
    )(x, w_packed)
    return {name: o for (name, _), o in zip(IN_PIECES, outs)}


def _pack_w_in(w):
    dt = w[:, 2560:2576]
    dtp = jnp.zeros((D_MODEL, 256), F32).at[:, 0:8].set(dt[:, :8]).at[:, 128:136].set(dt[:, 8:])
    return jnp.concatenate([w[:, :2560], w[:, 2576:], dtp], axis=1).astype(BF16)


def _stage(ref, ci, decode, r):
    if decode:
        row = ref[pl.ds(r, 1), :]
        rows = lax.broadcasted_iota(jnp.int32, (CHUNK, row.shape[1]), 0)
        return jnp.where(rows == 0, jnp.broadcast_to(row, (CHUNK, row.shape[1])), 0.0)
    return ref[pl.ds(pl.multiple_of(ci * CHUNK, CHUNK), CHUNK), :]


def _unstage(ref, val, ci, decode, r):
    if decode:
        ref[pl.ds(r, 1), :] = val[0:1, :]
    else:
        ref[pl.ds(pl.multiple_of(ci * CHUNK, CHUNK), CHUNK), :] = val


def _ssd_kernel(xs_ref, bm_ref, cm_ref, z_ref, dt_ref, dtt_ref,
                c0x_ref, c0b_ref, c0c_ref, h0_ref,
                wx_ref, wb_ref, wc_ref, bx_ref, bb_ref, bc_ref,
                dtb_ref, alog_ref, dtbt_ref, alogt_ref, dsk_ref, ng_ref,
                y_ref, hn_ref,
                h_scr, cx_scr, cb_scr, cc_scr, px_scr, pb_scr, pc_scr, *, nchunks, decode):
    b = pl.program_id(1)
    t = pl.program_id(2)
    r = b % SUBLANE
    q = CHUNK

    @pl.when(t == 0)
    def _():
        h_scr[...] = h0_ref[0]
        cx_scr[...] = c0x_ref[0]
        cb_scr[...] = c0b_ref[0]
        cc_scr[...] = c0c_ref[0]

    rows = lax.broadcasted_iota(jnp.int32, (q, q), 0)
    cols = lax.broadcasted_iota(jnp.int32, (q, q), 1)
    tril = rows >= cols
    tril_f = tril.astype(F32)
    triu_f = (rows <= cols).astype(F32)
    lane = lax.broadcasted_iota(jnp.int32, (q, LANE), 1)
    lo = lane < A_HEAD_DIM
    row_lo = lax.broadcasted_iota(jnp.int32, (LANE, LANE), 0) < A_HEAD_DIM
    valid_col = lax.broadcasted_iota(jnp.int32, (q, LANE), 0) == 0
    valid_row = lax.broadcasted_iota(jnp.int32, (SUBLANE, q), 1) == 0

    def conv(x, carry, pad, w_ref, b_ref):
        pad[0:SUBLANE, :] = carry[...]
        pad[SUBLANE:SUBLANE + q, :] = x
        if not decode:
            carry[...] = pad[q:q + SUBLANE, :]
        acc = b_ref[...]
        for j in range(A_CONV):
            acc = acc + pad[5 + j:5 + j + q, :] * w_ref[j:j + 1, :]
        return acc * _sigmoid(acc)

    def chunk(ci, carry_unused):
        xs = conv(_stage(xs_ref, ci, decode, r), cx_scr, px_scr, wx_ref, bx_ref)
        bm = conv(_stage(bm_ref, ci, decode, r), cb_scr, pb_scr, wb_ref, bb_ref)
        cm = conv(_stage(cm_ref, ci, decode, r), cc_scr, pc_scr, wc_ref, bc_ref)
        a_lane = -jnp.exp(alog_ref[...])
        a_sub = -jnp.exp(alogt_ref[...])
        dt = _softplus(_stage(dt_ref, ci, decode, r) + dtb_ref[...])
        dtt = _softplus(dtt_ref[ci] + dtbt_ref[...])
        if decode:
            dt = jnp.where(valid_col, dt, 0.0)
            dtt = jnp.where(valid_row, dtt, 0.0)
        acum = _dot_hi(tril_f, dt * a_lane)
        acum_t = _dot_hi(dtt * a_sub, triu_f)
        a_last = acum[q - 1:q, :]
        cb = _dot_nt(cm, bm)
        cm_b = cm.astype(BF16)
        bm_b = bm.astype(BF16)
        ys = []
        for j in range(4):
            xp = xs[:, LANE * j:LANE * (j + 1)]
            xp_b = xp.astype(BF16)
            yd = []
            for half in range(2):
                hl = 2 * j + half
                diff = acum[:, hl:hl + 1] - acum_t[hl:hl + 1, :]
                lm = jnp.exp(jnp.where(tril, diff, NEG_BIG))
                m = cb * lm * dtt[hl:hl + 1, :]
                yd.append(jnp.dot(m.astype(BF16), xp_b, preferred_element_type=F32))
            y = jnp.where(lo, yd[0], yd[1])
            hp = h_scr[LANE * j:LANE * (j + 1), :]
            yo = lax.dot_general(cm_b, hp.astype(BF16), (((1,), (1,)), ((), ())),
                                 preferred_element_type=F32)
            e0 = acum[:, 2 * j:2 * j + 1]
            e1 = acum[:, 2 * j + 1:2 * j + 2]
            y = y + yo * jnp.exp(jnp.where(lo, e0, e1))
            w0 = dt[:, 2 * j:2 * j + 1] * jnp.exp(a_last[:, 2 * j:2 * j + 1] - e0)
            w1 = dt[:, 2 * j + 1:2 * j + 2] * jnp.exp(a_last[:, 2 * j + 1:2 * j + 2] - e1)
            xw = xp * jnp.where(lo, w0, w1)
            upd = jnp.dot(xw.T.astype(BF16), bm_b, preferred_element_type=F32)
            dec = jnp.exp(jnp.where(row_lo, a_last[:, 2 * j:2 * j + 1], a_last[:, 2 * j + 1:2 * j + 2]))
            h_scr[LANE * j:LANE * (j + 1), :] = hp * dec + upd
            ys.append(y)
        y = jnp.concatenate(ys, axis=1) + dsk_ref[...] * xs
        z = _stage(z_ref, ci, decode, r)
        y = y * (z * _sigmoid(z))
        y = y * lax.rsqrt(jnp.mean(y * y, -1, keepdims=True) + RMS_EPS) * ng_ref[...]
        _unstage(y_ref, y, ci, decode, r)
        return carry_unused

    lax.fori_loop(0, nchunks, chunk, 0)

    @pl.when(t == pl.num_programs(2) - 1)
    def _():
        hn_ref[0] = h_scr[...]


def _ssd(u, dt_t, conv0_8, h0, prm, n, seq, decode):
    tb = CHUNK if decode else min(512, seq)
    nb = 1 if decode else seq // tb
    nchunks = tb // CHUNK
    rb = SUBLANE if decode else tb

    def rowmap(lane_block):
        if decode:
            return lambda g, b, t: (b // SUBLANE, lane_block(g))
        return lambda g, b, t: (b * nb + t, lane_block(g))

    chunk_map = (lambda g, b, t: (b, g, 0)) if decode else (lambda g, b, t: (b * nb + t, g, 0))
    cw = (512, LANE, LANE)
    lane_blocks = (lambda g: g, lambda g: 8 + g, lambda g: 10 + g)
    in_specs = [
        pl.BlockSpec((rb, 512), rowmap(lane_blocks[0])),
        pl.BlockSpec((rb, LANE), rowmap(lane_blocks[1])),
        pl.BlockSpec((rb, LANE), rowmap(lane_blocks[2])),
        pl.BlockSpec((rb, 512), rowmap(lambda g: g)),
        pl.BlockSpec((rb, LANE), rowmap(lambda g: g)),
        pl.BlockSpec((nchunks, SUBLANE, CHUNK), chunk_map),
    ]
    in_specs += [pl.BlockSpec((1, SUBLANE, w), (lambda lb: (lambda g, b, t: (b, 0, lb(g))))(lb))
                 for w, lb in zip(cw, lane_blocks)]
    in_specs += [pl.BlockSpec((1, 512, A_STATE), lambda g, b, t: (b, g, 0))]
    in_specs += [pl.BlockSpec((SUBLANE, w), (lambda lb: (lambda g, b, t: (0, lb(g))))(lb))
                 for w, lb in zip(cw, lane_blocks)]
    in_specs += [pl.BlockSpec((1, w), (lambda lb: (lambda g, b, t: (0, lb(g))))(lb))
                 for w, lb in zip(cw, lane_blocks)]
    in_specs += [
        pl.BlockSpec((1, LANE), lambda g, b, t: (0, g)),
        pl.BlockSpec((1, LANE), lambda g, b, t: (0, g)),
        pl.BlockSpec((SUBLANE, LANE), lambda g, b, t: (g, 0)),
        pl.BlockSpec((SUBLANE, LANE), lambda g, b, t: (g, 0)),
        pl.BlockSpec((1, 512), lambda g, b, t: (0, g)),
        pl.BlockSpec((1, 512), lambda g, b, t: (0, g)),
    ]
    rows_total = u['z'].shape[0]
    y, hn = pl.pallas_call(
        functools.partial(_ssd_kernel, nchunks=nchunks, decode=decode),
        grid=(2, n, nb),
        in_specs=in_specs,
        out_specs=[pl.BlockSpec((rb, 512), rowmap(lambda g: g)),
                   pl.BlockSpec((1, 512, A_STATE), lambda g, b, t: (b, g, 0))],
        out_shape=[jax.ShapeDtypeStruct((rows_total, D_MODEL), F32),
                   jax.ShapeDtypeStruct((n, D_MODEL, A_STATE), F32)],
        scratch_shapes=[pltpu.VMEM((512, A_STATE), F32),
                        pltpu.VMEM((SUBLANE, 512), F32), pltpu.VMEM((SUBLANE, LANE), F32),
                        pltpu.VMEM((SUBLANE, LANE), F32),
                        pltpu.VMEM((CHUNK + SUBLANE, 512), F32), pltpu.VMEM((CHUNK + SUBLANE, LANE), F32),
                        pltpu.VMEM((CHUNK + SUBLANE, LANE), F32)],
        compiler_params=_params(("arbitrary", "arbitrary", "arbitrary")),
        name="ssd_decode" if decode else "ssd_prompt",
    )(u['xbc'], u['xbc'], u['xbc'], u['z'], u['dt'], dt_t,
      conv0_8, conv0_8, conv0_8, h0,
      prm['conv_w8'], prm['conv_w8'], prm['conv_w8'], prm['conv_b'], prm['conv_b'], prm['conv_b'],
      prm['dt_bias_l'], prm['a_log_l'], prm['dt_bias_t'], prm['a_log_t'], prm['d_skip_l'], prm['ssm_norm_g'])
    return y, hn


def _hgrn_kernel(q_ref, f_ref, i_ref, g_ref, lb_ref, ng_ref, s0_ref, y_ref, sn_ref, st_scr,
                 *, nchunks, decode):
    b = pl.program_id(1)
    t = pl.program_id(2)
    r = b % SUBLANE
    c = CHUNK
    nsub = c // SUB

    @pl.when(t == 0)
    def _():
        st_scr[...] = s0_ref[0].T

    rows = lax.broadcasted_iota(jnp.int32, (c, c), 0)
    cols = lax.broadcasted_iota(jnp.int32, (c, c), 1)
    tril_f = (rows >= cols).astype(F32)
    row_c = lax.broadcasted_iota(jnp.int32, (c, LANE), 0)
    row_s = lax.broadcasted_iota(jnp.int32, (SUB, LANE), 0)
    lbv = lb_ref[...]
    live_sub = 1 if decode else nsub
    live_s = 1 if decode else SUB

    def chunk(ci, carry_unused):
        f = _stage(f_ref, ci, decode, r)
        qr = _stage(q_ref, ci, decode, r)
        v = _stage(i_ref, ci, decode, r)
        gr = _stage(g_ref, ci, decode, r)
        q = qr * _sigmoid(qr) * (B_KEY_DIM ** -0.5)
        logf = jnp.log(jnp.maximum(lbv + (1.0 - lbv) * _sigmoid(f), LOG_FLOOR))
        k = (1.0 - lbv) * _sigmoid(-f)
        if decode:
            logf = jnp.where(row_c == 0, logf, 0.0)
            k = jnp.where(row_c == 0, k, 0.0)
        bcum = _dot_hi(tril_f, logf)
        b_last = bcum[c - 1:c, :]
        st = st_scr[...]
        o = _dot_nt(q * jnp.exp(bcum), st)
        if live_sub > 1:
            att_rows = [jnp.zeros((SUB, c), F32)]
            qa = []
            for i in range(1, nsub):
                bsi = bcum[SUB * i - 1:SUB * i, :]
                ki = jnp.where(row_c < SUB * i, k * jnp.exp(jnp.minimum(bsi - bcum, 0.0)), 0.0)
                ai = q[SUB * i:SUB * (i + 1), :] * jnp.exp(bcum[SUB * i:SUB * (i + 1), :] - bsi)
                att_rows.append(_dot_nt(ai, ki))
            o = o + _dot(jnp.concatenate(att_rows, axis=0), v)
        diag = []
        for i in range(nsub):
            if i >= live_sub:
                diag.append(jnp.zeros((SUB, LANE), F32))
                continue
            sl = slice(SUB * i, SUB * (i + 1))
            qb, kb, vb, bb = q[sl, :], k[sl, :], v[sl, :], bcum[sl, :]
            od = jnp.zeros((SUB, LANE), F32)
            for s in range(live_s):
                e = jnp.exp(jnp.where(row_s >= s, bb - bb[s:s + 1, :], NEG_BIG))
                rs = jnp.sum(qb * kb[s:s + 1, :] * e, -1, keepdims=True)
                od = od + rs * vb[s:s + 1, :]
            diag.append(od)
        o = o + jnp.concatenate(diag, axis=0)
        k2 = k * jnp.exp(b_last - bcum)
        st_scr[...] = jnp.exp(b_last) * st + jnp.dot(v.T.astype(BF16), k2.astype(BF16),
                                                      preferred_element_type=F32)
        gate = gr * _sigmoid(gr)
        y = o * lax.rsqrt(jnp.mean(o * o, -1, keepdims=True) + RMS_EPS) * ng_ref[...] * gate
        _unstage(y_ref, y, ci, decode, r)
        return carry_unused

    lax.fori_loop(0, nchunks, chunk, 0)

    @pl.when(t == pl.num_programs(2) - 1)
    def _():
        sn_ref[0] = st_scr[...].T


def _hgrn(u, s0, lb, ng, n, seq, decode):
    tb = CHUNK if decode else min(512, seq)
    nb = 1 if decode else seq // tb
    nchunks = tb // CHUNK
    rb = SUBLANE if decode else tb
    rowmap = (lambda h, b, t: (b // SUBLANE, h)) if decode else (lambda h, b, t: (b * nb + t, h))
    rows_total = u['bq'].shape[0]
    y, sn = pl.pallas_call(
        functools.partial(_hgrn_kernel, nchunks=nchunks, decode=decode),
        grid=(B_HEADS, n, nb),
        in_specs=[pl.BlockSpec((rb, LANE), rowmap)] * 4 + [
            pl.BlockSpec((1, LANE), lambda h, b, t: (0, h)),
            pl.BlockSpec((1, LANE), lambda h, b, t: (0, 0)),
            pl.BlockSpec((1, B_KEY_DIM, LANE), lambda h, b, t: (b, h, 0))],
        out_specs=[pl.BlockSpec((rb, LANE), rowmap),
                   pl.BlockSpec((1, B_KEY_DIM, LANE), lambda h, b, t: (b, h, 0))],
        out_shape=[jax.ShapeDtypeStruct((rows_total, D_MODEL), F32),
                   jax.ShapeDtypeStruct((n, D_MODEL, LANE), F32)],
        scratch_shapes=[pltpu.VMEM((LANE, B_KEY_DIM), F32)],
        compiler_params=_params(("arbitrary", "arbitrary", "arbitrary")),
        name="hgrn_decode" if decode else "hgrn_prompt",
    )(u['bq'], u['bf'], u['bi'], u['bg'], lb, ng, s0)
    return y, sn


def _t5_bucket_np(dist):
    exact = REL_BUCKETS // 2
    d = np.maximum(dist, 1).astype(np.float32)
    large = exact + (np.log(d / np.float32(exact)) / np.float32(math.log(REL_MAX_DIST / exact))
                     * np.float32(REL_BUCKETS - exact)).astype(np.int32)
    large = np.clip(large, 0, REL_BUCKETS - 1)
    return np.where(dist < exact, dist, large)


def _attn_prompt_kernel(q_ref, kc_ref, kp_ref, vc_ref, vp_ref, bias_ref, o_ref, lse_ref):
    first = pl.program_id(2) == 0
    s_q = C_SPAN
    q = q_ref[...]
    k2 = jnp.concatenate([kp_ref[...], kc_ref[...]], axis=0).astype(BF16)
    v2 = jnp.concatenate([vp_ref[...], vc_ref[...]], axis=0).astype(BF16)
    lane = lax.broadcasted_iota(jnp.int32, (s_q, LANE), 1)
    lo = lane < 64
    kcol = lax.broadcasted_iota(jnp.int32, (s_q, 2 * s_q), 1)
    no_prev = jnp.logical_and(first, kcol < s_q)
    outs, lses = [], []
    for j in range(2):
        qp = q[:, LANE * j:LANE * (j + 1)]
        kp2 = k2[:, LANE * j:LANE * (j + 1)]
        vp2 = v2[:, LANE * j:LANE * (j + 1)]
        oh, lh = [], []
        for half in range(2):
            qm = jnp.where(lo if half == 0 else jnp.logical_not(lo), qp, 0.0).astype(BF16)
            s = lax.dot_general(qm, kp2, (((1,), (1,)), ((), ())), preferred_element_type=F32)
            s = s * (64 ** -0.5) + bias_ref[2 * j + half]
            s = jnp.where(no_prev, NEG_BIG, s)
            m = jnp.max(s, -1, keepdims=True)
            p = jnp.exp(s - m)
            den = jnp.sum(p, -1, keepdims=True)
            oh.append(jnp.dot((p / den).astype(BF16), vp2, preferred_element_type=F32))
            lh.append(m + jnp.log(den))
        outs.append(jnp.where(lo, oh[0], oh[1]))
        lses.append(jnp.where(lo, lh[0], lh[1]))
    o_ref[...] = jnp.concatenate(outs, axis=1)
    lse_ref[...] = jnp.concatenate(lses, axis=1)


def _attn_prompt(cq, ck, cv, bias, g, n, seq):
    dil = C_GROUPS[g][1]
    lq = seq // dil
    nbk = lq // C_SPAN
    rows = n * lq
    q2 = cq.reshape(rows, dil * 768)
    k2 = ck.reshape(rows, dil * 768)
    v2 = cv.reshape(rows, dil * 768)
    cur = lambda b, r, i: (b * nbk + i, r * 3 + g)
    prev = lambda b, r, i: (b * nbk + jnp.maximum(i - 1, 0), r * 3 + g)
    blk = (C_SPAN, C_GROUP_WIDTH)
    o, lse = pl.pallas_call(
        _attn_prompt_kernel,
        grid=(n, dil, nbk),
        in_specs=[pl.BlockSpec(blk, cur), pl.BlockSpec(blk, cur), pl.BlockSpec(blk, prev),
                  pl.BlockSpec(blk, cur), pl.BlockSpec(blk, prev),
                  pl.BlockSpec((4, C_SPAN, 2 * C_SPAN), lambda b, r, i: (0, 0, 0))],
        out_specs=[pl.BlockSpec(blk, lambda b, r, i: (b * nbk + i, r))] * 2,
        out_shape=[jax.ShapeDtypeStruct((rows, dil * C_GROUP_WIDTH), F32)] * 2,
        compiler_params=_params(("arbitrary", "arbitrary", "arbitrary")),
        name=f"attn_prompt_g{g}",
    )(q2, k2, k2, v2, v2, bias)
    return o.reshape(n * seq, C_GROUP_WIDTH), lse.reshape(n * seq, C_GROUP_WIDTH)


def _prompt_bias(rel_bias, g):
    dil = C_GROUPS[g][1]
    qi = np.arange(C_SPAN)[:, None]
    kj = np.arange(2 * C_SPAN)[None, :]
    rel = qi + C_SPAN - kj
    band = (rel >= 0) & (rel <= C_SPAN)
    idx = _t5_bucket_np(np.maximum(rel, 0) * dil)
    tab = rel_bias[:, 4 * g:4 * g + 4]
    return jnp.where(band[None], jnp.transpose(tab[idx], (2, 0, 1)), NEG_BIG)


def _decode_bias(rel_bias, g):
    dil = C_GROUPS[g][1]
    tab = rel_bias[:, 4 * g:4 * g + 4]
    idx_buf = _t5_bucket_np((C_SPAN - np.arange(C_SPAN)) * dil)
    bias_buf = jnp.zeros((SUBLANE, C_SPAN), F32).at[:4].set(tab[idx_buf].T)
    bias_new = jnp.zeros((SUBLANE, LANE), F32).at[:4].set(jnp.broadcast_to(tab[0][:, None], (4, LANE)))
    return bias_buf, bias_new


def _attn_decode_kernel(q_ref, k_ref, v_ref, b0_ref, b1_ref, b2_ref, bb_ref, bn_ref, o_ref, lse_ref):
    b = pl.program_id(0)
    w = C_GROUP_WIDTH
    sel = (lax.broadcasted_iota(jnp.int32, (SUBLANE, w), 0)
           == lax.broadcasted_iota(jnp.int32, (SUBLANE, w), 1) // 64)
    for g, buf_ref in enumerate((b0_ref, b1_ref, b2_ref)):
        qrow = q_ref[pl.ds(b, 1), w * g:w * (g + 1)]
        knew = k_ref[pl.ds(b, 1), w * g:w * (g + 1)]
        vnew = v_ref[pl.ds(b, 1), w * g:w * (g + 1)]
        q8 = jnp.where(sel, jnp.broadcast_to(qrow, (SUBLANE, w)), 0.0)
        kb = buf_ref[0, :, 0:w]
        vb = buf_ref[0, :, w:2 * w]
        s = lax.dot_general(q8, kb, (((1,), (1,)), ((), ())), preferred_element_type=F32,
                            precision=HI) * (64 ** -0.5) + bb_ref[g]
        snew = jnp.sum(q8 * knew, -1, keepdims=True) * (64 ** -0.5) + bn_ref[g][:, 0:1]
        m = jnp.maximum(jnp.max(s, -1, keepdims=True), snew)
        p = jnp.exp(s - m)
        pn = jnp.exp(snew - m)
        den = jnp.sum(p, -1, keepdims=True) + pn
        o8 = _dot_hi(p / den, vb) + (pn / den) * vnew
        lse8 = jnp.broadcast_to(m + jnp.log(den), (SUBLANE, w))
        o_ref[pl.ds(b, 1), w * g:w * (g + 1)] = jnp.sum(jnp.where(sel, o8, 0.0), 0, keepdims=True)
        lse_ref[pl.ds(b, 1), w * g:w * (g + 1)] = jnp.sum(jnp.where(sel, lse8, 0.0), 0, keepdims=True)


def _attn_decode(cq, ck, cv, bufs, bias_buf, bias_new):
    n = cq.shape[0]
    full = lambda shp: pl.BlockSpec(shp, lambda b: (0,) * len(shp))
    bufs3 = [bf.reshape(n, C_SPAN, d * 512) for bf, (_, d) in zip(bufs, C_GROUPS)]
    o, lse = pl.pallas_call(
        _attn_decode_kernel,
        grid=(n,),
        in_specs=[full((n, 768))] * 3 + [pl.BlockSpec((1, C_SPAN, 512), lambda b: (b, 0, 0))] * 3
        + [full((3, SUBLANE, C_SPAN)), full((3, SUBLANE, LANE))],
        out_specs=[full((n, 768))] * 2,
        out_shape=[jax.ShapeDtypeStruct((n, 768), F32)] * 2,
        compiler_params=_params(("arbitrary",)),
        name="attn_decode",
    )(cq, ck, cv, *bufs3, bias_buf, bias_new)
    return o, lse


def _merge_kernel(x_ref, ya_ref, yb_ref, o0_ref, o1_ref, o2_ref, l0_ref, l1_ref, l2_ref, gt_ref,
                  wa_ref, wb_ref, wc_ref, wo_ref, g_ref, b_ref, rw_ref, rb_ref,
                  x1_ref, te_ref, tg_ref):
    l0, l1, l2 = l0_ref[...], l1_ref[...], l2_ref[...]
    lm = jnp.maximum(jnp.maximum(l0, l1), l2)
    e0, e1, e2 = jnp.exp(l0 - lm), jnp.exp(l1 - lm), jnp.exp(l2 - lm)
    den = e0 + e1 + e2
    yc = (e0 / den) * o0_ref[...] + (e1 / den) * o1_ref[...] + (e2 / den) * o2_ref[...]
    ga = _sigmoid(gt_ref[:, 0:D_MODEL])
    gb = _sigmoid(gt_ref[:, D_MODEL:2 * D_MODEL])
    gc = _sigmoid(gt_ref[:, 2 * D_MODEL:3 * D_MODEL])
    merged = (ga * _dot(ya_ref[...], wa_ref[...]) + gb * _dot(yb_ref[...], wb_ref[...])
              + gc * _dot(yc, wc_ref[...]))
    h = DEEPNORM_ALPHA * x_ref[...] + _dot(merged, wo_ref[...])
    x1 = _layernorm(h, g_ref[...], b_ref[...])
    x1_ref[...] = x1
    logits = _dot_hi(x1, rw_ref[...]) + rb_ref[...]
    lane = lax.broadcasted_iota(jnp.int32, logits.shape, 1)
    lane_f = lane.astype(F32)
    te = jnp.zeros(logits.shape, F32)
    vals = []
    for k in range(TOP_K):
        m = jnp.max(logits, -1, keepdims=True)
        idx = jnp.min(jnp.where(logits == m, lane_f, float(LANE)), -1, keepdims=True)
        te = jnp.where(lane == k, idx, te)
        vals.append(m)
        logits = jnp.where(lane_f == idx, -jnp.inf, logits)
    ex = [jnp.exp(v - vals[0]) for v in vals]
    tot = ex[0] + ex[1] + ex[2] + ex[3]
    tg = jnp.zeros(logits.shape, F32)
    for k in range(TOP_K):
        tg = jnp.where(lane == k, ex[k] / tot, tg)
    te_ref[...] = te.astype(jnp.int32)
    tg_ref[...] = tg


def _merge(x, ya, yb, attn, gates, prm, tm):
    t = x.shape[0]
    row = lambda w: pl.BlockSpec((tm, w), lambda i: (i, 0))
    full = lambda a: pl.BlockSpec(a.shape, lambda i: (0,) * a.ndim)
    ws = [prm['w_branch_a'], prm['w_branch_b'], prm['w_branch_c'], prm['w_out'],
          prm['ln1_g'], prm['ln1_b'], prm['router_w'], prm['router_b']]
    (o0, l0), (o1, l1), (o2, l2) = attn
    return pl.pallas_call(
        _merge_kernel,
        grid=(t // tm,),
        in_specs=[row(D_MODEL)] * 3 + [row(C_GROUP_WIDTH)] * 6 + [row(3 * D_MODEL)] + [full(a) for a in ws],
        out_specs=[row(D_MODEL), row(LANE), row(LANE)],
        out_shape=[jax.ShapeDtypeStruct((t, D_MODEL), F32), jax.ShapeDtypeStruct((t, LANE), jnp.int32),
                   jax.ShapeDtypeStruct((t, LANE), F32)],
        compiler_params=_params(("arbitrary",)),
        name="merge_ln_router",
    )(x, ya, yb, o0, o1, o2, l0, l1, l2, gates, *ws)


MOE_TILE = 256


def _moe_kernel(te_ref, nv_ref, tok_ref, slot_ref, h_hbm, w1g_ref, w1l_ref, w2_ref,
                b1g_ref, b1l_ref, b2_ref, y_hbm, xbuf, ybuf, sem_in, sem_out):
    i = pl.program_id(0)
    nv = nv_ref[i]

    @pl.when(i == 0)
    def _():
        xbuf[...] = jnp.zeros(xbuf.shape, F32)

    def row_in(rr, tok):
        return pltpu.make_async_copy(h_hbm.at[pl.ds(tok, 1)], xbuf.at[pl.ds(rr, 1)], sem_in)

    def row_out(rr, slot):
        return pltpu.make_async_copy(ybuf.at[pl.ds(rr, 1)], y_hbm.at[pl.ds(slot, 1)], sem_out)

    @pl.when(nv > 0)
    def _():
        def start_in(rr, c):
            row_in(rr, tok_ref[0, 0, rr]).start()
            return c

        def wait_in(rr, c):
            row_in(rr, 0).wait()
            return c

        lax.fori_loop(0, nv, start_in, 0)
        lax.fori_loop(0, nv, wait_in, 0)
        x = xbuf[...].astype(BF16)
        ug = jnp.dot(x, w1g_ref[0], preferred_element_type=F32) + b1g_ref[0]
        ul = jnp.dot(x, w1l_ref[0], preferred_element_type=F32) + b1l_ref[0]
        glu = jnp.minimum(ug, SWIGLU_LIMIT)
        lin = jnp.clip(ul, -SWIGLU_LIMIT, SWIGLU_LIMIT)
        act = glu * _sigmoid(SWIGLU_ALPHA * glu) * (lin + 1.0)
        ybuf[...] = jnp.dot(act.astype(BF16), w2_ref[0], preferred_element_type=F32) + b2_ref[0]

        def start_out(rr, c):
            row_out(rr, slot_ref[0, 0, rr]).start()
            return c

        def wait_out(rr, c):
            row_out(rr, 0).wait()
            return c

        lax.fori_loop(0, nv, start_out, 0)
        lax.fori_loop(0, nv, wait_out, 0)


def _moe_experts(h_all, top_e, prm):
    t = h_all.shape[0]
    m = t * TOP_K
    tm = MOE_TILE
    n_tiles = -(-m // tm) + N_EXPERTS
    rows = n_tiles * tm
    flat_e = top_e.reshape(-1)
    order = jnp.argsort(flat_e).astype(jnp.int32)
    se = flat_e[order]
    counts = jnp.zeros((N_EXPERTS,), jnp.int32).at[flat_e].add(1)
    padded = (counts + tm - 1) // tm * tm
    start = jnp.cumsum(counts) - counts
    pend = jnp.cumsum(padded)
    pstart = pend - padded
    dest = pstart[se] + jnp.arange(m, dtype=jnp.int32) - start[se]
    row_tok = jnp.zeros((rows,), jnp.int32).at[dest].set(order // TOP_K)
    row_slot = jnp.zeros((rows,), jnp.int32).at[dest].set(order)
    tile_row0 = jnp.arange(n_tiles, dtype=jnp.int32) * tm
    tile_e = jnp.minimum(jnp.searchsorted(pend, tile_row0, side='right'), N_EXPERTS - 1).astype(jnp.int32)
    tile_nv = jnp.clip(pstart[tile_e] + counts[tile_e] - tile_row0, 0, tm).astype(jnp.int32)
    tile_nv = jnp.where(tile_row0 < pend[-1], tile_nv, 0)
    smem_rows = pl.BlockSpec((1, 1, tm), lambda i, te, nv: (i, 0, 0), memory_space=pltpu.SMEM)
    wspec = lambda a: pl.BlockSpec((1,) + a.shape[1:], lambda i, te, nv: (te[i], 0, 0))
    ws = [prm['w1g'], prm['w1l'], prm['w2'], prm['b1g'], prm['b1l'], prm['b2']]
    return pl.pallas_call(
        _moe_kernel,
        grid_spec=pltpu.PrefetchScalarGridSpec(
            num_scalar_prefetch=2,
            grid=(n_tiles,),
            in_specs=[smem_rows, smem_rows, pl.BlockSpec(memory_space=pl.ANY)] + [wspec(a) for a in ws],
            out_specs=pl.BlockSpec(memory_space=pl.ANY),
            scratch_shapes=[pltpu.VMEM((tm, D_MODEL), F32), pltpu.VMEM((tm, D_MODEL), F32),
                            pltpu.SemaphoreType.DMA, pltpu.SemaphoreType.DMA]),
        out_shape=jax.ShapeDtypeStruct((m, D_MODEL), F32),
        compiler_params=_params(("arbitrary",)),
        name="moe_experts",
    )(tile_e, tile_nv, row_tok.reshape(n_tiles, 1, tm), row_slot.reshape(n_tiles, 1, tm), h_all, *ws)


def _combine_kernel(x_ref, y4_ref, tg_ref, g_ref, b_ref, o_ref):
    f = tg_ref[:, 0:1] * y4_ref[:, 0:D_MODEL]
    for k in range(1, TOP_K):
        f = f + tg_ref[:, k:k + 1] * y4_ref[:, D_MODEL * k:D_MODEL * (k + 1)]
    o_ref[...] = _layernorm(DEEPNORM_ALPHA * x_ref[...] + f, g_ref[...], b_ref[...])


def _combine(x1_all, y4, tg_all, prm, row0, t, tm):
    off = row0 // tm
    row = lambda w: pl.BlockSpec((tm, w), lambda i: (i + off, 0))
    full = lambda a: pl.BlockSpec(a.shape, lambda i: (0,) * a.ndim)
    return pl.pallas_call(
        _combine_kernel,
        grid=(t // tm,),
        in_specs=[row(D_MODEL), row(TOP_K * D_MODEL), row(LANE), full(prm['ln2_g']), full(prm['ln2_b'])],
        out_specs=pl.BlockSpec((tm, D_MODEL), lambda i: (i, 0)),
        out_shape=jax.ShapeDtypeStruct((t, D_MODEL), F32),
        compiler_params=_params(("arbitrary",)),
        name="combine_ln",
    )(x1_all, y4.reshape(-1, TOP_K * D_MODEL), tg_all, prm['ln2_g'], prm['ln2_b'])


def _dt_pieces(dt_piece):
    return jnp.concatenate([dt_piece[:, 0:8], dt_piece[:, 128:136]], axis=1)


def _layer(xp, xs, prm, lb, rel_bias, st, n_p, seq, n_s):
    tp = n_p * seq
    up = _in_proj(xp, prm['w_in'], 128)
    dt16 = _dt_pieces(up['dt'])
    dt_t = jnp.transpose(dt16.reshape(tp // CHUNK, CHUNK, A_HEADS), (0, 2, 1))
    zeros_conv = jnp.zeros((n_p, SUBLANE, A_CONV_DIM), F32)
    ya, ssm_p = _ssd(up, dt_t, zeros_conv, jnp.zeros((n_p, D_MODEL, A_STATE), F32), prm, n_p, seq, False)
    yb, hg_p = _hgrn(up, jnp.zeros((n_p, D_MODEL, LANE), F32), lb, prm['hgrn_norm_g'], n_p, seq, False)
    attn = [_attn_prompt(up['cq'], up['ck'], up['cv'], _prompt_bias(rel_bias, g), g, n_p, seq)
            for g in range(3)]
    x1p, tep, tgp = _merge(xp, ya, yb, attn, up['gates'], prm, 256)
    xbc3 = up['xbc'].reshape(n_p, seq, A_CONV_DIM)
    conv_p = xbc3[:, seq - (A_CONV - 1):]
    k4 = up['ck'].reshape(n_p, seq, 3, 4, 64)
    v4 = up['cv'].reshape(n_p, seq, 3, 4, 64)
    kv_p = [jnp.stack([k4[:, seq - min(w, seq):, g], v4[:, seq - min(w, seq):, g]], axis=2)
            for g, (w, _) in enumerate(C_GROUPS)]
    us = _in_proj(xs, prm['w_in'], n_s)
    dt16s = _dt_pieces(us['dt'])
    dt_ts = jnp.zeros((n_s, A_HEADS, CHUNK), F32).at[:, :, 0].set(dt16s)
    conv0 = jnp.pad(st['conv'], ((0, 0), (SUBLANE - (A_CONV - 1), 0), (0, 0)))
    yas, ssm_s = _ssd(us, dt_ts, conv0, st['ssm'].reshape(n_s, D_MODEL, A_STATE), prm, n_s, 1, True)
    ybs, hg_s = _hgrn(us, st['hgrn'].reshape(n_s, D_MODEL, LANE), lb, prm['hgrn_norm_g'], n_s, 1, True)
    bias_d = [_decode_bias(rel_bias, g) for g in range(3)]
    o_s, lse_s = _attn_decode(us['cq'], us['ck'], us['cv'], st['kv'],
                              jnp.stack([b[0] for b in bias_d]), jnp.stack([b[1] for b in bias_d]))
    attn_s = [(o_s[:, 256 * g:256 * (g + 1)], lse_s[:, 256 * g:256 * (g + 1)]) for g in range(3)]
    x1s, tes, tgs = _merge(xs, yas, ybs, attn_s, us['gates'], prm, n_s)
    conv_s = jnp.concatenate([st['conv'][:, 1:], us['xbc'][:, None]], axis=1)
    ks4 = us['ck'].reshape(n_s, 1, 3, 4, 64)
    vs4 = us['cv'].reshape(n_s, 1, 3, 4, 64)
    kv_s = [jnp.stack([ks4[:, :, g], vs4[:, :, g]], axis=2) for g in range(3)]
    x1 = jnp.concatenate([x1p, x1s], axis=0)
    te = jnp.concatenate([tep, tes], axis=0)[:, :TOP_K]
    tg = jnp.concatenate([tgp, tgs], axis=0)
    y4 = _moe_experts(x1, te, prm)
    yp = _combine(x1, y4, tg, prm, 0, tp, 256)
    ys = _combine(x1, y4, tg, prm, tp, n_s, n_s)
    states_p = (conv_p, ssm_p.reshape(n_p, A_HEADS, A_HEAD_DIM, A_STATE),
                hg_p.reshape(n_p, B_HEADS, B_KEY_DIM, LANE), kv_p[0], kv_p[1], kv_p[2])
    states_s = (conv_s, ssm_s.reshape(n_s, A_HEADS, A_HEAD_DIM, A_STATE),
                hg_s.reshape(n_s, B_HEADS, B_KEY_DIM, LANE), kv_s[0], kv_s[1], kv_s[2])
    return yp, ys, states_p, states_s


def _prep_layer(l, w_in, conv_w, conv_b, dt_bias, a_log, d_skip, ssm_norm_g, hgrn_norm_g,
                w_branch_a, w_branch_b, w_branch_c, w_out, ln1_g, ln1_b, router_w, router_b,
                moe_w1, moe_b1, moe_w2, moe_b2, ln2_g, ln2_b):
    def lanes_per_group(v):
        return jnp.zeros((1, 256), F32).at[0, 0:8].set(v[:8]).at[0, 128:136].set(v[8:])

    def sublanes_per_group(v):
        return jnp.broadcast_to(v[:, None], (A_HEADS, LANE))

    return {
        'w_in': _pack_w_in(w_in[l]),
        'conv_w8': jnp.pad(conv_w[l], ((0, SUBLANE - A_CONV), (0, 0))),
        'conv_b': conv_b[l][None],
        'dt_bias_l': lanes_per_group(dt_bias[l]), 'a_log_l': lanes_per_group(a_log[l]),
        'dt_bias_t': sublanes_per_group(dt_bias[l]), 'a_log_t': sublanes_per_group(a_log[l]),
        'd_skip_l': jnp.repeat(d_skip[l], A_HEAD_DIM)[None],
        'ssm_norm_g': ssm_norm_g[l][None],
        'hgrn_norm_g': hgrn_norm_g[l][None],
        'w_branch_a': w_branch_a[l].astype(BF16), 'w_branch_b': w_branch_b[l].astype(BF16),
        'w_branch_c': w_branch_c[l].astype(BF16), 'w_out': w_out[l].astype(BF16),
        'ln1_g': ln1_g[l][None], 'ln1_b': ln1_b[l][None],
        'router_w': jnp.pad(router_w[l], ((0, 0), (0, LANE - N_EXPERTS))),
        'router_b': jnp.pad(router_b[l], (0, LANE - N_EXPERTS), constant_values=-jnp.inf)[None],
        'w1g': moe_w1[l][:, :, 0::2].astype(BF16), 'w1l': moe_w1[l][:, :, 1::2].astype(BF16),
        'w2': moe_w2[l].astype(BF16),
        'b1g': moe_b1[l][:, None, 0::2], 'b1l': moe_b1[l][:, None, 1::2], 'b2': moe_b2[l][:, None, :],
        'ln2_g': ln2_g[l][None], 'ln2_b': ln2_b[l][None],
    }


def kernel(x_prompt, x_sample, state_conv, state_ssm, state_hgrn, cache_kv_w128, cache_kv_w512, cache_kv_w2048, w_in, conv_w, conv_b, dt_bias, a_log, d_skip, ssm_norm_g, hgrn_lb, hgrn_norm_g, rel_bias, w_branch_a, w_branch_b, w_branch_c, w_out, ln1_g, ln1_b, router_w, router_b, moe_w1, moe_b1, moe_w2, moe_b2, ln2_g, ln2_b):
    n_p, seq, _ = x_prompt.shape
    n_s = x_sample.shape[0]
    depth = w_in.shape[0]
    p_lb = jax.nn.softmax(hgrn_lb.astype(F32), axis=0)
    lower_bounds = jnp.cumsum(p_lb, axis=0) - p_lb[0]
    yp = x_prompt.reshape(n_p * seq, D_MODEL)
    ys = x_sample.reshape(n_s, D_MODEL)
    st_p, st_s = [], []
    for l in range(depth):
        prm = _prep_layer(l, w_in, conv_w, conv_b, dt_bias, a_log, d_skip, ssm_norm_g, hgrn_norm_g,
                          w_branch_a, w_branch_b, w_branch_c, w_out, ln1_g, ln1_b, router_w, router_b,
                          moe_w1, moe_b1, moe_w2, moe_b2, ln2_g, ln2_b)
        st = {'conv': state_conv[l], 'ssm': state_ssm[l], 'hgrn': state_hgrn[l],
              'kv': (cache_kv_w128[l], cache_kv_w512[l], cache_kv_w2048[l])}
        yp, ys, sp, ss = _layer(yp, ys, prm, lower_bounds[l][None], rel_bias, st, n_p, seq, n_s)
        st_p.append(sp)
        st_s.append(ss)
    stack = lambda sts, i: jnp.stack([s[i] for s in sts], axis=0)
    return (yp.reshape(n_p, seq, D_MODEL), ys.reshape(n_s, 1, D_MODEL),
            stack(st_p, 0), stack(st_s, 0), stack(st_p, 1), stack(st_s, 1), stack(st_p, 2), stack(st_s, 2),
            stack(st_p, 3), stack(st_s, 3), stack(st_p, 4), stack(st_s, 4), stack(st_p, 5), stack(st_s, 5))
```

```python
import functools
import math

import jax
import jax.numpy as jnp
import numpy as np
from jax import lax
from jax.experimental import pallas as pl
from jax.experimental.pallas import tpu as pltpu

F32 = jnp.float32
BF16 = jnp.bfloat16
HI = lax.Precision.HIGHEST

D_MODEL = 1024
A_HEADS = 16
A_HEAD_DIM = 64
A_STATE = 128
A_CONV = 4
A_CONV_DIM = 1536
B_HEADS = 8
B_KEY_DIM = 128
C_GROUPS = ((128, 1), (512, 4), (2048, 16))
C_SPAN = 128
C_GROUP_WIDTH = 256
REL_BUCKETS = 32
REL_MAX_DIST = 2048
N_EXPERTS = 32
TOP_K = 4
SWIGLU_ALPHA = 1.702
SWIGLU_LIMIT = 7.0
DEEPNORM_ALPHA = (2.0 * 2) ** 0.25
LN_EPS = 1e-5
RMS_EPS = 1e-5
NEG_BIG = -1e30
LOG_FLOOR = 1e-30

LANE = 128
SUBLANE = 8
CHUNK = 128
SUB = 16
VMEM_LIMIT = 56 * 1024 * 1024

IN_PIECES = (('z', 1024), ('xbc', 1536), ('bq', 1024), ('bf', 1024), ('bi', 1024), ('bg', 1024),
             ('cq', 768), ('ck', 768), ('cv', 768), ('gates', 3072), ('dt', 256))
IN_PACKED = sum(w for _, w in IN_PIECES)


def _params(sem):
    return pltpu.CompilerParams(dimension_semantics=sem, vmem_limit_bytes=VMEM_LIMIT)


def _sigmoid(x):
    return 1.0 / (1.0 + jnp.exp(-x))


def _softplus(x):
    return jnp.maximum(x, 0.0) + jnp.log(1.0 + jnp.exp(-jnp.abs(x)))


def _dot(a, b):
    return jnp.dot(a.astype(BF16), b.astype(BF16), preferred_element_type=F32)


def _dot_nt(a, b):
    return lax.dot_general(a.astype(BF16), b.astype(BF16), (((1,), (1,)), ((), ())),
                           preferred_element_type=F32)


def _dot_hi(a, b):
    return jnp.dot(a, b, preferred_element_type=F32, precision=HI)


def _layernorm(h, g, b):
    mu = jnp.mean(h, -1, keepdims=True)
    c = h - mu
    var = jnp.mean(c * c, -1, keepdims=True)
    return c * lax.rsqrt(var + LN_EPS) * g + b


def _in_proj_kernel(x_ref, w_ref, *o_refs):
    xb = x_ref[...].astype(BF16)
    off = 0
    for o_ref in o_refs:
        wd = o_ref.shape[1]
        o_ref[...] = jnp.dot(xb, w_ref[:, off:off + wd], preferred_element_type=F32)
        off += wd


def _in_proj(x, w_packed, tm):
    t = x.shape[0]
    outs = pl.pallas_call(
        _in_proj_kernel,
        grid=(t // tm,),
        in_specs=[pl.BlockSpec((tm, D_MODEL), lambda i: (i, 0)),
                  pl.BlockSpec(memory_space=pltpu.VMEM)],
        out_specs=[pl.BlockSpec((tm, w), lambda i: (i, 0)) for _, w in IN_PIECES],
        out_shape=[jax.ShapeDtypeStruct((t, w), F32) for _, w in IN_PIECES],
        compiler_params=_params(("arbitrary",)),
        name="in_proj",
    )(x, w_packed)
    return {name: o for (name, _), o in zip(IN_PIECES, outs)}


def _pack_w_in(w):
    dt = w[:, 2560:2576]
    dtp = jnp.zeros((D_MODEL, 256), F32).at[:, 0:8].set(dt[:, :8]).at[:, 128:136].set(dt[:, 8:])
    return jnp.concatenate([w[:, :2560], w[:, 2576:], dtp], axis=1).astype(BF16)


def _stage(ref, ci, decode, r):
    if decode:
        row = ref[pl.ds(r, 1), :]
        rows = lax.broadcasted_iota(jnp.int32, (CHUNK, row.shape[1]), 0)
        return jnp.where(rows == 0, jnp.broadcast_to(row, (CHUNK, row.shape[1])), 0.0)
    return ref[pl.ds(pl.multiple_of(ci * CHUNK, CHUNK), CHUNK), :]


def _unstage(ref, val, ci, decode, r):
    if decode:
        ref[pl.ds(r, 1), :] = val[0:1, :]
    else:
        ref[pl.ds(pl.multiple_of(ci * CHUNK, CHUNK), CHUNK), :] = val


def _ssd_kernel(xs_ref, bm_ref, cm_ref, z_ref, dt_ref, dtt_ref,
                c0x_ref, c0b_ref, c0c_ref, h0_ref,
                wx_ref, wb_ref, wc_ref, bx_ref, bb_ref, bc_ref,
                dtb_ref, alog_ref, dtbt_ref, alogt_ref, dsk_ref, ng_ref,
                y_ref, hn_ref,
                h_scr, cx_scr, cb_scr, cc_scr, px_scr, pb_scr, pc_scr, *, nchunks, decode):
    b = pl.program_id(1)
    t = pl.program_id(2)
    r = b % SUBLANE
    q = CHUNK

    @pl.when(t == 0)
    def _():
        h_scr[...] = h0_ref[0]
        cx_scr[...] = c0x_ref[0]
        cb_scr[...] = c0b_ref[0]
        cc_scr[...] = c0c_ref[0]

    rows = lax.broadcasted_iota(jnp.int32, (q, q), 0)
    cols = lax.broadcasted_iota(jnp.int32, (q, q), 1)
    tril = rows >= cols
    tril_f = tril.astype(F32)
    triu_f = (rows <= cols).astype(F32)
    lane = lax.broadcasted_iota(jnp.int32, (q, LANE), 1)
    lo = lane < A_HEAD_DIM
    row_lo = lax.broadcasted_iota(jnp.int32, (LANE, LANE), 0) < A_HEAD_DIM
    valid_col = lax.broadcasted_iota(jnp.int32, (q, LANE), 0) == 0
    valid_row = lax.broadcasted_iota(jnp.int32, (SUBLANE, q), 1) == 0

    def conv(x, carry, pad, w_ref, b_ref):
        pad[0:SUBLANE, :] = carry[...]
        pad[SUBLANE:SUBLANE + q, :] = x
        if not decode:
            carry[...] = pad[q:q + SUBLANE, :]
        acc = b_ref[...]
        for j in range(A_CONV):
            acc = acc + pad[5 + j:5 + j + q, :] * w_ref[j:j + 1, :]
        return acc * _sigmoid(acc)

    def chunk(ci, carry_unused):
        xs = conv(_stage(xs_ref, ci, decode, r), cx_scr, px_scr, wx_ref, bx_ref)
        bm = conv(_stage(bm_ref, ci, decode, r), cb_scr, pb_scr, wb_ref, bb_ref)
        cm = conv(_stage(cm_ref, ci, decode, r), cc_scr, pc_scr, wc_ref, bc_ref)
        a_lane = -jnp.exp(alog_ref[...])
        a_sub = -jnp.exp(alogt_ref[...])
        dt = _softplus(_stage(dt_ref, ci, decode, r) + dtb_ref[...])
        dtt = _softplus(dtt_ref[ci] + dtbt_ref[...])
        if decode:
            dt = jnp.where(valid_col, dt, 0.0)
            dtt = jnp.where(valid_row, dtt, 0.0)
        acum = _dot_hi(tril_f, dt * a_lane)
        acum_t = _dot_hi(dtt * a_sub, triu_f)
        a_last = acum[q - 1:q, :]
        cb = _dot_nt(cm, bm)
        cm_b = cm.astype(BF16)
        bm_b = bm.astype(BF16)
        ys = []
        for j in range(4):
            xp = xs[:, LANE * j:LANE * (j + 1)]
            xp_b = xp.astype(BF16)
            yd = []
            for half in range(2):
                hl = 2 * j + half
                diff = acum[:, hl:hl + 1] - acum_t[hl:hl + 1, :]
                lm = jnp.exp(jnp.where(tril, diff, NEG_BIG))
                m = cb * lm * dtt[hl:hl + 1, :]
                yd.append(jnp.dot(m.astype(BF16), xp_b, preferred_element_type=F32))
            y = jnp.where(lo, yd[0], yd[1])
            hp = h_scr[LANE * j:LANE * (j + 1), :]
            yo = lax.dot_general(cm_b, hp.astype(BF16), (((1,), (1,)), ((), ())),
                                 preferred_element_type=F32)
            e0 = acum[:, 2 * j:2 * j + 1]
            e1 = acum[:, 2 * j + 1:2 * j + 2]
            y = y + yo * jnp.exp(jnp.where(lo, e0, e1))
            w0 = dt[:, 2 * j:2 * j + 1] * jnp.exp(a_last[:, 2 * j:2 * j + 1] - e0)
            w1 = dt[:, 2 * j + 1:2 * j + 2] * jnp.exp(a_last[:, 2 * j + 1:2 * j + 2] - e1)
            xw = xp * jnp.where(lo, w0, w1)
            upd = jnp.dot(xw.T.astype(BF16), bm_b, preferred_element_type=F32)
            dec = jnp.exp(jnp.where(row_lo, a_last[:, 2 * j:2 * j + 1], a_last[:, 2 * j + 1:2 * j + 2]))
            h_scr[LANE * j:LANE * (j + 1), :] = hp * dec + upd
            ys.append(y)
        y = jnp.concatenate(ys, axis=1) + dsk_ref[...] * xs
        z = _stage(z_ref, ci, decode, r)
        y = y * (z * _sigmoid(z))
        y = y * lax.rsqrt(jnp.mean(y * y, -1, keepdims=True) + RMS_EPS) * ng_ref[...]
        _unstage(y_ref, y, ci, decode, r)
        return carry_unused

    lax.fori_loop(0, nchunks, chunk, 0)

    @pl.when(t == pl.num_programs(2) - 1)
    def _():
        hn_ref[0] = h_scr[...]


def _ssd(u, dt_t, conv0_8, h0, prm, n, seq, decode):
    tb = CHUNK if decode else min(512, seq)
    nb = 1 if decode else seq // tb
    nchunks = tb // CHUNK
    rb = SUBLANE if decode else tb

    def rowmap(lane_block):
        if decode:
            return lambda g, b, t: (b // SUBLANE, lane_block(g))
        return lambda g, b, t: (b * nb + t, lane_block(g))

    chunk_map = (lambda g, b, t: (b, g, 0)) if decode else (lambda g, b, t: (b * nb + t, g, 0))
    cw = (512, LANE, LANE)
    lane_blocks = (lambda g: g, lambda g: 8 + g, lambda g: 10 + g)
    in_specs = [
        pl.BlockSpec((rb, 512), rowmap(lane_blocks[0])),
        pl.BlockSpec((rb, LANE), rowmap(lane_blocks[1])),
        pl.BlockSpec((rb, LANE), rowmap(lane_blocks[2])),
        pl.BlockSpec((rb, 512), rowmap(lambda g: g)),
        pl.BlockSpec((rb, LANE), rowmap(lambda g: g)),
        pl.BlockSpec((nchunks, SUBLANE, CHUNK), chunk_map),
    ]
    in_specs += [pl.BlockSpec((1, SUBLANE, w), (lambda lb: (lambda g, b, t: (b, 0, lb(g))))(lb))
                 for w, lb in zip(cw, lane_blocks)]
    in_specs += [pl.BlockSpec((1, 512, A_STATE), lambda g, b, t: (b, g, 0))]
    in_specs += [pl.BlockSpec((SUBLANE, w), (lambda lb: (lambda g, b, t: (0, lb(g))))(lb))
                 for w, lb in zip(cw, lane_blocks)]
    in_specs += [pl.BlockSpec((1, w), (lambda lb: (lambda g, b, t: (0, lb(g))))(lb))
                 for w, lb in zip(cw, lane_blocks)]
    in_specs += [
        pl.BlockSpec((1, LANE), lambda g, b, t: (0, g)),
        pl.BlockSpec((1, LANE), lambda g, b, t: (0, g)),
        pl.BlockSpec((SUBLANE, LANE), lambda g, b, t: (g, 0)),
        pl.BlockSpec((SUBLANE, LANE), lambda g, b, t: (g, 0)),
        pl.BlockSpec((1, 512), lambda g, b, t: (0, g)),
        pl.BlockSpec((1, 512), lambda g, b, t: (0, g)),
    ]
    rows_total = u['z'].shape[0]
    y, hn = pl.pallas_call(
        functools.partial(_ssd_kernel, nchunks=nchunks, decode=decode),
        grid=(2, n, nb),
        in_specs=in_specs,
        out_specs=[pl.BlockSpec((rb, 512), rowmap(lambda g: g)),
                   pl.BlockSpec((1, 512, A_STATE), lambda g, b, t: (b, g, 0))],
        out_shape=[jax.ShapeDtypeStruct((rows_total, D_MODEL), F32),
                   jax.ShapeDtypeStruct((n, D_MODEL, A_STATE), F32)],
        scratch_shapes=[pltpu.VMEM((512, A_STATE), F32),
                        pltpu.VMEM((SUBLANE, 512), F32), pltpu.VMEM((SUBLANE, LANE), F32),
                        pltpu.VMEM((SUBLANE, LANE), F32),
                        pltpu.VMEM((CHUNK + SUBLANE, 512), F32), pltpu.VMEM((CHUNK + SUBLANE, LANE), F32),
                        pltpu.VMEM((CHUNK + SUBLANE, LANE), F32)],
        compiler_params=_params(("arbitrary", "arbitrary", "arbitrary")),
        name="ssd_decode" if decode else "ssd_prompt",
    )(u['xbc'], u['xbc'], u['xbc'], u['z'], u['dt'], dt_t,
      conv0_8, conv0_8, conv0_8, h0,
      prm['conv_w8'], prm['conv_w8'], prm['conv_w8'], prm['conv_b'], prm['conv_b'], prm['conv_b'],
      prm['dt_bias_l'], prm['a_log_l'], prm['dt_bias_t'], prm['a_log_t'], prm['d_skip_l'], prm['ssm_norm_g'])
    return y, hn


def _hgrn_kernel(q_ref, f_ref, i_ref, g_ref, lb_ref, ng_ref, s0_ref, y_ref, sn_ref, st_scr,
                 *, nchunks, decode):
    b = pl.program_id(1)
    t = pl.program_id(2)
    r = b % SUBLANE
    c = CHUNK
    nsub = c // SUB

    @pl.when(t == 0)
    def _():
        st_scr[...] = s0_ref[0].T

    rows = lax.broadcasted_iota(jnp.int32, (c, c), 0)
    cols = lax.broadcasted_iota(jnp.int32, (c, c), 1)
    tril_f = (rows >= cols).astype(F32)
    row_c = lax.broadcasted_iota(jnp.int32, (c, LANE), 0)
    row_s = lax.broadcasted_iota(jnp.int32, (SUB, LANE), 0)
    lbv = lb_ref[...]
    live_sub = 1 if decode else nsub
    live_s = 1 if decode else SUB

    def chunk(ci, carry_unused):
        f = _stage(f_ref, ci, decode, r)
        qr = _stage(q_ref, ci, decode, r)
        v = _stage(i_ref, ci, decode, r)
        gr = _stage(g_ref, ci, decode, r)
        q = qr * _sigmoid(qr) * (B_KEY_DIM ** -0.5)
        logf = jnp.log(jnp.maximum(lbv + (1.0 - lbv) * _sigmoid(f), LOG_FLOOR))
        k = (1.0 - lbv) * _sigmoid(-f)
        if decode:
            logf = jnp.where(row_c == 0, logf, 0.0)
            k = jnp.where(row_c == 0, k, 0.0)
        bcum = _dot_hi(tril_f, logf)
        b_last = bcum[c - 1:c, :]
        st = st_scr[...]
        o = _dot_nt(q * jnp.exp(bcum), st)
        if live_sub > 1:
            att_rows = [jnp.zeros((SUB, c), F32)]
            qa = []
            for i in range(1, nsub):
                bsi = bcum[SUB * i - 1:SUB * i, :]
                ki = jnp.where(row_c < SUB * i, k * jnp.exp(jnp.minimum(bsi - bcum, 0.0)), 0.0)
                ai = q[SUB * i:SUB * (i + 1), :] * jnp.exp(bcum[SUB * i:SUB * (i + 1), :] - bsi)
                att_rows.append(_dot_nt(ai, ki))
            o = o + _dot(jnp.concatenate(att_rows, axis=0), v)
        diag = []
        for i in range(nsub):
            if i >= live_sub:
                diag.append(jnp.zeros((SUB, LANE), F32))
                continue
            sl = slice(SUB * i, SUB * (i + 1))
            qb, kb, vb, bb = q[sl, :], k[sl, :], v[sl, :], bcum[sl, :]
            od = jnp.zeros((SUB, LANE), F32)
            for s in range(live_s):
                e = jnp.exp(jnp.where(row_s >= s, bb - bb[s:s + 1, :], NEG_BIG))
                rs = jnp.sum(qb * kb[s:s + 1, :] * e, -1, keepdims=True)
                od = od + rs * vb[s:s + 1, :]
            diag.append(od)
        o = o + jnp.concatenate(diag, axis=0)
        k2 = k * jnp.exp(b_last - bcum)
        st_scr[...] = jnp.exp(b_last) * st + jnp.dot(v.T.astype(BF16), k2.astype(BF16),
                                                      preferred_element_type=F32)
        gate = gr * _sigmoid(gr)
        y = o * lax.rsqrt(jnp.mean(o * o, -1, keepdims=True) + RMS_EPS) * ng_ref[...] * gate
        _unstage(y_ref, y, ci, decode, r)
        return carry_unused

    lax.fori_loop(0, nchunks, chunk, 0)

    @pl.when(t == pl.num_programs(2) - 1)
    def _():
        sn_ref[0] = st_scr[...].T


def _hgrn(u, s0, lb, ng, n, seq, decode):
    tb = CHUNK if decode else min(512, seq)
    nb = 1 if decode else seq // tb
    nchunks = tb // CHUNK
    rb = SUBLANE if decode else tb
    rowmap = (lambda h, b, t: (b // SUBLANE, h)) if decode else (lambda h, b, t: (b * nb + t, h))
    rows_total = u['bq'].shape[0]
    y, sn = pl.pallas_call(
        functools.partial(_hgrn_kernel, nchunks=nchunks, decode=decode),
        grid=(B_HEADS, n, nb),
        in_specs=[pl.BlockSpec((rb, LANE), rowmap)] * 4 + [
            pl.BlockSpec((1, LANE), lambda h, b, t: (0, h)),
            pl.BlockSpec((1, LANE), lambda h, b, t: (0, 0)),
            pl.BlockSpec((1, B_KEY_DIM, LANE), lambda h, b, t: (b, h, 0))],
        out_specs=[pl.BlockSpec((rb, LANE), rowmap),
                   pl.BlockSpec((1, B_KEY_DIM, LANE), lambda h, b, t: (b, h, 0))],
        out_shape=[jax.ShapeDtypeStruct((rows_total, D_MODEL), F32),
                   jax.ShapeDtypeStruct((n, D_MODEL, LANE), F32)],
        scratch_shapes=[pltpu.VMEM((LANE, B_KEY_DIM), F32)],
        compiler_params=_params(("arbitrary", "arbitrary", "arbitrary")),
        name="hgrn_decode" if decode else "hgrn_prompt",
    )(u['bq'], u['bf'], u['bi'], u['bg'], lb, ng, s0)
    return y, sn


def _t5_bucket_np(dist):
    exact = REL_BUCKETS // 2
    d = np.maximum(dist, 1).astype(np.float32)
    large = exact + (np.log(d / np.float32(exact)) / np.float32(math.log(REL_MAX_DIST / exact))
                     * np.float32(REL_BUCKETS - exact)).astype(np.int32)
    large = np.clip(large, 0, REL_BUCKETS - 1)
    return np.where(dist < exact, dist, large)


def _attn_prompt_kernel(q_ref, kc_ref, kp_ref, vc_ref, vp_ref, bias_ref, o_ref, lse_ref, *, dil):
    first = pl.program_id(1) == 0
    s_q = C_SPAN
    lane = lax.broadcasted_iota(jnp.int32, (s_q, LANE), 1)
    lo = lane < 64
    kcol = lax.broadcasted_iota(jnp.int32, (s_q, 2 * s_q), 1)
    no_prev = jnp.logical_and(first, kcol < s_q)

    pair = pl.program_id(2)

    def residue(r, carry):
        sl = pl.ds(r, s_q, stride=dil) if dil > 1 else slice(None)
        qp = q_ref[sl, :]
        kp2 = jnp.concatenate([kp_ref[sl, :], kc_ref[sl, :]], axis=0).astype(BF16)
        vp2 = jnp.concatenate([vp_ref[sl, :], vc_ref[sl, :]], axis=0).astype(BF16)
        oh, lh = [], []
        for half in range(2):
            qm = jnp.where(lo if half == 0 else jnp.logical_not(lo), qp, 0.0).astype(BF16)
            s = lax.dot_general(qm, kp2, (((1,), (1,)), ((), ())), preferred_element_type=F32)
            s = s * (64 ** -0.5) + bias_ref[2 * pair + half]
            s = jnp.where(no_prev, NEG_BIG, s)
            m = jnp.max(s, -1, keepdims=True)
            p = jnp.exp(s - m)
            den = jnp.sum(p, -1, keepdims=True)
            oh.append(jnp.dot((p / den).astype(BF16), vp2, preferred_element_type=F32))
            lh.append(m + jnp.log(den))
        o_ref[sl, :] = jnp.where(lo, oh[0], oh[1])
        lse_ref[sl, :] = jnp.where(lo, lh[0], lh[1])
        return carry

    if dil == 1:
        residue(0, 0)
    else:
        lax.fori_loop(0, dil, residue, 0)


def _attn_prompt(cq, ck, cv, bias, g, n, seq):
    dil = C_GROUPS[g][1]
    rows_blk = C_SPAN * dil
    nbk = seq // rows_blk
    cur = lambda b, i, j: (b * nbk + i, 2 * g + j)
    prev = lambda b, i, j: (b * nbk + jnp.maximum(i - 1, 0), 2 * g + j)
    blk = (rows_blk, LANE)
    return pl.pallas_call(
        functools.partial(_attn_prompt_kernel, dil=dil),
        grid=(n, nbk, 2),
        in_specs=[pl.BlockSpec(blk, cur), pl.BlockSpec(blk, cur), pl.BlockSpec(blk, prev),
                  pl.BlockSpec(blk, cur), pl.BlockSpec(blk, prev),
                  pl.BlockSpec((4, C_SPAN, 2 * C_SPAN), lambda b, i, j: (0, 0, 0))],
        out_specs=[pl.BlockSpec(blk, lambda b, i, j: (b * nbk + i, j))] * 2,
        out_shape=[jax.ShapeDtypeStruct((n * seq, C_GROUP_WIDTH), F32)] * 2,
        compiler_params=_params(("arbitrary", "arbitrary", "arbitrary")),
        name=f"attn_prompt_g{g}",
    )(cq, ck, ck, cv, cv, bias)


def _prompt_bias(rel_bias, g):
    dil = C_GROUPS[g][1]
    qi = np.arange(C_SPAN)[:, None]
    kj = np.arange(2 * C_SPAN)[None, :]
    rel = qi + C_SPAN - kj
    band = (rel >= 0) & (rel <= C_SPAN)
    idx = _t5_bucket_np(np.maximum(rel, 0) * dil)
    tab = rel_bias[:, 4 * g:4 * g + 4]
    return jnp.where(band[None], jnp.transpose(tab[idx], (2, 0, 1)), NEG_BIG)


def _decode_bias(rel_bias, g):
    win, dil = C_GROUPS[g]
    tab = rel_bias[:, 4 * g:4 * g + 4]
    pos = np.arange(win)
    vals = jnp.where((pos % dil == 0)[:, None], tab[_t5_bucket_np(win - pos)], NEG_BIG)
    bias_buf = jnp.zeros((2, SUBLANE, win), F32).at[:, 0:2].set(vals.T.reshape(2, 2, win))
    new = jnp.broadcast_to(tab[0].reshape(2, 2, 1), (2, 2, LANE))
    bias_new = jnp.zeros((2, SUBLANE, LANE), F32).at[:, 0:2].set(new)
    return bias_buf, bias_new


def _attn_decode_kernel(q_ref, k_ref, v_ref, c0_ref, c1_ref, c2_ref, bb0_ref, bb1_ref, bb2_ref, bn_ref,
                        o_ref, lse_ref):
    b = pl.program_id(0)
    row8 = lax.broadcasted_iota(jnp.int32, (SUBLANE, LANE), 0)
    lane8 = lax.broadcasted_iota(jnp.int32, (SUBLANE, LANE), 1)
    qmask = jnp.logical_or(jnp.logical_and(row8 == 0, lane8 < 64), jnp.logical_and(row8 == 1, lane8 >= 64))
    lo1 = lax.broadcasted_iota(jnp.int32, (1, LANE), 1) < 64
    q_all = q_ref[pl.ds(b, 1), :]
    k_all = k_ref[pl.ds(b, 1), :]
    v_all = v_ref[pl.ds(b, 1), :]
    o_parts, lse_parts = [], []
    for g, (buf_ref, bb_ref) in enumerate(((c0_ref, bb0_ref), (c1_ref, bb1_ref), (c2_ref, bb2_ref))):
        win = C_GROUPS[g][0]
        for j in range(2):
            c0 = C_GROUP_WIDTH * g + LANE * j
            qrow = q_all[:, c0:c0 + LANE]
            knew = k_all[:, c0:c0 + LANE]
            vnew = v_all[:, c0:c0 + LANE]
            q8 = jnp.where(qmask, jnp.broadcast_to(qrow, (SUBLANE, LANE)), 0.0)
            kt = buf_ref[0, 0, 0, 2 * j:2 * j + 2].reshape(LANE, win)
            vt = buf_ref[0, 0, 1, 2 * j:2 * j + 2].reshape(LANE, win)
            s = _dot(q8, kt) * (64 ** -0.5) + bb_ref[j]
            snew = jnp.sum(q8 * knew, -1, keepdims=True) * (64 ** -0.5) + bn_ref[g, j][:, 0:1]
            m = jnp.maximum(jnp.max(s, -1, keepdims=True), snew)
            p = jnp.exp(s - m)
            pn = jnp.exp(snew - m)
            den = jnp.sum(p, -1, keepdims=True) + pn
            o8 = _dot_nt(p / den, vt) + (pn / den) * vnew
            lse8 = jnp.broadcast_to(m + jnp.log(den), (SUBLANE, LANE))
            o_parts.append(jnp.where(lo1, o8[0:1, :], o8[1:2, :]))
            lse_parts.append(jnp.where(lo1, lse8[0:1, :], lse8[1:2, :]))
    o_ref[pl.ds(b, 1), :] = jnp.concatenate(o_parts, axis=1)
    lse_ref[pl.ds(b, 1), :] = jnp.concatenate(lse_parts, axis=1)


def _attn_decode(cq, ck, cv, caches_t, layer, bias_bufs, bias_new):
    n = cq.shape[0]
    full = lambda a: pl.BlockSpec(a.shape, lambda b: (0,) * a.ndim)
    cache_spec = lambda c: pl.BlockSpec((1, 1) + c.shape[2:], lambda b: (layer, b, 0, 0, 0, 0))
    return pl.pallas_call(
        _attn_decode_kernel,
        grid=(n,),
        in_specs=[full(cq), full(ck), full(cv)] + [cache_spec(c) for c in caches_t]
        + [full(bb) for bb in bias_bufs] + [full(bias_new)],
        out_specs=[pl.BlockSpec((n, 768), lambda b: (0, 0))] * 2,
        out_shape=[jax.ShapeDtypeStruct((n, 768), F32)] * 2,
        compiler_params=_params(("arbitrary",)),
        name="attn_decode",
    )(cq, ck, cv, *caches_t, *bias_bufs, bias_new)


def _merge_kernel(x_ref, ya_ref, yb_ref, o0_ref, o1_ref, o2_ref, l0_ref, l1_ref, l2_ref, gt_ref,
                  wa_ref, wb_ref, wc_ref, wo_ref, g_ref, b_ref, rw_ref, rb_ref, cnt0_ref,
                  x1_ref, te_ref, tg_ref, rk_ref, cnt_ref, cnt_scr):
    l0, l1, l2 = l0_ref[...], l1_ref[...], l2_ref[...]
    lm = jnp.maximum(jnp.maximum(l0, l1), l2)
    e0, e1, e2 = jnp.exp(l0 - lm), jnp.exp(l1 - lm), jnp.exp(l2 - lm)
    den = e0 + e1 + e2
    yc = (e0 / den) * o0_ref[...] + (e1 / den) * o1_ref[...] + (e2 / den) * o2_ref[...]
    ga = _sigmoid(gt_ref[:, 0:D_MODEL])
    gb = _sigmoid(gt_ref[:, D_MODEL:2 * D_MODEL])
    gc = _sigmoid(gt_ref[:, 2 * D_MODEL:3 * D_MODEL])
    merged = (ga * _dot(ya_ref[...], wa_ref[...]) + gb * _dot(yb_ref[...], wb_ref[...])
              + gc * _dot(yc, wc_ref[...]))
    h = DEEPNORM_ALPHA * x_ref[...] + _dot(merged, wo_ref[...])
    x1 = _layernorm(h, g_ref[...], b_ref[...])
    x1_ref[...] = x1
    logits = _dot_hi(x1, rw_ref[...]) + rb_ref[...]
    lane = lax.broadcasted_iota(jnp.int32, logits.shape, 1)
    lane_f = lane.astype(F32)
    te = jnp.zeros(logits.shape, F32)
    vals, onehots = [], []
    for k in range(TOP_K):
        m = jnp.max(logits, -1, keepdims=True)
        idx = jnp.min(jnp.where(logits == m, lane_f, float(LANE)), -1, keepdims=True)
        te = jnp.where(lane == k, idx, te)
        vals.append(m)
        hit = lane_f == idx
        onehots.append(hit.astype(F32))
        logits = jnp.where(hit, -jnp.inf, logits)
    ex = [jnp.exp(v - vals[0]) for v in vals]
    tot = ex[0] + ex[1] + ex[2] + ex[3]
    tg = jnp.zeros(logits.shape, F32)
    for k in range(TOP_K):
        tg = jnp.where(lane == k, ex[k] / tot, tg)
    te_ref[...] = te.astype(jnp.int32)
    tg_ref[...] = tg
    @pl.when(pl.program_id(0) == 0)
    def _():
        cnt_scr[...] = cnt0_ref[...]

    tm = logits.shape[0]
    oh = onehots[0] + onehots[1] + onehots[2] + onehots[3]
    earlier = (lax.broadcasted_iota(jnp.int32, (tm, tm), 0)
               > lax.broadcasted_iota(jnp.int32, (tm, tm), 1)).astype(BF16)
    before = jnp.dot(earlier, oh.astype(BF16), preferred_element_type=F32) + cnt_scr[...]
    rank = jnp.zeros(logits.shape, F32)
    for k in range(TOP_K):
        rank = jnp.where(lane == k, jnp.sum(onehots[k] * before, -1, keepdims=True), rank)
    rk_ref[...] = rank.astype(jnp.int32)
    cnt_scr[...] = cnt_scr[...] + jnp.sum(oh, 0, keepdims=True)
    cnt_ref[...] = cnt_scr[...]


def _merge(x, ya, yb, attn, gates, cnt0, prm, tm):
    t = x.shape[0]
    row = lambda w: pl.BlockSpec((tm, w), lambda i: (i, 0))
    full = lambda a: pl.BlockSpec(a.shape, lambda i: (0,) * a.ndim)
    ws = [prm['w_branch_a'], prm['w_branch_b'], prm['w_branch_c'], prm['w_out'],
          prm['ln1_g'], prm['ln1_b'], prm['router_w'], prm['router_b'], cnt0]
    (o0, l0), (o1, l1), (o2, l2) = attn
    return pl.pallas_call(
        _merge_kernel,
        grid=(t // tm,),
        in_specs=[row(D_MODEL)] * 3 + [row(C_GROUP_WIDTH)] * 6 + [row(3 * D_MODEL)] + [full(a) for a in ws],
        out_specs=[row(D_MODEL), row(LANE), row(LANE), row(LANE), pl.BlockSpec((1, LANE), lambda i: (0, 0))],
        out_shape=[jax.ShapeDtypeStruct((t, D_MODEL), F32), jax.ShapeDtypeStruct((t, LANE), jnp.int32),
                   jax.ShapeDtypeStruct((t, LANE), F32), jax.ShapeDtypeStruct((t, LANE), jnp.int32),
                   jax.ShapeDtypeStruct((1, LANE), F32)],
        scratch_shapes=[pltpu.VMEM((1, LANE), F32)],
        compiler_params=_params(("arbitrary",)),
        name="merge_ln_router",
    )(x, ya, yb, o0, o1, o2, l0, l1, l2, gates, *ws)


MOE_TILE = 256


DMA_UNROLL = 8


def _dispatch_kernel(dest_ref, x_ref, xs_in_hbm, xs_hbm, sem, *, tm):
    del xs_in_hbm

    def row_copy(r, slot):
        return pltpu.make_async_copy(x_ref.at[pl.ds(r, 1)], xs_hbm.at[pl.ds(slot, 1)], sem)

    def start(r, c):
        for k in range(TOP_K):
            row_copy(r, dest_ref[0, 0, TOP_K * r + k]).start()
        return c

    def wait(r, c):
        for k in range(TOP_K):
            row_copy(0, 0).wait()
        return c

    lax.fori_loop(0, tm, start, 0, unroll=DMA_UNROLL)
    lax.fori_loop(0, tm, wait, 0, unroll=DMA_UNROLL)


def _dispatch(x1, dest, xs, tm):
    t = x1.shape[0]
    return pl.pallas_call(
        functools.partial(_dispatch_kernel, tm=tm),
        grid=(t // tm,),
        in_specs=[pl.BlockSpec((1, 1, TOP_K * tm), lambda i: (i, 0, 0), memory_space=pltpu.SMEM),
                  pl.BlockSpec((tm, D_MODEL), lambda i: (i, 0)),
                  pl.BlockSpec(memory_space=pl.ANY)],
        out_specs=pl.BlockSpec(memory_space=pl.ANY),
        out_shape=jax.ShapeDtypeStruct(xs.shape, F32),
        scratch_shapes=[pltpu.SemaphoreType.DMA],
        input_output_aliases={2: 0},
        compiler_params=_params(("arbitrary",)),
        name="moe_dispatch",
    )(dest.reshape(t // tm, 1, TOP_K * tm), x1, xs)


def _expert_kernel(te_ref, nv_ref, x_ref, w1_ref, w2_ref, b1g_ref, b1l_ref, b2_ref, y_ref, w1p_scr, w2b_scr):
    i = pl.program_id(0)
    nv = nv_ref[i]
    changed = jnp.logical_or(i == 0, te_ref[i] != te_ref[jnp.maximum(i - 1, 0)])
    half = LANE
    blk = 2 * LANE
    d_ff = w2b_scr.shape[0]

    @pl.when(jnp.logical_and(nv > 0, changed))
    def _():
        src_r = lax.broadcasted_iota(jnp.int32, (blk, blk), 0)
        dst_c = lax.broadcasted_iota(jnp.int32, (blk, blk), 1)
        pick = jnp.where(dst_c < half, 2 * dst_c, 2 * (dst_c - half) + 1)
        perm = (src_r == pick).astype(BF16)
        for c in range(2 * d_ff // blk):
            wb = w1_ref[0, 0, :, blk * c:blk * (c + 1)].astype(BF16)
            w1p_scr[:, blk * c:blk * (c + 1)] = jnp.dot(wb, perm, preferred_element_type=F32).astype(BF16)
        w2b_scr[...] = w2_ref[0, 0].astype(BF16)

    @pl.when(nv > 0)
    def _():
        x = x_ref[...].astype(BF16)
        u = jnp.dot(x, w1p_scr[...], preferred_element_type=F32)
        nblk = 2 * d_ff // blk
        ug = jnp.concatenate([u[:, blk * c:blk * c + half] for c in range(nblk)], axis=1) + b1g_ref[0, 0]
        ul = jnp.concatenate([u[:, blk * c + half:blk * (c + 1)] for c in range(nblk)], axis=1) + b1l_ref[0, 0]
        glu = jnp.minimum(ug, SWIGLU_LIMIT)
        lin = jnp.clip(ul, -SWIGLU_LIMIT, SWIGLU_LIMIT)
        act = glu * _sigmoid(SWIGLU_ALPHA * glu) * (lin + 1.0)
        y_ref[...] = jnp.dot(act.astype(BF16), w2b_scr[...], preferred_element_type=F32) + b2_ref[0, 0]

    @pl.when(nv == 0)
    def _():
        y_ref[...] = jnp.zeros(y_ref.shape, F32)


def _moe_plan(counts, n_tiles):
    tm = MOE_TILE
    padded = (counts + tm - 1) // tm * tm
    pend = jnp.cumsum(padded)
    pstart = pend - padded
    tile_row0 = jnp.arange(n_tiles, dtype=jnp.int32) * tm
    tile_e = jnp.minimum(jnp.sum(tile_row0[:, None] >= pend[None, :], axis=1), N_EXPERTS - 1).astype(jnp.int32)
    tile_nv = jnp.clip(pstart[tile_e] + counts[tile_e] - tile_row0, 0, tm)
    tile_nv = jnp.where(tile_row0 < pend[-1], tile_nv, 0).astype(jnp.int32)
    return pstart, tile_e, tile_nv


def _slots(top_e, rank, pstart):
    experts = jnp.arange(N_EXPERTS, dtype=jnp.int32)
    base = jnp.sum(jnp.where(top_e[:, :TOP_K, None] == experts, pstart, 0), axis=-1)
    return (base + rank[:, :TOP_K]).astype(jnp.int32)


def _moe_experts(xs, tile_e, tile_nv, layer, w1, w2, b1g, b1l, b2):
    tm = MOE_TILE
    n_tiles = xs.shape[0] // tm
    d_ff = w2.shape[2]
    wspec = lambda a: pl.BlockSpec((1, 1) + a.shape[2:], lambda i, te, nv: (layer, te[i], 0, 0))
    return pl.pallas_call(
        _expert_kernel,
        grid_spec=pltpu.PrefetchScalarGridSpec(
            num_scalar_prefetch=2,
            grid=(n_tiles,),
            in_specs=[pl.BlockSpec((tm, D_MODEL), lambda i, te, nv: (i, 0))]
            + [wspec(a) for a in (w1, w2, b1g, b1l, b2)],
            out_specs=pl.BlockSpec((tm, D_MODEL), lambda i, te, nv: (i, 0)),
            scratch_shapes=[pltpu.VMEM((D_MODEL, 2 * d_ff), BF16), pltpu.VMEM((d_ff, D_MODEL), BF16)]),
        out_shape=jax.ShapeDtypeStruct(xs.shape, F32),
        compiler_params=_params(("arbitrary",)),
        name="moe_experts",
    )(tile_e, tile_nv, xs, w1, w2, b1g, b1l, b2)


def _combine_kernel(dest_ref, x_ref, tg_ref, g_ref, b_ref, ys_hbm, o_ref, ybuf, sem, *, tm):
    def row_copy(r, k, slot):
        return pltpu.make_async_copy(ys_hbm.at[pl.ds(slot, 1)], ybuf.at[k, pl.ds(r, 1)], sem)

    def start(r, c):
        for k in range(TOP_K):
            row_copy(r, k, dest_ref[0, 0, TOP_K * r + k]).start()
        return c

    def wait(r, c):
        for k in range(TOP_K):
            row_copy(0, k, 0).wait()
        return c

    lax.fori_loop(0, tm, start, 0, unroll=DMA_UNROLL)
    lax.fori_loop(0, tm, wait, 0, unroll=DMA_UNROLL)
    f = tg_ref[:, 0:1] * ybuf[0]
    for k in range(1, TOP_K):
        f = f + tg_ref[:, k:k + 1] * ybuf[k]
    o_ref[...] = _layernorm(DEEPNORM_ALPHA * x_ref[...] + f, g_ref[...], b_ref[...])


def _combine(x1, tg, dest, ys, prm, tm):
    t = x1.shape[0]
    row = lambda w: pl.BlockSpec((tm, w), lambda i: (i, 0))
    full = lambda a: pl.BlockSpec(a.shape, lambda i: (0,) * a.ndim)
    return pl.pallas_call(
        functools.partial(_combine_kernel, tm=tm),
        grid=(t // tm,),
        in_specs=[pl.BlockSpec((1, 1, TOP_K * tm), lambda i: (i, 0, 0), memory_space=pltpu.SMEM),
                  row(D_MODEL), row(LANE), full(prm['ln2_g']), full(prm['ln2_b']),
                  pl.BlockSpec(memory_space=pl.ANY)],
        out_specs=row(D_MODEL),
        out_shape=jax.ShapeDtypeStruct((t, D_MODEL), F32),
        scratch_shapes=[pltpu.VMEM((TOP_K, tm, D_MODEL), F32), pltpu.SemaphoreType.DMA],
        compiler_params=_params(("arbitrary",)),
        name="combine_ln",
    )(dest.reshape(t // tm, 1, TOP_K * tm), x1, tg, prm['ln2_g'], prm['ln2_b'], ys)


def _dt_pieces(dt_piece):
    return jnp.concatenate([dt_piece[:, 0:8], dt_piece[:, 128:136]], axis=1)


def _layer(xp, xs, prm, lb, rel_bias, st, n_p, seq, n_s, layer, moe):
    tp = n_p * seq
    up = _in_proj(xp, prm['w_in'], 128)
    dt16 = _dt_pieces(up['dt'])
    dt_t = jnp.transpose(dt16.reshape(tp // CHUNK, CHUNK, A_HEADS), (0, 2, 1))
    zeros_conv = jnp.zeros((n_p, SUBLANE, A_CONV_DIM), F32)
    ya, ssm_p = _ssd(up, dt_t, zeros_conv, jnp.zeros((n_p, D_MODEL, A_STATE), F32), prm, n_p, seq, False)
    yb, hg_p = _hgrn(up, jnp.zeros((n_p, D_MODEL, LANE), F32), lb, prm['hgrn_norm_g'], n_p, seq, False)
    attn = [_attn_prompt(up['cq'], up['ck'], up['cv'], _prompt_bias(rel_bias, g), g, n_p, seq)
            for g in range(3)]
    x1p, tep, tgp, rkp, cnt_p = _merge(xp, ya, yb, attn, up['gates'], jnp.zeros((1, LANE), F32), prm, 256)
    xbc3 = up['xbc'].reshape(n_p, seq, A_CONV_DIM)
    conv_p = xbc3[:, seq - (A_CONV - 1):]
    k4 = up['ck'].reshape(n_p, seq, 3, 4, 64)
    v4 = up['cv'].reshape(n_p, seq, 3, 4, 64)
    kv_p = [jnp.stack([k4[:, seq - min(w, seq):, g], v4[:, seq - min(w, seq):, g]], axis=2)
            for g, (w, _) in enumerate(C_GROUPS)]
    us = _in_proj(xs, prm['w_in'], n_s)
    dt16s = _dt_pieces(us['dt'])
    dt_ts = jnp.zeros((n_s, A_HEADS, CHUNK), F32).at[:, :, 0].set(dt16s)
    conv0 = jnp.pad(st['conv'], ((0, 0), (SUBLANE - (A_CONV - 1), 0), (0, 0)))
    yas, ssm_s = _ssd(us, dt_ts, conv0, st['ssm'].reshape(n_s, D_MODEL, A_STATE), prm, n_s, 1, True)
    ybs, hg_s = _hgrn(us, st['hgrn'].reshape(n_s, D_MODEL, LANE), lb, prm['hgrn_norm_g'], n_s, 1, True)
    bias_d = [_decode_bias(rel_bias, g) for g in range(3)]
    o_s, lse_s = _attn_decode(us['cq'], us['ck'], us['cv'], st['kv_t'], layer,
                              [b[0] for b in bias_d], jnp.stack([b[1] for b in bias_d]))
    attn_s = [(o_s[:, 256 * g:256 * (g + 1)], lse_s[:, 256 * g:256 * (g + 1)]) for g in range(3)]
    x1s, tes, tgs, rks, cnt = _merge(xs, yas, ybs, attn_s, us['gates'], cnt_p, prm, n_s)
    conv_s = jnp.concatenate([st['conv'][:, 1:], us['xbc'][:, None]], axis=1)
    ks4 = us['ck'].reshape(n_s, 1, 3, 4, 64)
    vs4 = us['cv'].reshape(n_s, 1, 3, 4, 64)
    kv_s = [jnp.stack([ks4[:, :, g], vs4[:, :, g]], axis=2) for g in range(3)]
    n_tiles = -(-(tp + n_s) * TOP_K // MOE_TILE) + N_EXPERTS
    pstart, tile_e, tile_nv = _moe_plan(cnt[0, :N_EXPERTS].astype(jnp.int32), n_tiles)
    dest_p = _slots(tep, rkp, pstart)
    dest_s = _slots(tes, rks, pstart)
    x_sorted = jnp.zeros((n_tiles * MOE_TILE, D_MODEL), F32)
    x_sorted = _dispatch(x1p, dest_p, x_sorted, 256)
    x_sorted = _dispatch(x1s, dest_s, x_sorted, n_s)
    y_sorted = _moe_experts(x_sorted, tile_e, tile_nv, layer, *moe)
    yp = _combine(x1p, tgp, dest_p, y_sorted, prm, 256)
    ys = _combine(x1s, tgs, dest_s, y_sorted, prm, n_s)
    states_p = (conv_p, ssm_p.reshape(n_p, A_HEADS, A_HEAD_DIM, A_STATE),
                hg_p.reshape(n_p, B_HEADS, B_KEY_DIM, LANE), kv_p[0], kv_p[1], kv_p[2])
    states_s = (conv_s, ssm_s.reshape(n_s, A_HEADS, A_HEAD_DIM, A_STATE),
                hg_s.reshape(n_s, B_HEADS, B_KEY_DIM, LANE), kv_s[0], kv_s[1], kv_s[2])
    return yp, ys, states_p, states_s


def _prep_layer(l, w_in, conv_w, conv_b, dt_bias, a_log, d_skip, ssm_norm_g, hgrn_norm_g,
                w_branch_a, w_branch_b, w_branch_c, w_out, ln1_g, ln1_b, router_w, router_b,
                moe_w1, moe_b1, moe_w2, moe_b2, ln2_g, ln2_b):
    def lanes_per_group(v):
        return jnp.zeros((1, 256), F32).at[0, 0:8].set(v[:8]).at[0, 128:136].set(v[8:])

    def sublanes_per_group(v):
        return jnp.broadcast_to(v[:, None], (A_HEADS, LANE))

    return {
        'w_in': _pack_w_in(w_in[l]),
        'conv_w8': jnp.pad(conv_w[l], ((0, SUBLANE - A_CONV), (0, 0))),
        'conv_b': conv_b[l][None],
        'dt_bias_l': lanes_per_group(dt_bias[l]), 'a_log_l': lanes_per_group(a_log[l]),
        'dt_bias_t': sublanes_per_group(dt_bias[l]), 'a_log_t': sublanes_per_group(a_log[l]),
        'd_skip_l': jnp.repeat(d_skip[l], A_HEAD_DIM)[None],
        'ssm_norm_g': ssm_norm_g[l][None],
        'hgrn_norm_g': hgrn_norm_g[l][None],
        'w_branch_a': w_branch_a[l].astype(BF16), 'w_branch_b': w_branch_b[l].astype(BF16),
        'w_branch_c': w_branch_c[l].astype(BF16), 'w_out': w_out[l].astype(BF16),
        'ln1_g': ln1_g[l][None], 'ln1_b': ln1_b[l][None],
        'router_w': jnp.pad(router_w[l], ((0, 0), (0, LANE - N_EXPERTS))),
        'router_b': jnp.pad(router_b[l], (0, LANE - N_EXPERTS), constant_values=-jnp.inf)[None],
        'ln2_g': ln2_g[l][None], 'ln2_b': ln2_b[l][None],
    }


def kernel(x_prompt, x_sample, state_conv, state_ssm, state_hgrn, cache_kv_w128, cache_kv_w512, cache_kv_w2048, w_in, conv_w, conv_b, dt_bias, a_log, d_skip, ssm_norm_g, hgrn_lb, hgrn_norm_g, rel_bias, w_branch_a, w_branch_b, w_branch_c, w_out, ln1_g, ln1_b, router_w, router_b, moe_w1, moe_b1, moe_w2, moe_b2, ln2_g, ln2_b):
    n_p, seq, _ = x_prompt.shape
    n_s = x_sample.shape[0]
    depth = w_in.shape[0]
    p_lb = jax.nn.softmax(hgrn_lb.astype(F32), axis=0)
    lower_bounds = jnp.cumsum(p_lb, axis=0) - p_lb[0]
    yp = x_prompt.reshape(n_p * seq, D_MODEL)
    ys = x_sample.reshape(n_s, D_MODEL)
    st_p, st_s = [], []
    kv_t = tuple(jnp.transpose(c, (0, 1, 3, 4, 5, 2)) for c in (cache_kv_w128, cache_kv_w512, cache_kv_w2048))
    moe = (moe_w1, moe_w2, moe_b1[:, :, None, 0::2], moe_b1[:, :, None, 1::2], moe_b2[:, :, None, :])
    for l in range(depth):
        prm = _prep_layer(l, w_in, conv_w, conv_b, dt_bias, a_log, d_skip, ssm_norm_g, hgrn_norm_g,
                          w_branch_a, w_branch_b, w_branch_c, w_out, ln1_g, ln1_b, router_w, router_b,
                          moe_w1, moe_b1, moe_w2, moe_b2, ln2_g, ln2_b)
        st = {'conv': state_conv[l], 'ssm': state_ssm[l], 'hgrn': state_hgrn[l], 'kv_t': kv_t}
        yp, ys, sp, ss = _layer(yp, ys, prm, lower_bounds[l][None], rel_bias, st, n_p, seq, n_s, l, moe)
        st_p.append(sp)
        st_s.append(ss)
    stack = lambda sts, i: jnp.stack([s[i] for s in sts], axis=0)
    return (yp.reshape(n_p, seq, D_MODEL), ys.reshape(n_s, 1, D_MODEL),
            stack(st_p, 0), stack(st_s, 0), stack(st_p, 1), stack(st_s, 1), stack(st_p, 2), stack(st_s, 2),
            stack(st_p, 3), stack(st_s, 3), stack(st_p, 4), stack(st_s, 4), stack(st_p, 5), stack(st_s, 5))
```

```python
import functools
import math

import jax
import jax.numpy as jnp
import numpy as np
from jax import lax
from jax.experimental import pallas as pl
from jax.experimental.pallas import tpu as pltpu

F32 = jnp.float32
BF16 = jnp.bfloat16
HI = lax.Precision.HIGHEST

D_MODEL = 1024
A_HEADS = 16
A_HEAD_DIM = 64
A_STATE = 128
A_CONV = 4
A_CONV_DIM = 1536
B_HEADS = 8
B_KEY_DIM = 128
C_GROUPS = ((128, 1), (512, 4), (2048, 16))
C_SPAN = 128
C_GROUP_WIDTH = 256
REL_BUCKETS = 32
REL_MAX_DIST = 2048
N_EXPERTS = 32
TOP_K = 4
SWIGLU_ALPHA = 1.702
SWIGLU_LIMIT = 7.0
DEEPNORM_ALPHA = (2.0 * 2) ** 0.25
LN_EPS = 1e-5
RMS_EPS = 1e-5
NEG_BIG = -1e30
LOG_FLOOR = 1e-30

LANE = 128
SUBLANE = 8
CHUNK = 128
SUB = 8
HGRN_HEADS_PER_STEP = 2
ATTN_ILP = 4
VMEM_LIMIT = 56 * 1024 * 1024

IN_PIECES = (('z', 1024), ('xbc', 1536), ('bq', 1024), ('bf', 1024), ('bi', 1024), ('bg', 1024),
             ('cq', 768), ('ck', 768), ('cv', 768), ('gates', 3072), ('dt', 256))
IN_PACKED = sum(w for _, w in IN_PIECES)


def _params(sem):
    return pltpu.CompilerParams(dimension_semantics=sem, vmem_limit_bytes=VMEM_LIMIT)


def _sigmoid(x):
    return 1.0 / (1.0 + jnp.exp(-x))


def _softplus(x):
    return jnp.maximum(x, 0.0) + jnp.log(1.0 + jnp.exp(-jnp.abs(x)))


def _dot(a, b):
    return jnp.dot(a.astype(BF16), b.astype(BF16), preferred_element_type=F32)


def _dot_nt(a, b):
    return lax.dot_general(a.astype(BF16), b.astype(BF16), (((1,), (1,)), ((), ())),
                           preferred_element_type=F32)


def _dot_hi(a, b):
    return jnp.dot(a, b, preferred_element_type=F32, precision=HI)


def _layernorm(h, g, b):
    mu = jnp.mean(h, -1, keepdims=True)
    c = h - mu
    var = jnp.mean(c * c, -1, keepdims=True)
    return c * lax.rsqrt(var + LN_EPS) * g + b


def _in_proj_kernel(x_ref, w_ref, *o_refs):
    xb = x_ref[...].astype(BF16)
    off = 0
    for o_ref in o_refs:
        wd = o_ref.shape[1]
        o_ref[...] = jnp.dot(xb, w_ref[:, off:off + wd], preferred_element_type=F32)
        off += wd


def _in_proj(x, w_packed, tm):
    t = x.shape[0]
    outs = pl.pallas_call(
        _in_proj_kernel,
        grid=(t // tm,),
        in_specs=[pl.BlockSpec((tm, D_MODEL), lambda i: (i, 0)),
                  pl.BlockSpec(memory_space=pltpu.VMEM)],
        out_specs=[pl.BlockSpec((tm, w), lambda i: (i, 0)) for _, w in IN_PIECES],
        out_shape=[jax.ShapeDtypeStruct((t, w), F32) for _, w in IN_PIECES],
        compiler_params=_params(("arbitrary",)),
        name="in_proj",
    )(x, w_packed)
    return {name: o for (name, _), o in zip(IN_PIECES, outs)}


def _pack_w_in(w):
    dt = w[:, 2560:2576]
    dtp = jnp.zeros((D_MODEL, 256), F32).at[:, 0:8].set(dt[:, :8]).at[:, 128:136].set(dt[:, 8:])
    return jnp.concatenate([w[:, :2560], w[:, 2576:], dtp], axis=1).astype(BF16)


def _stage(ref, ci, decode, r):
    if decode:
        row = ref[pl.ds(r, 1), :]
        rows = lax.broadcasted_iota(jnp.int32, (CHUNK, row.shape[1]), 0)
        return jnp.where(rows == 0, jnp.broadcast_to(row, (CHUNK, row.shape[1])), 0.0)
    return ref[pl.ds(pl.multiple_of(ci * CHUNK, CHUNK), CHUNK), :]


def _unstage(ref, val, ci, decode, r):
    if decode:
        ref[pl.ds(r, 1), :] = val[0:1, :]
    else:
        ref[pl.ds(pl.multiple_of(ci * CHUNK, CHUNK), CHUNK), :] = val


def _ssd_kernel(xs_ref, bm_ref, cm_ref, z_ref, dt_ref, dtt_ref,
                c0x_ref, c0b_ref, c0c_ref, h0_ref,
                wx_ref, wb_ref, wc_ref, bx_ref, bb_ref, bc_ref,
                dtb_ref, alog_ref, dtbt_ref, alogt_ref, dsk_ref, ng_ref,
                y_ref, hn_ref,
                h_scr, cx_scr, cb_scr, cc_scr, px_scr, pb_scr, pc_scr, *, nchunks, decode):
    b = pl.program_id(1)
    t = pl.program_id(2)
    r = b % SUBLANE
    q = CHUNK

    @pl.when(t == 0)
    def _():
        h_scr[...] = h0_ref[0]
        cx_scr[...] = c0x_ref[0]
        cb_scr[...] = c0b_ref[0]
        cc_scr[...] = c0c_ref[0]

    rows = lax.broadcasted_iota(jnp.int32, (q, q), 0)
    cols = lax.broadcasted_iota(jnp.int32, (q, q), 1)
    tril = rows >= cols
    tril_f = tril.astype(F32)
    triu_f = (rows <= cols).astype(F32)
    lane = lax.broadcasted_iota(jnp.int32, (q, LANE), 1)
    lo = lane < A_HEAD_DIM
    row_lo = lax.broadcasted_iota(jnp.int32, (LANE, LANE), 0) < A_HEAD_DIM
    valid_col = lax.broadcasted_iota(jnp.int32, (q, LANE), 0) == 0
    valid_row = lax.broadcasted_iota(jnp.int32, (SUBLANE, q), 1) == 0

    def conv(x, carry, pad, w_ref, b_ref):
        pad[0:SUBLANE, :] = carry[...]
        pad[SUBLANE:SUBLANE + q, :] = x
        if not decode:
            carry[...] = pad[q:q + SUBLANE, :]
        acc = b_ref[...]
        for j in range(A_CONV):
            acc = acc + pad[5 + j:5 + j + q, :] * w_ref[j:j + 1, :]
        return acc * _sigmoid(acc)

    def chunk(ci, carry_unused):
        xs = conv(_stage(xs_ref, ci, decode, r), cx_scr, px_scr, wx_ref, bx_ref)
        bm = conv(_stage(bm_ref, ci, decode, r), cb_scr, pb_scr, wb_ref, bb_ref)
        cm = conv(_stage(cm_ref, ci, decode, r), cc_scr, pc_scr, wc_ref, bc_ref)
        a_lane = -jnp.exp(alog_ref[...])
        a_sub = -jnp.exp(alogt_ref[...])
        dt = _softplus(_stage(dt_ref, ci, decode, r) + dtb_ref[...])
        dtt = _softplus(dtt_ref[ci] + dtbt_ref[...])
        if decode:
            dt = jnp.where(valid_col, dt, 0.0)
            dtt = jnp.where(valid_row, dtt, 0.0)
        acum = _dot_hi(tril_f, dt * a_lane)
        acum_t = _dot_hi(dtt * a_sub, triu_f)
        a_last = acum[q - 1:q, :]
        cb = _dot_nt(cm, bm)
        cm_b = cm.astype(BF16)
        bm_b = bm.astype(BF16)
        ys = []
        for j in range(4):
            xp = xs[:, LANE * j:LANE * (j + 1)]
            xp_b = xp.astype(BF16)
            yd = []
            for half in range(2):
                hl = 2 * j + half
                diff = acum[:, hl:hl + 1] - acum_t[hl:hl + 1, :]
                lm = jnp.exp(jnp.where(tril, diff, NEG_BIG))
                m = cb * lm * dtt[hl:hl + 1, :]
                yd.append(jnp.dot(m.astype(BF16), xp_b, preferred_element_type=F32))
            y = jnp.where(lo, yd[0], yd[1])
            hp = h_scr[LANE * j:LANE * (j + 1), :]
            yo = lax.dot_general(cm_b, hp.astype(BF16), (((1,), (1,)), ((), ())),
                                 preferred_element_type=F32)
            e0 = acum[:, 2 * j:2 * j + 1]
            e1 = acum[:, 2 * j + 1:2 * j + 2]
            y = y + yo * jnp.exp(jnp.where(lo, e0, e1))
            w0 = dt[:, 2 * j:2 * j + 1] * jnp.exp(a_last[:, 2 * j:2 * j + 1] - e0)
            w1 = dt[:, 2 * j + 1:2 * j + 2] * jnp.exp(a_last[:, 2 * j + 1:2 * j + 2] - e1)
            xw = xp * jnp.where(lo, w0, w1)
            upd = jnp.dot(xw.T.astype(BF16), bm_b, preferred_element_type=F32)
            dec = jnp.exp(jnp.where(row_lo, a_last[:, 2 * j:2 * j + 1], a_last[:, 2 * j + 1:2 * j + 2]))
            h_scr[LANE * j:LANE * (j + 1), :] = hp * dec + upd
            ys.append(y)
        y = jnp.concatenate(ys, axis=1) + dsk_ref[...] * xs
        z = _stage(z_ref, ci, decode, r)
        y = y * (z * _sigmoid(z))
        y = y * lax.rsqrt(jnp.mean(y * y, -1, keepdims=True) + RMS_EPS) * ng_ref[...]
        _unstage(y_ref, y, ci, decode, r)
        return carry_unused

    lax.fori_loop(0, nchunks, chunk, 0)

    @pl.when(t == pl.num_programs(2) - 1)
    def _():
        hn_ref[0] = h_scr[...]


def _ssd(u, dt_t, conv0_8, h0, prm, n, seq, decode):
    tb = CHUNK if decode else min(512, seq)
    nb = 1 if decode else seq // tb
    nchunks = tb // CHUNK
    rb = SUBLANE if decode else tb

    def rowmap(lane_block):
        if decode:
            return lambda g, b, t: (b // SUBLANE, lane_block(g))
        return lambda g, b, t: (b * nb + t, lane_block(g))

    chunk_map = (lambda g, b, t: (b, g, 0)) if decode else (lambda g, b, t: (b * nb + t, g, 0))
    cw = (512, LANE, LANE)
    lane_blocks = (lambda g: g, lambda g: 8 + g, lambda g: 10 + g)
    in_specs = [
        pl.BlockSpec((rb, 512), rowmap(lane_blocks[0])),
        pl.BlockSpec((rb, LANE), rowmap(lane_blocks[1])),
        pl.BlockSpec((rb, LANE), rowmap(lane_blocks[2])),
        pl.BlockSpec((rb, 512), rowmap(lambda g: g)),
        pl.BlockSpec((rb, LANE), rowmap(lambda g: g)),
        pl.BlockSpec((nchunks, SUBLANE, CHUNK), chunk_map),
    ]
    in_specs += [pl.BlockSpec((1, SUBLANE, w), (lambda lb: (lambda g, b, t: (b, 0, lb(g))))(lb))
                 for w, lb in zip(cw, lane_blocks)]
    in_specs += [pl.BlockSpec((1, 512, A_STATE), lambda g, b, t: (b, g, 0))]
    in_specs += [pl.BlockSpec((SUBLANE, w), (lambda lb: (lambda g, b, t: (0, lb(g))))(lb))
                 for w, lb in zip(cw, lane_blocks)]
    in_specs += [pl.BlockSpec((1, w), (lambda lb: (lambda g, b, t: (0, lb(g))))(lb))
                 for w, lb in zip(cw, lane_blocks)]
    in_specs += [
        pl.BlockSpec((1, LANE), lambda g, b, t: (0, g)),
        pl.BlockSpec((1, LANE), lambda g, b, t: (0, g)),
        pl.BlockSpec((SUBLANE, LANE), lambda g, b, t: (g, 0)),
        pl.BlockSpec((SUBLANE, LANE), lambda g, b, t: (g, 0)),
        pl.BlockSpec((1, 512), lambda g, b, t: (0, g)),
        pl.BlockSpec((1, 512), lambda g, b, t: (0, g)),
    ]
    rows_total = u['z'].shape[0]
    y, hn = pl.pallas_call(
        functools.partial(_ssd_kernel, nchunks=nchunks, decode=decode),
        grid=(2, n, nb),
        in_specs=in_specs,
        out_specs=[pl.BlockSpec((rb, 512), rowmap(lambda g: g)),
                   pl.BlockSpec((1, 512, A_STATE), lambda g, b, t: (b, g, 0))],
        out_shape=[jax.ShapeDtypeStruct((rows_total, D_MODEL), F32),
                   jax.ShapeDtypeStruct((n, D_MODEL, A_STATE), F32)],
        scratch_shapes=[pltpu.VMEM((512, A_STATE), F32),
                        pltpu.VMEM((SUBLANE, 512), F32), pltpu.VMEM((SUBLANE, LANE), F32),
                        pltpu.VMEM((SUBLANE, LANE), F32),
                        pltpu.VMEM((CHUNK + SUBLANE, 512), F32), pltpu.VMEM((CHUNK + SUBLANE, LANE), F32),
                        pltpu.VMEM((CHUNK + SUBLANE, LANE), F32)],
        compiler_params=_params(("arbitrary", "arbitrary", "arbitrary")),
        name="ssd_decode" if decode else "ssd_prompt",
    )(u['xbc'], u['xbc'], u['xbc'], u['z'], u['dt'], dt_t,
      conv0_8, conv0_8, conv0_8, h0,
      prm['conv_w8'], prm['conv_w8'], prm['conv_w8'], prm['conv_b'], prm['conv_b'], prm['conv_b'],
      prm['dt_bias_l'], prm['a_log_l'], prm['dt_bias_t'], prm['a_log_t'], prm['d_skip_l'], prm['ssm_norm_g'])
    return y, hn


def _hgrn_kernel(q_ref, f_ref, i_ref, g_ref, lb_ref, ng_ref, s0_ref, y_ref, sn_ref, st_scr,
                 *, nchunks, decode, nheads):
    b = pl.program_id(1)
    t = pl.program_id(2)
    r = b % SUBLANE
    c = CHUNK

    @pl.when(t == 0)
    def _():
        for hh in range(nheads):
            st_scr[hh] = s0_ref[0, LANE * hh:LANE * (hh + 1), :].T

    rows = lax.broadcasted_iota(jnp.int32, (c, c), 0)
    cols = lax.broadcasted_iota(jnp.int32, (c, c), 1)
    tril_f = (rows >= cols).astype(F32)
    row_c = lax.broadcasted_iota(jnp.int32, (c, LANE), 0)
    row_s = lax.broadcasted_iota(jnp.int32, (SUB, LANE), 0)
    blk_xor = (rows // SUB) ^ (cols // SUB)
    level = jnp.where(blk_xor >= 8, 3, jnp.where(blk_xor >= 4, 2, jnp.where(blk_xor >= 2, 1, 0)))
    level = jnp.where((rows // SUB) > (cols // SUB), level, -1)
    nlevels = 4
    assert SUB << nlevels == c

    def one_head(hh, ci):
        ls = slice(LANE * hh, LANE * (hh + 1))
        lbv = lb_ref[:, ls]
        f = _stage(f_ref, ci, decode, r)[:, ls]
        qr = _stage(q_ref, ci, decode, r)[:, ls]
        v = _stage(i_ref, ci, decode, r)[:, ls]
        gr = _stage(g_ref, ci, decode, r)[:, ls]
        q = qr * _sigmoid(qr) * (B_KEY_DIM ** -0.5)
        e_f = jnp.exp(-jnp.abs(f))
        r_f = 1.0 / (1.0 + e_f)
        pos = f >= 0.0
        sig_p = jnp.where(pos, r_f, e_f * r_f)
        sig_n = jnp.where(pos, e_f * r_f, r_f)
        logf = jnp.log(jnp.maximum(lbv + (1.0 - lbv) * sig_p, LOG_FLOOR))
        k = (1.0 - lbv) * sig_n
        if decode:
            logf = jnp.where(row_c == 0, logf, 0.0)
            k = jnp.where(row_c == 0, k, 0.0)
        bcum = _dot_hi(tril_f, logf)
        b_last = bcum[c - 1:c, :]
        st = st_scr[hh]
        o = _dot_nt(q * jnp.exp(bcum), st)
        if not decode:
            att = jnp.zeros((c, c), F32)
            for lv in range(nlevels):
                h = SUB << lv
                ref = jnp.concatenate(
                    [jnp.broadcast_to(bcum[2 * h * m + h - 1:2 * h * m + h, :], (2 * h, LANE))
                     for m in range(c // (2 * h))], axis=0)
                x = jnp.exp(-jnp.abs(bcum - ref))
                att = jnp.where(level == lv, _dot_nt(q * x, k * x), att)
            o = o + _dot(att, v)
        diag = []
        for i in range(c // SUB):
            if decode and i > 0:
                diag.append(jnp.zeros((SUB, LANE), F32))
                continue
            sl = slice(SUB * i, SUB * (i + 1))
            qb, kb, vb, bb = q[sl, :], k[sl, :], v[sl, :], bcum[sl, :]
            od = jnp.zeros((SUB, LANE), F32)
            for s in range(1 if decode else SUB):
                e = jnp.exp(jnp.where(row_s >= s, bb - bb[s:s + 1, :], NEG_BIG))
                rs = jnp.sum(qb * kb[s:s + 1, :] * e, -1, keepdims=True)
                od = od + rs * vb[s:s + 1, :]
            diag.append(od)
        o = o + jnp.concatenate(diag, axis=0)
        k2 = k * jnp.exp(b_last - bcum)
        st_scr[hh] = jnp.exp(b_last) * st + jnp.dot(v.T.astype(BF16), k2.astype(BF16),
                                                     preferred_element_type=F32)
        gate = gr * _sigmoid(gr)
        return o * lax.rsqrt(jnp.mean(o * o, -1, keepdims=True) + RMS_EPS) * ng_ref[...] * gate

    def chunk(ci, carry_unused):
        y = [one_head(hh, ci) for hh in range(nheads)]
        _unstage(y_ref, y[0] if nheads == 1 else jnp.concatenate(y, axis=1), ci, decode, r)
        return carry_unused

    lax.fori_loop(0, nchunks, chunk, 0)

    @pl.when(t == pl.num_programs(2) - 1)
    def _():
        for hh in range(nheads):
            sn_ref[0, LANE * hh:LANE * (hh + 1), :] = st_scr[hh].T


def _hgrn(u, s0, lb, ng, n, seq, decode):
    tb = CHUNK if decode else min(512, seq)
    nb = 1 if decode else seq // tb
    nchunks = tb // CHUNK
    rb = SUBLANE if decode else tb
    rowmap = (lambda h, b, t: (b // SUBLANE, h)) if decode else (lambda h, b, t: (b * nb + t, h))
    rows_total = u['bq'].shape[0]
    nheads = HGRN_HEADS_PER_STEP
    wd = LANE * nheads
    y, sn = pl.pallas_call(
        functools.partial(_hgrn_kernel, nchunks=nchunks, decode=decode, nheads=nheads),
        grid=(B_HEADS // nheads, n, nb),
        in_specs=[pl.BlockSpec((rb, wd), rowmap)] * 4 + [
            pl.BlockSpec((1, wd), lambda h, b, t: (0, h)),
            pl.BlockSpec((1, LANE), lambda h, b, t: (0, 0)),
            pl.BlockSpec((1, B_KEY_DIM * nheads, LANE), lambda h, b, t: (b, h, 0))],
        out_specs=[pl.BlockSpec((rb, wd), rowmap),
                   pl.BlockSpec((1, B_KEY_DIM * nheads, LANE), lambda h, b, t: (b, h, 0))],
        out_shape=[jax.ShapeDtypeStruct((rows_total, D_MODEL), F32),
                   jax.ShapeDtypeStruct((n, D_MODEL, LANE), F32)],
        scratch_shapes=[pltpu.VMEM((nheads, LANE, B_KEY_DIM), F32)],
        compiler_params=_params(("arbitrary", "arbitrary", "arbitrary")),
        name="hgrn_decode" if decode else "hgrn_prompt",
    )(u['bq'], u['bf'], u['bi'], u['bg'], lb, ng, s0)
    return y, sn


def _t5_bucket_np(dist):
    exact = REL_BUCKETS // 2
    d = np.maximum(dist, 1).astype(np.float32)
    large = exact + (np.log(d / np.float32(exact)) / np.float32(math.log(REL_MAX_DIST / exact))
                     * np.float32(REL_BUCKETS - exact)).astype(np.int32)
    large = np.clip(large, 0, REL_BUCKETS - 1)
    return np.where(dist < exact, dist, large)


def _attn_prompt_kernel(q_ref, kc_ref, kp_ref, vc_ref, vp_ref, bias_ref, o_ref, lse_ref, *, dil):
    first = pl.program_id(1) == 0
    s_q = C_SPAN
    lane = lax.broadcasted_iota(jnp.int32, (s_q, LANE), 1)
    lo = lane < 64
    kcol = lax.broadcasted_iota(jnp.int32, (s_q, 2 * s_q), 1)
    no_prev = jnp.logical_and(first, kcol < s_q)

    pair = pl.program_id(2)

    def solve(qp, kprev, kcur, vprev, vcur, mask_prev):
        kp2 = jnp.concatenate([kprev, kcur], axis=0).astype(BF16)
        vp2 = jnp.concatenate([vprev, vcur], axis=0).astype(BF16)
        oh, lh = [], []
        for half in range(2):
            qm = jnp.where(lo if half == 0 else jnp.logical_not(lo), qp, 0.0).astype(BF16)
            s = lax.dot_general(qm, kp2, (((1,), (1,)), ((), ())), preferred_element_type=F32)
            s = s * (64 ** -0.5) + bias_ref[2 * pair + half]
            if mask_prev:
                s = jnp.where(no_prev, NEG_BIG, s)
            m = jnp.max(s, -1, keepdims=True)
            p = jnp.exp(s - m)
            den = jnp.sum(p, -1, keepdims=True)
            oh.append(jnp.dot((p / den).astype(BF16), vp2, preferred_element_type=F32))
            lh.append(m + jnp.log(den))
        return jnp.where(lo, oh[0], oh[1]), jnp.where(lo, lh[0], lh[1])

    if dil == 1:
        for sb in range(ATTN_ILP):
            cur = slice(s_q * sb, s_q * (sb + 1))
            if sb == 0:
                o, lse = solve(q_ref[cur, :], kp_ref[...], kc_ref[cur, :], vp_ref[...], vc_ref[cur, :], True)
            else:
                prv = slice(s_q * (sb - 1), s_q * sb)
                o, lse = solve(q_ref[cur, :], kc_ref[prv, :], kc_ref[cur, :], vc_ref[prv, :], vc_ref[cur, :],
                               False)
            o_ref[cur, :] = o
            lse_ref[cur, :] = lse
    else:
        def residues(it, carry):
            for jj in range(ATTN_ILP):
                sl = pl.ds(it * ATTN_ILP + jj, s_q, stride=dil)
                o, lse = solve(q_ref[sl, :], kp_ref[sl, :], kc_ref[sl, :], vp_ref[sl, :], vc_ref[sl, :], True)
                o_ref[sl, :] = o
                lse_ref[sl, :] = lse
            return carry

        lax.fori_loop(0, dil // ATTN_ILP, residues, 0)


def _attn_prompt(cq, ck, cv, bias, g, n, seq):
    dil = C_GROUPS[g][1]
    look = C_SPAN * dil
    rows_blk = look * (ATTN_ILP if dil == 1 else 1)
    nbk = seq // rows_blk
    per = rows_blk // look
    cur = lambda b, i, j: (b * nbk + i, 2 * g + j)
    prev = lambda b, i, j: (jnp.maximum((b * nbk + i) * per - 1, 0), 2 * g + j)
    blk = (rows_blk, LANE)
    pblk = (look, LANE)
    return pl.pallas_call(
        functools.partial(_attn_prompt_kernel, dil=dil),
        grid=(n, nbk, 2),
        in_specs=[pl.BlockSpec(blk, cur), pl.BlockSpec(blk, cur), pl.BlockSpec(pblk, prev),
                  pl.BlockSpec(blk, cur), pl.BlockSpec(pblk, prev),
                  pl.BlockSpec((4, C_SPAN, 2 * C_SPAN), lambda b, i, j: (0, 0, 0))],
        out_specs=[pl.BlockSpec(blk, lambda b, i, j: (b * nbk + i, j))] * 2,
        out_shape=[jax.ShapeDtypeStruct((n * seq, C_GROUP_WIDTH), F32)] * 2,
        compiler_params=_params(("arbitrary", "arbitrary", "arbitrary")),
        name=f"attn_prompt_g{g}",
    )(cq, ck, ck, cv, cv, bias)


def _prompt_bias(rel_bias, g):
    dil = C_GROUPS[g][1]
    qi = np.arange(C_SPAN)[:, None]
    kj = np.arange(2 * C_SPAN)[None, :]
    rel = qi + C_SPAN - kj
    band = (rel >= 0) & (rel <= C_SPAN)
    idx = _t5_bucket_np(np.maximum(rel, 0) * dil)
    tab = rel_bias[:, 4 * g:4 * g + 4]
    hit = jnp.asarray(idx)[None, :, :, None] == jnp.arange(REL_BUCKETS)
    vals = jnp.sum(jnp.where(hit, tab.T[:, None, None, :], 0.0), axis=-1)
    return jnp.where(band[None], vals, NEG_BIG)


def _decode_bias(rel_bias, g):
    win, dil = C_GROUPS[g]
    tab = rel_bias[:, 4 * g:4 * g + 4]
    pos = np.arange(win)
    vals = jnp.where((pos % dil == 0)[:, None], tab[_t5_bucket_np(win - pos)], NEG_BIG)
    bias_buf = jnp.zeros((2, SUBLANE, win), F32).at[:, 0:2].set(vals.T.reshape(2, 2, win))
    new = jnp.broadcast_to(tab[0].reshape(2, 2, 1), (2, 2, LANE))
    bias_new = jnp.zeros((2, SUBLANE, LANE), F32).at[:, 0:2].set(new)
    return bias_buf, bias_new


def _attn_decode_kernel(q_ref, k_ref, v_ref, c0_ref, c1_ref, c2_ref, bb0_ref, bb1_ref, bb2_ref, bn_ref,
                        o_ref, lse_ref):
    b = pl.program_id(0)
    row8 = lax.broadcasted_iota(jnp.int32, (SUBLANE, LANE), 0)
    lane8 = lax.broadcasted_iota(jnp.int32, (SUBLANE, LANE), 1)
    qmask = jnp.logical_or(jnp.logical_and(row8 == 0, lane8 < 64), jnp.logical_and(row8 == 1, lane8 >= 64))
    lo1 = lax.broadcasted_iota(jnp.int32, (1, LANE), 1) < 64
    q_all = q_ref[pl.ds(b, 1), :]
    k_all = k_ref[pl.ds(b, 1), :]
    v_all = v_ref[pl.ds(b, 1), :]
    o_parts, lse_parts = [], []
    for g, (buf_ref, bb_ref) in enumerate(((c0_ref, bb0_ref), (c1_ref, bb1_ref), (c2_ref, bb2_ref))):
        win = C_GROUPS[g][0]
        for j in range(2):
            c0 = C_GROUP_WIDTH * g + LANE * j
            qrow = q_all[:, c0:c0 + LANE]
            knew = k_all[:, c0:c0 + LANE]
            vnew = v_all[:, c0:c0 + LANE]
            q8 = jnp.where(qmask, jnp.broadcast_to(qrow, (SUBLANE, LANE)), 0.0)
            kt = buf_ref[0, 0, 0, 2 * j:2 * j + 2].reshape(LANE, win)
            vt = buf_ref[0, 0, 1, 2 * j:2 * j + 2].reshape(LANE, win)
            s = _dot(q8, kt) * (64 ** -0.5) + bb_ref[j]
            snew = jnp.sum(q8 * knew, -1, keepdims=True) * (64 ** -0.5) + bn_ref[g, j][:, 0:1]
            m = jnp.maximum(jnp.max(s, -1, keepdims=True), snew)
            p = jnp.exp(s - m)
            pn = jnp.exp(snew - m)
            den = jnp.sum(p, -1, keepdims=True) + pn
            o8 = _dot_nt(p / den, vt) + (pn / den) * vnew
            lse8 = jnp.broadcast_to(m + jnp.log(den), (SUBLANE, LANE))
            o_parts.append(jnp.where(lo1, o8[0:1, :], o8[1:2, :]))
            lse_parts.append(jnp.where(lo1, lse8[0:1, :], lse8[1:2, :]))
    o_ref[pl.ds(b, 1), :] = jnp.concatenate(o_parts, axis=1)
    lse_ref[pl.ds(b, 1), :] = jnp.concatenate(lse_parts, axis=1)


def _attn_decode(cq, ck, cv, caches_t, layer, bias_bufs, bias_new):
    n = cq.shape[0]
    full = lambda a: pl.BlockSpec(a.shape, lambda b: (0,) * a.ndim)
    cache_spec = lambda c: pl.BlockSpec((1, 1) + c.shape[2:], lambda b: (layer, b, 0, 0, 0, 0))
    return pl.pallas_call(
        _attn_decode_kernel,
        grid=(n,),
        in_specs=[full(cq), full(ck), full(cv)] + [cache_spec(c) for c in caches_t]
        + [full(bb) for bb in bias_bufs] + [full(bias_new)],
        out_specs=[pl.BlockSpec((n, 768), lambda b: (0, 0))] * 2,
        out_shape=[jax.ShapeDtypeStruct((n, 768), F32)] * 2,
        compiler_params=_params(("arbitrary",)),
        name="attn_decode",
    )(cq, ck, cv, *caches_t, *bias_bufs, bias_new)


def _merge_kernel(x_ref, ya_ref, yb_ref, o0_ref, o1_ref, o2_ref, l0_ref, l1_ref, l2_ref, gt_ref,
                  wa_ref, wb_ref, wc_ref, wo_ref, g_ref, b_ref, rw_ref, rb_ref, cnt0_ref,
                  x1_ref, te_ref, tg_ref, rk_ref, cnt_ref, cnt_scr):
    l0, l1, l2 = l0_ref[...], l1_ref[...], l2_ref[...]
    lm = jnp.maximum(jnp.maximum(l0, l1), l2)
    e0, e1, e2 = jnp.exp(l0 - lm), jnp.exp(l1 - lm), jnp.exp(l2 - lm)
    den = e0 + e1 + e2
    yc = (e0 / den) * o0_ref[...] + (e1 / den) * o1_ref[...] + (e2 / den) * o2_ref[...]
    ga = _sigmoid(gt_ref[:, 0:D_MODEL])
    gb = _sigmoid(gt_ref[:, D_MODEL:2 * D_MODEL])
    gc = _sigmoid(gt_ref[:, 2 * D_MODEL:3 * D_MODEL])
    merged = (ga * _dot(ya_ref[...], wa_ref[...]) + gb * _dot(yb_ref[...], wb_ref[...])
              + gc * _dot(yc, wc_ref[...]))
    h = DEEPNORM_ALPHA * x_ref[...] + _dot(merged, wo_ref[...])
    x1 = _layernorm(h, g_ref[...], b_ref[...])
    x1_ref[...] = x1
    logits = _dot_hi(x1, rw_ref[...]) + rb_ref[...]
    lane = lax.broadcasted_iota(jnp.int32, logits.shape, 1)
    lane_f = lane.astype(F32)
    te = jnp.zeros(logits.shape, F32)
    vals, onehots = [], []
    for k in range(TOP_K):
        m = jnp.max(logits, -1, keepdims=True)
        idx = jnp.min(jnp.where(logits == m, lane_f, float(LANE)), -1, keepdims=True)
        te = jnp.where(lane == k, idx, te)
        vals.append(m)
        hit = lane_f == idx
        onehots.append(hit.astype(F32))
        logits = jnp.where(hit, -jnp.inf, logits)
    ex = [jnp.exp(v - vals[0]) for v in vals]
    tot = ex[0] + ex[1] + ex[2] + ex[3]
    tg = jnp.zeros(logits.shape, F32)
    for k in range(TOP_K):
        tg = jnp.where(lane == k, ex[k] / tot, tg)
    te_ref[...] = te.astype(jnp.int32)
    tg_ref[...] = tg
    @pl.when(pl.program_id(0) == 0)
    def _():
        cnt_scr[...] = cnt0_ref[...]

    tm = logits.shape[0]
    oh = onehots[0] + onehots[1] + onehots[2] + onehots[3]
    earlier = (lax.broadcasted_iota(jnp.int32, (tm, tm), 0)
               > lax.broadcasted_iota(jnp.int32, (tm, tm), 1)).astype(BF16)
    before = jnp.dot(earlier, oh.astype(BF16), preferred_element_type=F32) + cnt_scr[...]
    rank = jnp.zeros(logits.shape, F32)
    for k in range(TOP_K):
        rank = jnp.where(lane == k, jnp.sum(onehots[k] * before, -1, keepdims=True), rank)
    rk_ref[...] = rank.astype(jnp.int32)
    cnt_scr[...] = cnt_scr[...] + jnp.sum(oh, 0, keepdims=True)
    cnt_ref[...] = cnt_scr[...]


def _merge(x, ya, yb, attn, gates, cnt0, prm, tm):
    t = x.shape[0]
    row = lambda w: pl.BlockSpec((tm, w), lambda i: (i, 0))
    full = lambda a: pl.BlockSpec(a.shape, lambda i: (0,) * a.ndim)
    ws = [prm['w_branch_a'], prm['w_branch_b'], prm['w_branch_c'], prm['w_out'],
          prm['ln1_g'], prm['ln1_b'], prm['router_w'], prm['router_b'], cnt0]
    (o0, l0), (o1, l1), (o2, l2) = attn
    return pl.pallas_call(
        _merge_kernel,
        grid=(t // tm,),
        in_specs=[row(D_MODEL)] * 3 + [row(C_GROUP_WIDTH)] * 6 + [row(3 * D_MODEL)] + [full(a) for a in ws],
        out_specs=[row(D_MODEL), row(LANE), row(LANE), row(LANE), pl.BlockSpec((1, LANE), lambda i: (0, 0))],
        out_shape=[jax.ShapeDtypeStruct((t, D_MODEL), F32), jax.ShapeDtypeStruct((t, LANE), jnp.int32),
                   jax.ShapeDtypeStruct((t, LANE), F32), jax.ShapeDtypeStruct((t, LANE), jnp.int32),
                   jax.ShapeDtypeStruct((1, LANE), F32)],
        scratch_shapes=[pltpu.VMEM((1, LANE), F32)],
        compiler_params=_params(("arbitrary",)),
        name="merge_ln_router",
    )(x, ya, yb, o0, o1, o2, l0, l1, l2, gates, *ws)


MOE_TILE = 256


def _dispatch_kernel(dest_ref, x_ref, xs_in_hbm, xs_hbm, sem, *, tm):
    del xs_in_hbm

    def row_copy(r, slot):
        return pltpu.make_async_copy(x_ref.at[pl.ds(r, 1)], xs_hbm.at[pl.ds(slot, 1)], sem)

    def start(g, c):
        r0 = pl.multiple_of(g * SUBLANE, SUBLANE)
        for u in range(SUBLANE):
            for k in range(TOP_K):
                row_copy(r0 + u, dest_ref[0, 0, TOP_K * (r0 + u) + k]).start(priority=k % 2)
        return c

    def wait(g, c):
        for _ in range(SUBLANE * TOP_K):
            row_copy(0, 0).wait()
        return c

    lax.fori_loop(0, tm // SUBLANE, start, 0)
    lax.fori_loop(0, tm // SUBLANE, wait, 0)


def _dispatch(x1, dest, xs, tm):
    t = x1.shape[0]
    return pl.pallas_call(
        functools.partial(_dispatch_kernel, tm=tm),
        grid=(t // tm,),
        in_specs=[pl.BlockSpec((1, 1, TOP_K * tm), lambda i: (i, 0, 0), memory_space=pltpu.SMEM),
                  pl.BlockSpec((tm, D_MODEL), lambda i: (i, 0)),
                  pl.BlockSpec(memory_space=pl.ANY)],
        out_specs=pl.BlockSpec(memory_space=pl.ANY),
        out_shape=jax.ShapeDtypeStruct(xs.shape, F32),
        scratch_shapes=[pltpu.SemaphoreType.DMA],
        input_output_aliases={2: 0},
        compiler_params=_params(("arbitrary",)),
        name="moe_dispatch",
    )(dest.reshape(t // tm, 1, TOP_K * tm), x1, xs)


def _expert_kernel(te_ref, nv_ref, x_ref, w1_ref, w2_ref, b1g_ref, b1l_ref, b2_ref, y_ref, w1p_scr, w2b_scr):
    i = pl.program_id(0)
    nv = nv_ref[i]
    changed = jnp.logical_or(i == 0, te_ref[i] != te_ref[jnp.maximum(i - 1, 0)])
    half = LANE
    blk = 2 * LANE
    d_ff = w2b_scr.shape[0]

    @pl.when(jnp.logical_and(nv > 0, changed))
    def _():
        src_r = lax.broadcasted_iota(jnp.int32, (blk, blk), 0)
        dst_c = lax.broadcasted_iota(jnp.int32, (blk, blk), 1)
        pick = jnp.where(dst_c < half, 2 * dst_c, 2 * (dst_c - half) + 1)
        perm = (src_r == pick).astype(BF16)
        for c in range(2 * d_ff // blk):
            wb = w1_ref[0, 0, :, blk * c:blk * (c + 1)].astype(BF16)
            w1p_scr[:, blk * c:blk * (c + 1)] = jnp.dot(wb, perm, preferred_element_type=F32).astype(BF16)
        w2b_scr[...] = w2_ref[0, 0].astype(BF16)

    @pl.when(nv > 0)
    def _():
        x = x_ref[...].astype(BF16)
        u = jnp.dot(x, w1p_scr[...], preferred_element_type=F32)
        nblk = 2 * d_ff // blk
        ug = jnp.concatenate([u[:, blk * c:blk * c + half] for c in range(nblk)], axis=1) + b1g_ref[0, 0]
        ul = jnp.concatenate([u[:, blk * c + half:blk * (c + 1)] for c in range(nblk)], axis=1) + b1l_ref[0, 0]
        glu = jnp.minimum(ug, SWIGLU_LIMIT)
        lin = jnp.clip(ul, -SWIGLU_LIMIT, SWIGLU_LIMIT)
        act = glu * _sigmoid(SWIGLU_ALPHA * glu) * (lin + 1.0)
        y_ref[...] = jnp.dot(act.astype(BF16), w2b_scr[...], preferred_element_type=F32) + b2_ref[0, 0]

    @pl.when(nv == 0)
    def _():
        y_ref[...] = jnp.zeros(y_ref.shape, F32)


def _moe_plan(counts, n_tiles):
    tm = MOE_TILE
    padded = (counts + tm - 1) // tm * tm
    pend = jnp.cumsum(padded)
    pstart = pend - padded
    tile_row0 = jnp.arange(n_tiles, dtype=jnp.int32) * tm
    tile_e = jnp.minimum(jnp.sum(tile_row0[:, None] >= pend[None, :], axis=1), N_EXPERTS - 1).astype(jnp.int32)
    tile_nv = jnp.clip(pstart[tile_e] + counts[tile_e] - tile_row0, 0, tm)
    tile_nv = jnp.where(tile_row0 < pend[-1], tile_nv, 0).astype(jnp.int32)
    return pstart, tile_e, tile_nv


def _slots(top_e, rank, pstart):
    experts = jnp.arange(N_EXPERTS, dtype=jnp.int32)
    base = jnp.sum(jnp.where(top_e[:, :TOP_K, None] == experts, pstart, 0), axis=-1)
    return (base + rank[:, :TOP_K]).astype(jnp.int32)


def _moe_experts(xs, tile_e, tile_nv, layer, w1, w2, b1g, b1l, b2):
    tm = MOE_TILE
    n_tiles = xs.shape[0] // tm
    d_ff = w2.shape[2]
    wspec = lambda a: pl.BlockSpec((1, 1) + a.shape[2:], lambda i, te, nv: (layer, te[i], 0, 0))
    return pl.pallas_call(
        _expert_kernel,
        grid_spec=pltpu.PrefetchScalarGridSpec(
            num_scalar_prefetch=2,
            grid=(n_tiles,),
            in_specs=[pl.BlockSpec((tm, D_MODEL), lambda i, te, nv: (i, 0))]
            + [wspec(a) for a in (w1, w2, b1g, b1l, b2)],
            out_specs=pl.BlockSpec((tm, D_MODEL), lambda i, te, nv: (i, 0)),
            scratch_shapes=[pltpu.VMEM((D_MODEL, 2 * d_ff), BF16), pltpu.VMEM((d_ff, D_MODEL), BF16)]),
        out_shape=jax.ShapeDtypeStruct(xs.shape, F32),
        compiler_params=_params(("arbitrary",)),
        name="moe_experts",
    )(tile_e, tile_nv, xs, w1, w2, b1g, b1l, b2)


def _combine_kernel(dest_ref, x_ref, tg_ref, g_ref, b_ref, ys_hbm, o_ref, ybuf, sem, *, tm):
    def row_copy(r, k, slot):
        return pltpu.make_async_copy(ys_hbm.at[pl.ds(slot, 1)], ybuf.at[k, pl.ds(r, 1)], sem)

    def start(g, c):
        r0 = pl.multiple_of(g * SUBLANE, SUBLANE)
        for u in range(SUBLANE):
            for k in range(TOP_K):
                row_copy(r0 + u, k, dest_ref[0, 0, TOP_K * (r0 + u) + k]).start(priority=k % 2)
        return c

    def wait(g, c):
        for _ in range(SUBLANE):
            for k in range(TOP_K):
                row_copy(0, k, 0).wait()
        return c

    lax.fori_loop(0, tm // SUBLANE, start, 0)
    lax.fori_loop(0, tm // SUBLANE, wait, 0)
    f = tg_ref[:, 0:1] * ybuf[0]
    for k in range(1, TOP_K):
        f = f + tg_ref[:, k:k + 1] * ybuf[k]
    o_ref[...] = _layernorm(DEEPNORM_ALPHA * x_ref[...] + f, g_ref[...], b_ref[...])


def _combine(x1, tg, dest, ys, prm, tm):
    t = x1.shape[0]
    row = lambda w: pl.BlockSpec((tm, w), lambda i: (i, 0))
    full = lambda a: pl.BlockSpec(a.shape, lambda i: (0,) * a.ndim)
    return pl.pallas_call(
        functools.partial(_combine_kernel, tm=tm),
        grid=(t // tm,),
        in_specs=[pl.BlockSpec((1, 1, TOP_K * tm), lambda i: (i, 0, 0), memory_space=pltpu.SMEM),
                  row(D_MODEL), row(LANE), full(prm['ln2_g']), full(prm['ln2_b']),
                  pl.BlockSpec(memory_space=pl.ANY)],
        out_specs=row(D_MODEL),
        out_shape=jax.ShapeDtypeStruct((t, D_MODEL), F32),
        scratch_shapes=[pltpu.VMEM((TOP_K, tm, D_MODEL), F32), pltpu.SemaphoreType.DMA],
        compiler_params=_params(("arbitrary",)),
        name="combine_ln",
    )(dest.reshape(t // tm, 1, TOP_K * tm), x1, tg, prm['ln2_g'], prm['ln2_b'], ys)


def _dt_pieces(dt_piece):
    return jnp.concatenate([dt_piece[:, 0:8], dt_piece[:, 128:136]], axis=1)


def _layer(xp, xs, prm, lb, rel_bias, st, n_p, seq, n_s, layer, moe):
    tp = n_p * seq
    up = _in_proj(xp, prm['w_in'], 128)
    dt16 = _dt_pieces(up['dt'])
    dt_t = jnp.transpose(dt16.reshape(tp // CHUNK, CHUNK, A_HEADS), (0, 2, 1))
    zeros_conv = jnp.zeros((n_p, SUBLANE, A_CONV_DIM), F32)
    ya, ssm_p = _ssd(up, dt_t, zeros_conv, jnp.zeros((n_p, D_MODEL, A_STATE), F32), prm, n_p, seq, False)
    yb, hg_p = _hgrn(up, jnp.zeros((n_p, D_MODEL, LANE), F32), lb, prm['hgrn_norm_g'], n_p, seq, False)
    attn = [_attn_prompt(up['cq'], up['ck'], up['cv'], _prompt_bias(rel_bias, g), g, n_p, seq)
            for g in range(3)]
    x1p, tep, tgp, rkp, cnt_p = _merge(xp, ya, yb, attn, up['gates'], jnp.zeros((1, LANE), F32), prm, 256)
    xbc3 = up['xbc'].reshape(n_p, seq, A_CONV_DIM)
    conv_p = xbc3[:, seq - (A_CONV - 1):]
    k4 = up['ck'].reshape(n_p, seq, 3, 4, 64)
    v4 = up['cv'].reshape(n_p, seq, 3, 4, 64)
    kv_p = [jnp.stack([k4[:, seq - min(w, seq):, g], v4[:, seq - min(w, seq):, g]], axis=2)
            for g, (w, _) in enumerate(C_GROUPS)]
    us = _in_proj(xs, prm['w_in'], n_s)
    dt16s = _dt_pieces(us['dt'])
    dt_ts = jnp.zeros((n_s, A_HEADS, CHUNK), F32).at[:, :, 0].set(dt16s)
    conv0 = jnp.pad(st['conv'], ((0, 0), (SUBLANE - (A_CONV - 1), 0), (0, 0)))
    yas, ssm_s = _ssd(us, dt_ts, conv0, st['ssm'].reshape(n_s, D_MODEL, A_STATE), prm, n_s, 1, True)
    ybs, hg_s = _hgrn(us, st['hgrn'].reshape(n_s, D_MODEL, LANE), lb, prm['hgrn_norm_g'], n_s, 1, True)
    bias_d = [_decode_bias(rel_bias, g) for g in range(3)]
    o_s, lse_s = _attn_decode(us['cq'], us['ck'], us['cv'], st['kv_t'], layer,
                              [b[0] for b in bias_d], jnp.stack([b[1] for b in bias_d]))
    attn_s = [(o_s[:, 256 * g:256 * (g + 1)], lse_s[:, 256 * g:256 * (g + 1)]) for g in range(3)]
    x1s, tes, tgs, rks, cnt = _merge(xs, yas, ybs, attn_s, us['gates'], cnt_p, prm, n_s)
    conv_s = jnp.concatenate([st['conv'][:, 1:], us['xbc'][:, None]], axis=1)
    ks4 = us['ck'].reshape(n_s, 1, 3, 4, 64)
    vs4 = us['cv'].reshape(n_s, 1, 3, 4, 64)
    kv_s = [jnp.stack([ks4[:, :, g], vs4[:, :, g]], axis=2) for g in range(3)]
    n_tiles = -(-(tp + n_s) * TOP_K // MOE_TILE) + N_EXPERTS
    pstart, tile_e, tile_nv = _moe_plan(cnt[0, :N_EXPERTS].astype(jnp.int32), n_tiles)
    dest_p = _slots(tep, rkp, pstart)
    dest_s = _slots(tes, rks, pstart)
    x_sorted = jnp.zeros((n_tiles * MOE_TILE, D_MODEL), F32)
    x_sorted = _dispatch(x1p, dest_p, x_sorted, 256)
    x_sorted = _dispatch(x1s, dest_s, x_sorted, n_s)
    y_sorted = _moe_experts(x_sorted, tile_e, tile_nv, layer, *moe)
    yp = _combine(x1p, tgp, dest_p, y_sorted, prm, 256)
    ys = _combine(x1s, tgs, dest_s, y_sorted, prm, n_s)
    states_p = (conv_p, ssm_p.reshape(n_p, A_HEADS, A_HEAD_DIM, A_STATE),
                hg_p.reshape(n_p, B_HEADS, B_KEY_DIM, LANE), kv_p[0], kv_p[1], kv_p[2])
    states_s = (conv_s, ssm_s.reshape(n_s, A_HEADS, A_HEAD_DIM, A_STATE),
                hg_s.reshape(n_s, B_HEADS, B_KEY_DIM, LANE), kv_s[0], kv_s[1], kv_s[2])
    return yp, ys, states_p, states_s


def _prep_layer(l, w_in, conv_w, conv_b, dt_bias, a_log, d_skip, ssm_norm_g, hgrn_norm_g,
                w_branch_a, w_branch_b, w_branch_c, w_out, ln1_g, ln1_b, router_w, router_b,
                moe_w1, moe_b1, moe_w2, moe_b2, ln2_g, ln2_b):
    def lanes_per_group(v):
        return jnp.zeros((1, 256), F32).at[0, 0:8].set(v[:8]).at[0, 128:136].set(v[8:])

    def sublanes_per_group(v):
        return jnp.broadcast_to(v[:, None], (A_HEADS, LANE))

    return {
        'w_in': _pack_w_in(w_in[l]),
        'conv_w8': jnp.pad(conv_w[l], ((0, SUBLANE - A_CONV), (0, 0))),
        'conv_b': conv_b[l][None],
        'dt_bias_l': lanes_per_group(dt_bias[l]), 'a_log_l': lanes_per_group(a_log[l]),
        'dt_bias_t': sublanes_per_group(dt_bias[l]), 'a_log_t': sublanes_per_group(a_log[l]),
        'd_skip_l': jnp.repeat(d_skip[l], A_HEAD_DIM)[None],
        'ssm_norm_g': ssm_norm_g[l][None],
        'hgrn_norm_g': hgrn_norm_g[l][None],
        'w_branch_a': w_branch_a[l].astype(BF16), 'w_branch_b': w_branch_b[l].astype(BF16),
        'w_branch_c': w_branch_c[l].astype(BF16), 'w_out': w_out[l].astype(BF16),
        'ln1_g': ln1_g[l][None], 'ln1_b': ln1_b[l][None],
        'router_w': jnp.pad(router_w[l], ((0, 0), (0, LANE - N_EXPERTS))),
        'router_b': jnp.pad(router_b[l], (0, LANE - N_EXPERTS), constant_values=-jnp.inf)[None],
        'ln2_g': ln2_g[l][None], 'ln2_b': ln2_b[l][None],
    }


def kernel(x_prompt, x_sample, state_conv, state_ssm, state_hgrn, cache_kv_w128, cache_kv_w512, cache_kv_w2048, w_in, conv_w, conv_b, dt_bias, a_log, d_skip, ssm_norm_g, hgrn_lb, hgrn_norm_g, rel_bias, w_branch_a, w_branch_b, w_branch_c, w_out, ln1_g, ln1_b, router_w, router_b, moe_w1, moe_b1, moe_w2, moe_b2, ln2_g, ln2_b):
    n_p, seq, _ = x_prompt.shape
    n_s = x_sample.shape[0]
    depth = w_in.shape[0]
    p_lb = jax.nn.softmax(hgrn_lb.astype(F32), axis=0)
    lower_bounds = jnp.cumsum(p_lb, axis=0) - p_lb[0]
    yp = x_prompt.reshape(n_p * seq, D_MODEL)
    ys = x_sample.reshape(n_s, D_MODEL)
    st_p, st_s = [], []
    kv_t = tuple(jnp.transpose(c, (0, 1, 3, 4, 5, 2)) for c in (cache_kv_w128, cache_kv_w512, cache_kv_w2048))
    moe = (moe_w1, moe_w2, moe_b1[:, :, None, 0::2], moe_b1[:, :, None, 1::2], moe_b2[:, :, None, :])
    for l in range(depth):
        prm = _prep_layer(l, w_in, conv_w, conv_b, dt_bias, a_log, d_skip, ssm_norm_g, hgrn_norm_g,
                          w_branch_a, w_branch_b, w_branch_c, w_out, ln1_g, ln1_b, router_w, router_b,
                          moe_w1, moe_b1, moe_w2, moe_b2, ln2_g, ln2_b)
        st = {'conv': state_conv[l], 'ssm': state_ssm[l], 'hgrn': state_hgrn[l], 'kv_t': kv_t}
        yp, ys, sp, ss = _layer(yp, ys, prm, lower_bounds[l][None], rel_bias, st, n_p, seq, n_s, l, moe)
        st_p.append(sp)
        st_s.append(ss)
    stack = lambda sts, i: jnp.stack([s[i] for s in sts], axis=0)
    return (yp.reshape(n_p, seq, D_MODEL), ys.reshape(n_s, 1, D_MODEL),
            stack(st_p, 0), stack(st_s, 0), stack(st_p, 1), stack(st_s, 1), stack(st_p, 2), stack(st_s, 2),
            stack(st_p, 3), stack(st_s, 3), stack(st_p, 4), stack(st_s, 4), stack(st_p, 5), stack(st_s, 5))
```

```python
import functools
import math

import jax
import jax.numpy as jnp
import numpy as np
from jax import lax
from jax.experimental import pallas as pl
from jax.experimental.pallas import tpu as pltpu

F32 = jnp.float32
BF16 = jnp.bfloat16
HI = lax.Precision.HIGHEST

D_MODEL = 1024
A_HEADS = 16
A_HEAD_DIM = 64
A_STATE = 128
A_CONV = 4
A_CONV_DIM = 1536
B_HEADS = 8
B_KEY_DIM = 128
C_GROUPS = ((128, 1), (512, 4), (2048, 16))
C_SPAN = 128
C_GROUP_WIDTH = 256
REL_BUCKETS = 32
REL_MAX_DIST = 2048
N_EXPERTS = 32
TOP_K = 4
SWIGLU_ALPHA = 1.702
SWIGLU_LIMIT = 7.0
DEEPNORM_ALPHA = (2.0 * 2) ** 0.25
LN_EPS = 1e-5
RMS_EPS = 1e-5
NEG_BIG = -1e30
LOG_FLOOR = 1e-30

LANE = 128
SUBLANE = 8
CHUNK = 128
SUB = 8
HGRN_HEADS_PER_STEP = 4
ATTN_ILP = 4
VMEM_LIMIT = 56 * 1024 * 1024

IN_PIECES = (('z', 1024), ('xbc', 1536), ('bq', 1024), ('bf', 1024), ('bi', 1024), ('bg', 1024),
             ('cq', 768), ('ck', 768), ('cv', 768), ('gates', 3072), ('dt', 256))
IN_PACKED = sum(w for _, w in IN_PIECES)


def _params(sem):
    return pltpu.CompilerParams(dimension_semantics=sem, vmem_limit_bytes=VMEM_LIMIT)


def _sigmoid(x):
    return 1.0 / (1.0 + jnp.exp(-x))


def _softplus(x):
    return jnp.maximum(x, 0.0) + jnp.log(1.0 + jnp.exp(-jnp.abs(x)))


def _dot(a, b):
    return jnp.dot(a.astype(BF16), b.astype(BF16), preferred_element_type=F32)


def _dot_nt(a, b):
    return lax.dot_general(a.astype(BF16), b.astype(BF16), (((1,), (1,)), ((), ())),
                           preferred_element_type=F32)


def _dot_hi(a, b):
    return jnp.dot(a, b, preferred_element_type=F32, precision=HI)


def _layernorm(h, g, b):
    mu = jnp.mean(h, -1, keepdims=True)
    c = h - mu
    var = jnp.mean(c * c, -1, keepdims=True)
    return c * lax.rsqrt(var + LN_EPS) * g + b


def _in_proj_kernel(x_ref, w_ref, *o_refs):
    xb = x_ref[...].astype(BF16)
    off = 0
    for o_ref in o_refs:
        wd = o_ref.shape[1]
        o_ref[...] = jnp.dot(xb, w_ref[:, off:off + wd], preferred_element_type=F32)
        off += wd


def _in_proj(x, w_packed, tm):
    t = x.shape[0]
    outs = pl.pallas_call(
        _in_proj_kernel,
        grid=(t // tm,),
        in_specs=[pl.BlockSpec((tm, D_MODEL), lambda i: (i, 0)),
                  pl.BlockSpec(memory_space=pltpu.VMEM)],
        out_specs=[pl.BlockSpec((tm, w), lambda i: (i, 0)) for _, w in IN_PIECES],
        out_shape=[jax.ShapeDtypeStruct((t, w), F32) for _, w in IN_PIECES],
        compiler_params=_params(("arbitrary",)),
        name="in_proj",
    )(x, w_packed)
    return {name: o for (name, _), o in zip(IN_PIECES, outs)}


def _pack_w_in(w):
    dt = w[:, 2560:2576]
    dtp = jnp.zeros((D_MODEL, 256), F32).at[:, 0:8].set(dt[:, :8]).at[:, 128:136].set(dt[:, 8:])
    return jnp.concatenate([w[:, :2560], w[:, 2576:], dtp], axis=1).astype(BF16)


def _stage(ref, ci, decode, r):
    if decode:
        row = ref[pl.ds(r, 1), :]
        rows = lax.broadcasted_iota(jnp.int32, (CHUNK, row.shape[1]), 0)
        return jnp.where(rows == 0, jnp.broadcast_to(row, (CHUNK, row.shape[1])), 0.0)
    return ref[pl.ds(pl.multiple_of(ci * CHUNK, CHUNK), CHUNK), :]


def _unstage(ref, val, ci, decode, r):
    if decode:
        ref[pl.ds(r, 1), :] = val[0:1, :]
    else:
        ref[pl.ds(pl.multiple_of(ci * CHUNK, CHUNK), CHUNK), :] = val


def _ssd_kernel(xs_ref, bm_ref, cm_ref, z_ref, dt_ref, dtt_ref,
                c0x_ref, c0b_ref, c0c_ref, h0_ref,
                wx_ref, wb_ref, wc_ref, bx_ref, bb_ref, bc_ref,
                dtb_ref, alog_ref, dtbt_ref, alogt_ref, dsk_ref, ng_ref,
                y_ref, hn_ref,
                h_scr, cx_scr, cb_scr, cc_scr, px_scr, pb_scr, pc_scr, *, nchunks, decode):
    b = pl.program_id(1)
    t = pl.program_id(2)
    r = b % SUBLANE
    q = CHUNK

    @pl.when(t == 0)
    def _():
        h_scr[...] = h0_ref[0]
        cx_scr[...] = c0x_ref[0]
        cb_scr[...] = c0b_ref[0]
        cc_scr[...] = c0c_ref[0]

    rows = lax.broadcasted_iota(jnp.int32, (q, q), 0)
    cols = lax.broadcasted_iota(jnp.int32, (q, q), 1)
    tril = rows >= cols
    tril_f = tril.astype(F32)
    triu_f = (rows <= cols).astype(F32)
    lane = lax.broadcasted_iota(jnp.int32, (q, LANE), 1)
    lo = lane < A_HEAD_DIM
    row_lo = lax.broadcasted_iota(jnp.int32, (LANE, LANE), 0) < A_HEAD_DIM
    valid_col = lax.broadcasted_iota(jnp.int32, (q, LANE), 0) == 0
    valid_row = lax.broadcasted_iota(jnp.int32, (SUBLANE, q), 1) == 0

    def conv(x, carry, pad, w_ref, b_ref):
        pad[0:SUBLANE, :] = carry[...]
        pad[SUBLANE:SUBLANE + q, :] = x
        if not decode:
            carry[...] = pad[q:q + SUBLANE, :]
        acc = b_ref[...]
        for j in range(A_CONV):
            acc = acc + pad[5 + j:5 + j + q, :] * w_ref[j:j + 1, :]
        return acc * _sigmoid(acc)

    def chunk(ci, carry_unused):
        xs = conv(_stage(xs_ref, ci, decode, r), cx_scr, px_scr, wx_ref, bx_ref)
        bm = conv(_stage(bm_ref, ci, decode, r), cb_scr, pb_scr, wb_ref, bb_ref)
        cm = conv(_stage(cm_ref, ci, decode, r), cc_scr, pc_scr, wc_ref, bc_ref)
        a_lane = -jnp.exp(alog_ref[...])
        a_sub = -jnp.exp(alogt_ref[...])
        dt = _softplus(_stage(dt_ref, ci, decode, r) + dtb_ref[...])
        dtt = _softplus(dtt_ref[ci] + dtbt_ref[...])
        if decode:
            dt = jnp.where(valid_col, dt, 0.0)
            dtt = jnp.where(valid_row, dtt, 0.0)
        if decode:
            acum = jnp.broadcast_to((dt * a_lane)[0:1, :], (q, LANE))
            acum_t = jnp.broadcast_to((dtt * a_sub)[:, 0:1], (SUBLANE, q))
        else:
            acum = _dot_hi(tril_f, dt * a_lane)
            acum_t = _dot_hi(dtt * a_sub, triu_f)
        a_last = acum[q - 1:q, :]
        cb = _dot_nt(cm, bm)
        cm_b = cm.astype(BF16)
        bm_b = bm.astype(BF16)
        ys = []
        for j in range(4):
            xp = xs[:, LANE * j:LANE * (j + 1)]
            xp_b = xp.astype(BF16)
            yd = []
            for half in range(2):
                hl = 2 * j + half
                diff = acum[:, hl:hl + 1] - acum_t[hl:hl + 1, :]
                lm = jnp.exp(jnp.where(tril, diff, NEG_BIG))
                m = cb * lm * dtt[hl:hl + 1, :]
                yd.append(jnp.dot(m.astype(BF16), xp_b, preferred_element_type=F32))
            y = jnp.where(lo, yd[0], yd[1])
            hp = h_scr[LANE * j:LANE * (j + 1), :]
            yo = lax.dot_general(cm_b, hp.astype(BF16), (((1,), (1,)), ((), ())),
                                 preferred_element_type=F32)
            e0 = acum[:, 2 * j:2 * j + 1]
            e1 = acum[:, 2 * j + 1:2 * j + 2]
            y = y + yo * jnp.exp(jnp.where(lo, e0, e1))
            w0 = dt[:, 2 * j:2 * j + 1] * jnp.exp(a_last[:, 2 * j:2 * j + 1] - e0)
            w1 = dt[:, 2 * j + 1:2 * j + 2] * jnp.exp(a_last[:, 2 * j + 1:2 * j + 2] - e1)
            xw = xp * jnp.where(lo, w0, w1)
            upd = jnp.dot(xw.T.astype(BF16), bm_b, preferred_element_type=F32)
            dec = jnp.exp(jnp.where(row_lo, a_last[:, 2 * j:2 * j + 1], a_last[:, 2 * j + 1:2 * j + 2]))
            h_scr[LANE * j:LANE * (j + 1), :] = hp * dec + upd
            ys.append(y)
        y = jnp.concatenate(ys, axis=1) + dsk_ref[...] * xs
        z = _stage(z_ref, ci, decode, r)
        y = y * (z * _sigmoid(z))
        y = y * lax.rsqrt(jnp.mean(y * y, -1, keepdims=True) + RMS_EPS) * ng_ref[...]
        _unstage(y_ref, y, ci, decode, r)
        return carry_unused

    lax.fori_loop(0, nchunks, chunk, 0)

    @pl.when(t == pl.num_programs(2) - 1)
    def _():
        hn_ref[0] = h_scr[...]


def _ssd(u, dt_t, conv0_8, h0, prm, n, seq, decode):
    tb = CHUNK if decode else min(512, seq)
    nb = 1 if decode else seq // tb
    nchunks = tb // CHUNK
    rb = SUBLANE if decode else tb

    def rowmap(lane_block):
        if decode:
            return lambda g, b, t: (b // SUBLANE, lane_block(g))
        return lambda g, b, t: (b * nb + t, lane_block(g))

    chunk_map = (lambda g, b, t: (b, g, 0)) if decode else (lambda g, b, t: (b * nb + t, g, 0))
    cw = (512, LANE, LANE)
    lane_blocks = (lambda g: g, lambda g: 8 + g, lambda g: 10 + g)
    in_specs = [
        pl.BlockSpec((rb, 512), rowmap(lane_blocks[0])),
        pl.BlockSpec((rb, LANE), rowmap(lane_blocks[1])),
        pl.BlockSpec((rb, LANE), rowmap(lane_blocks[2])),
        pl.BlockSpec((rb, 512), rowmap(lambda g: g)),
        pl.BlockSpec((rb, LANE), rowmap(lambda g: g)),
        pl.BlockSpec((nchunks, SUBLANE, CHUNK), chunk_map),
    ]
    in_specs += [pl.BlockSpec((1, SUBLANE, w), (lambda lb: (lambda g, b, t: (b, 0, lb(g))))(lb))
                 for w, lb in zip(cw, lane_blocks)]
    in_specs += [pl.BlockSpec((1, 512, A_STATE), lambda g, b, t: (b, g, 0))]
    in_specs += [pl.BlockSpec((SUBLANE, w), (lambda lb: (lambda g, b, t: (0, lb(g))))(lb))
                 for w, lb in zip(cw, lane_blocks)]
    in_specs += [pl.BlockSpec((1, w), (lambda lb: (lambda g, b, t: (0, lb(g))))(lb))
                 for w, lb in zip(cw, lane_blocks)]
    in_specs += [
        pl.BlockSpec((1, LANE), lambda g, b, t: (0, g)),
        pl.BlockSpec((1, LANE), lambda g, b, t: (0, g)),
        pl.BlockSpec((SUBLANE, LANE), lambda g, b, t: (g, 0)),
        pl.BlockSpec((SUBLANE, LANE), lambda g, b, t: (g, 0)),
        pl.BlockSpec((1, 512), lambda g, b, t: (0, g)),
        pl.BlockSpec((1, 512), lambda g, b, t: (0, g)),
    ]
    rows_total = u['z'].shape[0]
    y, hn = pl.pallas_call(
        functools.partial(_ssd_kernel, nchunks=nchunks, decode=decode),
        grid=(2, n, nb),
        in_specs=in_specs,
        out_specs=[pl.BlockSpec((rb, 512), rowmap(lambda g: g)),
                   pl.BlockSpec((1, 512, A_STATE), lambda g, b, t: (b, g, 0))],
        out_shape=[jax.ShapeDtypeStruct((rows_total, D_MODEL), F32),
                   jax.ShapeDtypeStruct((n, D_MODEL, A_STATE), F32)],
        scratch_shapes=[pltpu.VMEM((512, A_STATE), F32),
                        pltpu.VMEM((SUBLANE, 512), F32), pltpu.VMEM((SUBLANE, LANE), F32),
                        pltpu.VMEM((SUBLANE, LANE), F32),
                        pltpu.VMEM((CHUNK + SUBLANE, 512), F32), pltpu.VMEM((CHUNK + SUBLANE, LANE), F32),
                        pltpu.VMEM((CHUNK + SUBLANE, LANE), F32)],
        compiler_params=_params(("arbitrary", "arbitrary", "arbitrary")),
        name="ssd_decode" if decode else "ssd_prompt",
    )(u['xbc'], u['xbc'], u['xbc'], u['z'], u['dt'], dt_t,
      conv0_8, conv0_8, conv0_8, h0,
      prm['conv_w8'], prm['conv_w8'], prm['conv_w8'], prm['conv_b'], prm['conv_b'], prm['conv_b'],
      prm['dt_bias_l'], prm['a_log_l'], prm['dt_bias_t'], prm['a_log_t'], prm['d_skip_l'], prm['ssm_norm_g'])
    return y, hn


def _hgrn_kernel(q_ref, f_ref, i_ref, g_ref, lb_ref, ng_ref, s0_ref, y_ref, sn_ref, st_scr,
                 *, nchunks, decode, nheads):
    b = pl.program_id(1)
    t = pl.program_id(2)
    r = b % SUBLANE
    c = CHUNK

    @pl.when(t == 0)
    def _():
        for hh in range(nheads):
            st_scr[hh] = s0_ref[0, LANE * hh:LANE * (hh + 1), :].T

    rows = lax.broadcasted_iota(jnp.int32, (c, c), 0)
    cols = lax.broadcasted_iota(jnp.int32, (c, c), 1)
    tril_f = (rows >= cols).astype(F32)
    row_c = lax.broadcasted_iota(jnp.int32, (c, LANE), 0)
    row_s = lax.broadcasted_iota(jnp.int32, (SUB, LANE), 0)
    blk_xor = (rows // SUB) ^ (cols // SUB)
    level = jnp.where(blk_xor >= 8, 3, jnp.where(blk_xor >= 4, 2, jnp.where(blk_xor >= 2, 1, 0)))
    level = jnp.where((rows // SUB) > (cols // SUB), level, -1)
    nlevels = 4
    assert SUB << nlevels == c

    def one_head(hh, ci):
        ls = slice(LANE * hh, LANE * (hh + 1))
        lbv = lb_ref[:, ls]
        f = _stage(f_ref, ci, decode, r)[:, ls]
        qr = _stage(q_ref, ci, decode, r)[:, ls]
        v = _stage(i_ref, ci, decode, r)[:, ls]
        gr = _stage(g_ref, ci, decode, r)[:, ls]
        q = qr * _sigmoid(qr) * (B_KEY_DIM ** -0.5)
        e_f = jnp.exp(-jnp.abs(f))
        r_f = 1.0 / (1.0 + e_f)
        pos = f >= 0.0
        sig_p = jnp.where(pos, r_f, e_f * r_f)
        sig_n = jnp.where(pos, e_f * r_f, r_f)
        logf = jnp.log(jnp.maximum(lbv + (1.0 - lbv) * sig_p, LOG_FLOOR))
        k = (1.0 - lbv) * sig_n
        if decode:
            logf = jnp.where(row_c == 0, logf, 0.0)
            k = jnp.where(row_c == 0, k, 0.0)
        if decode:
            bcum = jnp.broadcast_to(logf[0:1, :], (c, LANE))
        else:
            bcum = _dot_hi(tril_f, logf)
        b_last = bcum[c - 1:c, :]
        st = st_scr[hh]
        o = _dot_nt(q * jnp.exp(bcum), st)
        if not decode:
            att = jnp.zeros((c, c), F32)
            for lv in range(nlevels):
                h = SUB << lv
                ref = jnp.concatenate(
                    [jnp.broadcast_to(bcum[2 * h * m + h - 1:2 * h * m + h, :], (2 * h, LANE))
                     for m in range(c // (2 * h))], axis=0)
                x = jnp.exp(-jnp.abs(bcum - ref))
                att = jnp.where(level == lv, _dot_nt(q * x, k * x), att)
            o = o + _dot(att, v)
        diag = []
        for i in range(c // SUB):
            if decode and i > 0:
                diag.append(jnp.zeros((SUB, LANE), F32))
                continue
            sl = slice(SUB * i, SUB * (i + 1))
            qb, kb, vb, bb = q[sl, :], k[sl, :], v[sl, :], bcum[sl, :]
            od = jnp.zeros((SUB, LANE), F32)
            for s in range(1 if decode else SUB):
                e = jnp.exp(jnp.where(row_s >= s, bb - bb[s:s + 1, :], NEG_BIG))
                rs = jnp.sum(qb * kb[s:s + 1, :] * e, -1, keepdims=True)
                od = od + rs * vb[s:s + 1, :]
            diag.append(od)
        o = o + jnp.concatenate(diag, axis=0)
        k2 = k * jnp.exp(b_last - bcum)
        st_scr[hh] = jnp.exp(b_last) * st + jnp.dot(v.T.astype(BF16), k2.astype(BF16),
                                                     preferred_element_type=F32)
        gate = gr * _sigmoid(gr)
        return o * lax.rsqrt(jnp.mean(o * o, -1, keepdims=True) + RMS_EPS) * ng_ref[...] * gate

    def chunk(ci, carry_unused):
        y = [one_head(hh, ci) for hh in range(nheads)]
        _unstage(y_ref, y[0] if nheads == 1 else jnp.concatenate(y, axis=1), ci, decode, r)
        return carry_unused

    lax.fori_loop(0, nchunks, chunk, 0)

    @pl.when(t == pl.num_programs(2) - 1)
    def _():
        for hh in range(nheads):
            sn_ref[0, LANE * hh:LANE * (hh + 1), :] = st_scr[hh].T


def _hgrn(u, s0, lb, ng, n, seq, decode):
    tb = CHUNK if decode else min(512, seq)
    nb = 1 if decode else seq // tb
    nchunks = tb // CHUNK
    rb = SUBLANE if decode else tb
    rowmap = (lambda h, b, t: (b // SUBLANE, h)) if decode else (lambda h, b, t: (b * nb + t, h))
    rows_total = u['bq'].shape[0]
    nheads = B_HEADS if decode else HGRN_HEADS_PER_STEP
    wd = LANE * nheads
    y, sn = pl.pallas_call(
        functools.partial(_hgrn_kernel, nchunks=nchunks, decode=decode, nheads=nheads),
        grid=(B_HEADS // nheads, n, nb),
        in_specs=[pl.BlockSpec((rb, wd), rowmap)] * 4 + [
            pl.BlockSpec((1, wd), lambda h, b, t: (0, h)),
            pl.BlockSpec((1, LANE), lambda h, b, t: (0, 0)),
            pl.BlockSpec((1, B_KEY_DIM * nheads, LANE), lambda h, b, t: (b, h, 0))],
        out_specs=[pl.BlockSpec((rb, wd), rowmap),
                   pl.BlockSpec((1, B_KEY_DIM * nheads, LANE), lambda h, b, t: (b, h, 0))],
        out_shape=[jax.ShapeDtypeStruct((rows_total, D_MODEL), F32),
                   jax.ShapeDtypeStruct((n, D_MODEL, LANE), F32)],
        scratch_shapes=[pltpu.VMEM((nheads, LANE, B_KEY_DIM), F32)],
        compiler_params=_params(("arbitrary", "arbitrary", "arbitrary")),
        name="hgrn_decode" if decode else "hgrn_prompt",
    )(u['bq'], u['bf'], u['bi'], u['bg'], lb, ng, s0)
    return y, sn


def _t5_bucket_np(dist):
    exact = REL_BUCKETS // 2
    d = np.maximum(dist, 1).astype(np.float32)
    large = exact + (np.log(d / np.float32(exact)) / np.float32(math.log(REL_MAX_DIST / exact))
                     * np.float32(REL_BUCKETS - exact)).astype(np.int32)
    large = np.clip(large, 0, REL_BUCKETS - 1)
    return np.where(dist < exact, dist, large)


def _attn_prompt_kernel(q_ref, kc_ref, kp_ref, vc_ref, vp_ref, bias_ref, o_ref, lse_ref, *, dil):
    first = pl.program_id(1) == 0
    s_q = C_SPAN
    lane = lax.broadcasted_iota(jnp.int32, (s_q, LANE), 1)
    lo = lane < 64
    kcol = lax.broadcasted_iota(jnp.int32, (s_q, 2 * s_q), 1)
    no_prev = jnp.logical_and(first, kcol < s_q)

    pair = pl.program_id(2)

    def solve(qp, kprev, kcur, vprev, vcur, mask_prev):
        kp2 = jnp.concatenate([kprev, kcur], axis=0).astype(BF16)
        vp2 = jnp.concatenate([vprev, vcur], axis=0).astype(BF16)
        oh, lh = [], []
        for half in range(2):
            qm = jnp.where(lo if half == 0 else jnp.logical_not(lo), qp, 0.0).astype(BF16)
            s = lax.dot_general(qm, kp2, (((1,), (1,)), ((), ())), preferred_element_type=F32)
            s = s * (64 ** -0.5) + bias_ref[2 * pair + half]
            if mask_prev:
                s = jnp.where(no_prev, NEG_BIG, s)
            m = jnp.max(s, -1, keepdims=True)
            p = jnp.exp(s - m)
            den = jnp.sum(p, -1, keepdims=True)
            oh.append(jnp.dot((p / den).astype(BF16), vp2, preferred_element_type=F32))
            lh.append(m + jnp.log(den))
        return jnp.where(lo, oh[0], oh[1]), jnp.where(lo, lh[0], lh[1])

    if dil == 1:
        for sb in range(ATTN_ILP):
            cur = slice(s_q * sb, s_q * (sb + 1))
            if sb == 0:
                o, lse = solve(q_ref[cur, :], kp_ref[...], kc_ref[cur, :], vp_ref[...], vc_ref[cur, :], True)
            else:
                prv = slice(s_q * (sb - 1), s_q * sb)
                o, lse = solve(q_ref[cur, :], kc_ref[prv, :], kc_ref[cur, :], vc_ref[prv, :], vc_ref[cur, :],
                               False)
            o_ref[cur, :] = o
            lse_ref[cur, :] = lse
    else:
        def residues(it, carry):
            for jj in range(ATTN_ILP):
                sl = pl.ds(it * ATTN_ILP + jj, s_q, stride=dil)
                o, lse = solve(q_ref[sl, :], kp_ref[sl, :], kc_ref[sl, :], vp_ref[sl, :], vc_ref[sl, :], True)
                o_ref[sl, :] = o
                lse_ref[sl, :] = lse
            return carry

        lax.fori_loop(0, dil // ATTN_ILP, residues, 0)


def _attn_prompt(cq, ck, cv, bias, g, n, seq):
    dil = C_GROUPS[g][1]
    look = C_SPAN * dil
    rows_blk = look * (ATTN_ILP if dil == 1 else 1)
    nbk = seq // rows_blk
    per = rows_blk // look
    cur = lambda b, i, j: (b * nbk + i, 2 * g + j)
    prev = lambda b, i, j: (jnp.maximum((b * nbk + i) * per - 1, 0), 2 * g + j)
    blk = (rows_blk, LANE)
    pblk = (look, LANE)
    return pl.pallas_call(
        functools.partial(_attn_prompt_kernel, dil=dil),
        grid=(n, nbk, 2),
        in_specs=[pl.BlockSpec(blk, cur), pl.BlockSpec(blk, cur), pl.BlockSpec(pblk, prev),
                  pl.BlockSpec(blk, cur), pl.BlockSpec(pblk, prev),
                  pl.BlockSpec((4, C_SPAN, 2 * C_SPAN), lambda b, i, j: (0, 0, 0))],
        out_specs=[pl.BlockSpec(blk, lambda b, i, j: (b * nbk + i, j))] * 2,
        out_shape=[jax.ShapeDtypeStruct((n * seq, C_GROUP_WIDTH), F32)] * 2,
        compiler_params=_params(("arbitrary", "arbitrary", "arbitrary")),
        name=f"attn_prompt_g{g}",
    )(cq, ck, ck, cv, cv, bias)


def _prompt_bias(rel_bias, g):
    dil = C_GROUPS[g][1]
    qi = np.arange(C_SPAN)[:, None]
    kj = np.arange(2 * C_SPAN)[None, :]
    rel = qi + C_SPAN - kj
    band = (rel >= 0) & (rel <= C_SPAN)
    idx = _t5_bucket_np(np.maximum(rel, 0) * dil)
    tab = rel_bias[:, 4 * g:4 * g + 4]
    hit = jnp.asarray(idx)[None, :, :, None] == jnp.arange(REL_BUCKETS)
    vals = jnp.sum(jnp.where(hit, tab.T[:, None, None, :], 0.0), axis=-1)
    return jnp.where(band[None], vals, NEG_BIG)


def _decode_bias(rel_bias, g):
    win, dil = C_GROUPS[g]
    tab = rel_bias[:, 4 * g:4 * g + 4]
    pos = np.arange(win)
    vals = jnp.where((pos % dil == 0)[:, None], tab[_t5_bucket_np(win - pos)], NEG_BIG)
    bias_buf = jnp.zeros((2, SUBLANE, win), F32).at[:, 0:2].set(vals.T.reshape(2, 2, win))
    new = jnp.broadcast_to(tab[0].reshape(2, 2, 1), (2, 2, LANE))
    bias_new = jnp.zeros((2, SUBLANE, LANE), F32).at[:, 0:2].set(new)
    return bias_buf, bias_new


def _attn_decode_kernel(q_ref, k_ref, v_ref, c0_ref, c1_ref, c2_ref, bb0_ref, bb1_ref, bb2_ref, bn_ref,
                        o_ref, lse_ref):
    b = pl.program_id(0)
    row8 = lax.broadcasted_iota(jnp.int32, (SUBLANE, LANE), 0)
    lane8 = lax.broadcasted_iota(jnp.int32, (SUBLANE, LANE), 1)
    qmask = jnp.logical_or(jnp.logical_and(row8 == 0, lane8 < 64), jnp.logical_and(row8 == 1, lane8 >= 64))
    lo1 = lax.broadcasted_iota(jnp.int32, (1, LANE), 1) < 64
    q_all = q_ref[pl.ds(b, 1), :]
    k_all = k_ref[pl.ds(b, 1), :]
    v_all = v_ref[pl.ds(b, 1), :]
    o_parts, lse_parts = [], []
    for g, (buf_ref, bb_ref) in enumerate(((c0_ref, bb0_ref), (c1_ref, bb1_ref), (c2_ref, bb2_ref))):
        win = C_GROUPS[g][0]
        for j in range(2):
            c0 = C_GROUP_WIDTH * g + LANE * j
            qrow = q_all[:, c0:c0 + LANE]
            knew = k_all[:, c0:c0 + LANE]
            vnew = v_all[:, c0:c0 + LANE]
            q8 = jnp.where(qmask, jnp.broadcast_to(qrow, (SUBLANE, LANE)), 0.0)
            kt = buf_ref[0, 0, 0, 2 * j:2 * j + 2].reshape(LANE, win)
            vt = buf_ref[0, 0, 1, 2 * j:2 * j + 2].reshape(LANE, win)
            s = _dot(q8, kt) * (64 ** -0.5) + bb_ref[j]
            snew = jnp.sum(q8 * knew, -1, keepdims=True) * (64 ** -0.5) + bn_ref[g, j][:, 0:1]
            m = jnp.maximum(jnp.max(s, -1, keepdims=True), snew)
            p = jnp.exp(s - m)
            pn = jnp.exp(snew - m)
            den = jnp.sum(p, -1, keepdims=True) + pn
            o8 = _dot_nt(p / den, vt) + (pn / den) * vnew
            lse8 = jnp.broadcast_to(m + jnp.log(den), (SUBLANE, LANE))
            o_parts.append(jnp.where(lo1, o8[0:1, :], o8[1:2, :]))
            lse_parts.append(jnp.where(lo1, lse8[0:1, :], lse8[1:2, :]))
    o_ref[pl.ds(b, 1), :] = jnp.concatenate(o_parts, axis=1)
    lse_ref[pl.ds(b, 1), :] = jnp.concatenate(lse_parts, axis=1)


def _attn_decode(cq, ck, cv, caches_t, layer, bias_bufs, bias_new):
    n = cq.shape[0]
    full = lambda a: pl.BlockSpec(a.shape, lambda b: (0,) * a.ndim)
    cache_spec = lambda c: pl.BlockSpec((1, 1) + c.shape[2:], lambda b: (layer, b, 0, 0, 0, 0))
    return pl.pallas_call(
        _attn_decode_kernel,
        grid=(n,),
        in_specs=[full(cq), full(ck), full(cv)] + [cache_spec(c) for c in caches_t]
        + [full(bb) for bb in bias_bufs] + [full(bias_new)],
        out_specs=[pl.BlockSpec((n, 768), lambda b: (0, 0))] * 2,
        out_shape=[jax.ShapeDtypeStruct((n, 768), F32)] * 2,
        compiler_params=_params(("arbitrary",)),
        name="attn_decode",
    )(cq, ck, cv, *caches_t, *bias_bufs, bias_new)


def _merge_kernel(x_ref, ya_ref, yb_ref, o0_ref, o1_ref, o2_ref, l0_ref, l1_ref, l2_ref, gt_ref,
                  wa_ref, wb_ref, wc_ref, wo_ref, g_ref, b_ref, rw_ref, rb_ref, cnt0_ref,
                  x1_ref, te_ref, tg_ref, rk_ref, cnt_ref, cnt_scr):
    l0, l1, l2 = l0_ref[...], l1_ref[...], l2_ref[...]
    lm = jnp.maximum(jnp.maximum(l0, l1), l2)
    e0, e1, e2 = jnp.exp(l0 - lm), jnp.exp(l1 - lm), jnp.exp(l2 - lm)
    den = e0 + e1 + e2
    yc = (e0 / den) * o0_ref[...] + (e1 / den) * o1_ref[...] + (e2 / den) * o2_ref[...]
    ga = _sigmoid(gt_ref[:, 0:D_MODEL])
    gb = _sigmoid(gt_ref[:, D_MODEL:2 * D_MODEL])
    gc = _sigmoid(gt_ref[:, 2 * D_MODEL:3 * D_MODEL])
    merged = (ga * _dot(ya_ref[...], wa_ref[...]) + gb * _dot(yb_ref[...], wb_ref[...])
              + gc * _dot(yc, wc_ref[...]))
    h = DEEPNORM_ALPHA * x_ref[...] + _dot(merged, wo_ref[...])
    x1 = _layernorm(h, g_ref[...], b_ref[...])
    x1_ref[...] = x1
    logits = _dot(x1, rw_ref[...]) + rb_ref[...]
    lane = lax.broadcasted_iota(jnp.int32, logits.shape, 1)
    lane_f = lane.astype(F32)
    te = jnp.zeros(logits.shape, F32)
    vals, onehots = [], []
    for k in range(TOP_K):
        m = jnp.max(logits, -1, keepdims=True)
        idx = jnp.min(jnp.where(logits == m, lane_f, float(LANE)), -1, keepdims=True)
        te = jnp.where(lane == k, idx, te)
        vals.append(m)
        hit = lane_f == idx
        onehots.append(hit.astype(F32))
        logits = jnp.where(hit, -jnp.inf, logits)
    ex = [jnp.exp(v - vals[0]) for v in vals]
    tot = ex[0] + ex[1] + ex[2] + ex[3]
    tg = jnp.zeros(logits.shape, F32)
    for k in range(TOP_K):
        tg = jnp.where(lane == k, ex[k] / tot, tg)
    te_ref[...] = te.astype(jnp.int32)
    tg_ref[...] = tg
    @pl.when(pl.program_id(0) == 0)
    def _():
        cnt_scr[...] = cnt0_ref[...]

    tm = logits.shape[0]
    oh = onehots[0] + onehots[1] + onehots[2] + onehots[3]
    earlier = (lax.broadcasted_iota(jnp.int32, (tm, tm), 0)
               > lax.broadcasted_iota(jnp.int32, (tm, tm), 1)).astype(BF16)
    before = jnp.dot(earlier, oh.astype(BF16), preferred_element_type=F32) + cnt_scr[...]
    rank = jnp.zeros(logits.shape, F32)
    for k in range(TOP_K):
        rank = jnp.where(lane == k, jnp.sum(onehots[k] * before, -1, keepdims=True), rank)
    rk_ref[...] = rank.astype(jnp.int32)
    cnt_scr[...] = cnt_scr[...] + jnp.sum(oh, 0, keepdims=True)
    cnt_ref[...] = cnt_scr[...]


def _merge(x, ya, yb, attn, gates, cnt0, prm, tm):
    t = x.shape[0]
    row = lambda w: pl.BlockSpec((tm, w), lambda i: (i, 0))
    full = lambda a: pl.BlockSpec(a.shape, lambda i: (0,) * a.ndim)
    ws = [prm['w_branch_a'], prm['w_branch_b'], prm['w_branch_c'], prm['w_out'],
          prm['ln1_g'], prm['ln1_b'], prm['router_w'], prm['router_b'], cnt0]
    (o0, l0), (o1, l1), (o2, l2) = attn
    return pl.pallas_call(
        _merge_kernel,
        grid=(t // tm,),
        in_specs=[row(D_MODEL)] * 3 + [row(C_GROUP_WIDTH)] * 6 + [row(3 * D_MODEL)] + [full(a) for a in ws],
        out_specs=[row(D_MODEL), row(LANE), row(LANE), row(LANE), pl.BlockSpec((1, LANE), lambda i: (0, 0))],
        out_shape=[jax.ShapeDtypeStruct((t, D_MODEL), F32), jax.ShapeDtypeStruct((t, LANE), jnp.int32),
                   jax.ShapeDtypeStruct((t, LANE), F32), jax.ShapeDtypeStruct((t, LANE), jnp.int32),
                   jax.ShapeDtypeStruct((1, LANE), F32)],
        scratch_shapes=[pltpu.VMEM((1, LANE), F32)],
        compiler_params=_params(("arbitrary",)),
        name="merge_ln_router",
    )(x, ya, yb, o0, o1, o2, l0, l1, l2, gates, *ws)


MOE_TILE = 512


def _dispatch_kernel(dest_ref, x_ref, xs_in_hbm, xs_hbm, sem, *, tm):
    del xs_in_hbm

    def row_copy(r, slot):
        return pltpu.make_async_copy(x_ref.at[pl.ds(r, 1)], xs_hbm.at[pl.ds(slot, 1)], sem)

    def start(r, c):
        for k in range(TOP_K):
            row_copy(r, dest_ref[0, 0, TOP_K * r + k]).start(priority=k % 2)
        return c

    def wait(r, c):
        for _ in range(TOP_K):
            row_copy(0, 0).wait()
        return c

    lax.fori_loop(0, tm, start, 0)
    lax.fori_loop(0, tm, wait, 0)


def _dispatch(x1, dest, xs, tm):
    t = x1.shape[0]
    return pl.pallas_call(
        functools.partial(_dispatch_kernel, tm=tm),
        grid=(t // tm,),
        in_specs=[pl.BlockSpec((1, 1, TOP_K * tm), lambda i: (i, 0, 0), memory_space=pltpu.SMEM),
                  pl.BlockSpec((tm, D_MODEL), lambda i: (i, 0)),
                  pl.BlockSpec(memory_space=pl.ANY)],
        out_specs=pl.BlockSpec(memory_space=pl.ANY),
        out_shape=jax.ShapeDtypeStruct(xs.shape, F32),
        scratch_shapes=[pltpu.SemaphoreType.DMA],
        input_output_aliases={2: 0},
        compiler_params=_params(("arbitrary",)),
        name="moe_dispatch",
    )(dest.reshape(t // tm, 1, TOP_K * tm), x1, xs)


def _expert_kernel(te_ref, nv_ref, x_ref, w1_ref, w2_ref, b1g_ref, b1l_ref, b2_ref, y_ref, w1p_scr, w2b_scr):
    i = pl.program_id(0)
    nv = nv_ref[i]
    changed = jnp.logical_or(i == 0, te_ref[i] != te_ref[jnp.maximum(i - 1, 0)])
    half = LANE
    blk = 2 * LANE
    d_ff = w2b_scr.shape[0]

    @pl.when(jnp.logical_and(nv > 0, changed))
    def _():
        src_r = lax.broadcasted_iota(jnp.int32, (blk, blk), 0)
        dst_c = lax.broadcasted_iota(jnp.int32, (blk, blk), 1)
        pick = jnp.where(dst_c < half, 2 * dst_c, 2 * (dst_c - half) + 1)
        perm = (src_r == pick).astype(BF16)
        for c in range(2 * d_ff // blk):
            wb = w1_ref[0, 0, :, blk * c:blk * (c + 1)].astype(BF16)
            w1p_scr[:, blk * c:blk * (c + 1)] = jnp.dot(wb, perm, preferred_element_type=F32).astype(BF16)
        w2b_scr[...] = w2_ref[0, 0].astype(BF16)

    @pl.when(nv > 0)
    def _():
        x = x_ref[...].astype(BF16)
        u = jnp.dot(x, w1p_scr[...], preferred_element_type=F32)
        nblk = 2 * d_ff // blk
        ug = jnp.concatenate([u[:, blk * c:blk * c + half] for c in range(nblk)], axis=1) + b1g_ref[0, 0]
        ul = jnp.concatenate([u[:, blk * c + half:blk * (c + 1)] for c in range(nblk)], axis=1) + b1l_ref[0, 0]
        glu = jnp.minimum(ug, SWIGLU_LIMIT)
        lin = jnp.clip(ul, -SWIGLU_LIMIT, SWIGLU_LIMIT)
        act = glu * _sigmoid(SWIGLU_ALPHA * glu) * (lin + 1.0)
        y_ref[...] = jnp.dot(act.astype(BF16), w2b_scr[...], preferred_element_type=F32) + b2_ref[0, 0]

    @pl.when(nv == 0)
    def _():
        y_ref[...] = jnp.zeros(y_ref.shape, F32)


def _moe_plan(counts, n_tiles):
    tm = MOE_TILE
    padded = (counts + tm - 1) // tm * tm
    pend = jnp.cumsum(padded)
    pstart = pend - padded
    tile_row0 = jnp.arange(n_tiles, dtype=jnp.int32) * tm
    tile_e = jnp.minimum(jnp.sum(tile_row0[:, None] >= pend[None, :], axis=1), N_EXPERTS - 1).astype(jnp.int32)
    tile_nv = jnp.clip(pstart[tile_e] + counts[tile_e] - tile_row0, 0, tm)
    tile_nv = jnp.where(tile_row0 < pend[-1], tile_nv, 0).astype(jnp.int32)
    return pstart, tile_e, tile_nv


def _slots(top_e, rank, pstart):
    experts = jnp.arange(N_EXPERTS, dtype=jnp.int32)
    base = jnp.sum(jnp.where(top_e[:, :TOP_K, None] == experts, pstart, 0), axis=-1)
    return (base + rank[:, :TOP_K]).astype(jnp.int32)


def _moe_experts(xs, tile_e, tile_nv, layer, w1, w2, b1g, b1l, b2):
    tm = MOE_TILE
    n_tiles = xs.shape[0] // tm
    d_ff = w2.shape[2]
    wspec = lambda a: pl.BlockSpec((1, 1) + a.shape[2:], lambda i, te, nv: (layer, te[i], 0, 0))
    return pl.pallas_call(
        _expert_kernel,
        grid_spec=pltpu.PrefetchScalarGridSpec(
            num_scalar_prefetch=2,
            grid=(n_tiles,),
            in_specs=[pl.BlockSpec((tm, D_MODEL), lambda i, te, nv: (i, 0))]
            + [wspec(a) for a in (w1, w2, b1g, b1l, b2)],
            out_specs=pl.BlockSpec((tm, D_MODEL), lambda i, te, nv: (i, 0)),
            scratch_shapes=[pltpu.VMEM((D_MODEL, 2 * d_ff), BF16), pltpu.VMEM((d_ff, D_MODEL), BF16)]),
        out_shape=jax.ShapeDtypeStruct(xs.shape, F32),
        compiler_params=_params(("arbitrary",)),
        name="moe_experts",
    )(tile_e, tile_nv, xs, w1, w2, b1g, b1l, b2)


def _combine_kernel(dest_ref, x_ref, tg_ref, g_ref, b_ref, ys_hbm, o_ref, ybuf, sem, *, tm):
    def row_copy(r, k, slot):
        return pltpu.make_async_copy(ys_hbm.at[pl.ds(slot, 1)], ybuf.at[k, pl.ds(r, 1)], sem)

    def start(r, c):
        for k in range(TOP_K):
            row_copy(r, k, dest_ref[0, 0, TOP_K * r + k]).start(priority=k % 2)
        return c

    def wait(r, c):
        for k in range(TOP_K):
            row_copy(0, k, 0).wait()
        return c

    lax.fori_loop(0, tm, start, 0)
    lax.fori_loop(0, tm, wait, 0)
    f = tg_ref[:, 0:1] * ybuf[0]
    for k in range(1, TOP_K):
        f = f + tg_ref[:, k:k + 1] * ybuf[k]
    o_ref[...] = _layernorm(DEEPNORM_ALPHA * x_ref[...] + f, g_ref[...], b_ref[...])


def _combine(x1, tg, dest, ys, prm, tm):
    t = x1.shape[0]
    row = lambda w: pl.BlockSpec((tm, w), lambda i: (i, 0))
    full = lambda a: pl.BlockSpec(a.shape, lambda i: (0,) * a.ndim)
    return pl.pallas_call(
        functools.partial(_combine_kernel, tm=tm),
        grid=(t // tm,),
        in_specs=[pl.BlockSpec((1, 1, TOP_K * tm), lambda i: (i, 0, 0), memory_space=pltpu.SMEM),
                  row(D_MODEL), row(LANE), full(prm['ln2_g']), full(prm['ln2_b']),
                  pl.BlockSpec(memory_space=pl.ANY)],
        out_specs=row(D_MODEL),
        out_shape=jax.ShapeDtypeStruct((t, D_MODEL), F32),
        scratch_shapes=[pltpu.VMEM((TOP_K, tm, D_MODEL), F32), pltpu.SemaphoreType.DMA],
        compiler_params=_params(("arbitrary",)),
        name="combine_ln",
    )(dest.reshape(t // tm, 1, TOP_K * tm), x1, tg, prm['ln2_g'], prm['ln2_b'], ys)


def _dt_pieces(dt_piece):
    return jnp.concatenate([dt_piece[:, 0:8], dt_piece[:, 128:136]], axis=1)


def _layer(xp, xs, prm, lb, rel_bias, st, n_p, seq, n_s, layer, moe, xs_buf):
    tp = n_p * seq
    up = _in_proj(xp, prm['w_in'], 128)
    dt16 = _dt_pieces(up['dt'])
    dt_t = jnp.transpose(dt16.reshape(tp // CHUNK, CHUNK, A_HEADS), (0, 2, 1))
    zeros_conv = jnp.zeros((n_p, SUBLANE, A_CONV_DIM), F32)
    ya, ssm_p = _ssd(up, dt_t, zeros_conv, jnp.zeros((n_p, D_MODEL, A_STATE), F32), prm, n_p, seq, False)
    yb, hg_p = _hgrn(up, jnp.zeros((n_p, D_MODEL, LANE), F32), lb, prm['hgrn_norm_g'], n_p, seq, False)
    attn = [_attn_prompt(up['cq'], up['ck'], up['cv'], _prompt_bias(rel_bias, g), g, n_p, seq)
            for g in range(3)]
    x1p, tep, tgp, rkp, cnt_p = _merge(xp, ya, yb, attn, up['gates'], jnp.zeros((1, LANE), F32), prm, 256)
    xbc3 = up['xbc'].reshape(n_p, seq, A_CONV_DIM)
    conv_p = xbc3[:, seq - (A_CONV - 1):]
    k3 = up['ck'].reshape(n_p, seq, 768)
    v3 = up['cv'].reshape(n_p, seq, 768)

    def last_rows(a, g, w):
        return a[:, seq - min(w, seq):, 256 * g:256 * (g + 1)].reshape(n_p, min(w, seq), 4, 64)

    kv_p = [jnp.stack([last_rows(k3, g, w), last_rows(v3, g, w)], axis=2) for g, (w, _) in enumerate(C_GROUPS)]
    us = _in_proj(xs, prm['w_in'], n_s)
    dt16s = _dt_pieces(us['dt'])
    dt_ts = jnp.zeros((n_s, A_HEADS, CHUNK), F32).at[:, :, 0].set(dt16s)
    conv0 = jnp.pad(st['conv'], ((0, 0), (SUBLANE - (A_CONV - 1), 0), (0, 0)))
    yas, ssm_s = _ssd(us, dt_ts, conv0, st['ssm'].reshape(n_s, D_MODEL, A_STATE), prm, n_s, 1, True)
    ybs, hg_s = _hgrn(us, st['hgrn'].reshape(n_s, D_MODEL, LANE), lb, prm['hgrn_norm_g'], n_s, 1, True)
    bias_d = [_decode_bias(rel_bias, g) for g in range(3)]
    o_s, lse_s = _attn_decode(us['cq'], us['ck'], us['cv'], st['kv_t'], layer,
                              [b[0] for b in bias_d], jnp.stack([b[1] for b in bias_d]))
    attn_s = [(o_s[:, 256 * g:256 * (g + 1)], lse_s[:, 256 * g:256 * (g + 1)]) for g in range(3)]
    x1s, tes, tgs, rks, cnt = _merge(xs, yas, ybs, attn_s, us['gates'], cnt_p, prm, n_s)
    conv_s = jnp.concatenate([st['conv'][:, 1:], us['xbc'][:, None]], axis=1)
    ks4 = us['ck'].reshape(n_s, 1, 3, 4, 64)
    vs4 = us['cv'].reshape(n_s, 1, 3, 4, 64)
    kv_s = [jnp.stack([ks4[:, :, g], vs4[:, :, g]], axis=2) for g in range(3)]
    n_tiles = -(-(tp + n_s) * TOP_K // MOE_TILE) + N_EXPERTS
    pstart, tile_e, tile_nv = _moe_plan(cnt[0, :N_EXPERTS].astype(jnp.int32), n_tiles)
    dest_p = _slots(tep, rkp, pstart)
    dest_s = _slots(tes, rks, pstart)
    x_sorted = jnp.zeros((n_tiles * MOE_TILE, D_MODEL), F32) if xs_buf is None else xs_buf
    x_sorted = _dispatch(x1p, dest_p, x_sorted, 256)
    x_sorted = _dispatch(x1s, dest_s, x_sorted, n_s)
    y_sorted = _moe_experts(x_sorted, tile_e, tile_nv, layer, *moe)
    yp = _combine(x1p, tgp, dest_p, y_sorted, prm, 256)
    ys = _combine(x1s, tgs, dest_s, y_sorted, prm, n_s)
    states_p = (conv_p, ssm_p.reshape(n_p, A_HEADS, A_HEAD_DIM, A_STATE),
                hg_p.reshape(n_p, B_HEADS, B_KEY_DIM, LANE), kv_p[0], kv_p[1], kv_p[2])
    states_s = (conv_s, ssm_s.reshape(n_s, A_HEADS, A_HEAD_DIM, A_STATE),
                hg_s.reshape(n_s, B_HEADS, B_KEY_DIM, LANE), kv_s[0], kv_s[1], kv_s[2])
    return yp, ys, states_p, states_s, x_sorted


def _prep_layer(l, w_in, conv_w, conv_b, dt_bias, a_log, d_skip, ssm_norm_g, hgrn_norm_g,
                w_branch_a, w_branch_b, w_branch_c, w_out, ln1_g, ln1_b, router_w, router_b,
                moe_w1, moe_b1, moe_w2, moe_b2, ln2_g, ln2_b):
    def lanes_per_group(v):
        return jnp.zeros((1, 256), F32).at[0, 0:8].set(v[:8]).at[0, 128:136].set(v[8:])

    def sublanes_per_group(v):
        return jnp.broadcast_to(v[:, None], (A_HEADS, LANE))

    return {
        'w_in': _pack_w_in(w_in[l]),
        'conv_w8': jnp.pad(conv_w[l], ((0, SUBLANE - A_CONV), (0, 0))),
        'conv_b': conv_b[l][None],
        'dt_bias_l': lanes_per_group(dt_bias[l]), 'a_log_l': lanes_per_group(a_log[l]),
        'dt_bias_t': sublanes_per_group(dt_bias[l]), 'a_log_t': sublanes_per_group(a_log[l]),
        'd_skip_l': jnp.repeat(d_skip[l], A_HEAD_DIM)[None],
        'ssm_norm_g': ssm_norm_g[l][None],
        'hgrn_norm_g': hgrn_norm_g[l][None],
        'w_branch_a': w_branch_a[l].astype(BF16), 'w_branch_b': w_branch_b[l].astype(BF16),
        'w_branch_c': w_branch_c[l].astype(BF16), 'w_out': w_out[l].astype(BF16),
        'ln1_g': ln1_g[l][None], 'ln1_b': ln1_b[l][None],
        'router_w': jnp.pad(router_w[l], ((0, 0), (0, LANE - N_EXPERTS))),
        'router_b': jnp.pad(router_b[l], (0, LANE - N_EXPERTS), constant_values=-jnp.inf)[None],
        'ln2_g': ln2_g[l][None], 'ln2_b': ln2_b[l][None],
    }


def kernel(x_prompt, x_sample, state_conv, state_ssm, state_hgrn, cache_kv_w128, cache_kv_w512, cache_kv_w2048, w_in, conv_w, conv_b, dt_bias, a_log, d_skip, ssm_norm_g, hgrn_lb, hgrn_norm_g, rel_bias, w_branch_a, w_branch_b, w_branch_c, w_out, ln1_g, ln1_b, router_w, router_b, moe_w1, moe_b1, moe_w2, moe_b2, ln2_g, ln2_b):
    n_p, seq, _ = x_prompt.shape
    n_s = x_sample.shape[0]
    depth = w_in.shape[0]
    p_lb = jax.nn.softmax(hgrn_lb.astype(F32), axis=0)
    lower_bounds = jnp.cumsum(p_lb, axis=0) - p_lb[0]
    yp = x_prompt.reshape(n_p * seq, D_MODEL)
    ys = x_sample.reshape(n_s, D_MODEL)
    st_p, st_s = [], []
    xs_buf = None
    kv_t = tuple(jnp.transpose(c, (0, 1, 3, 4, 5, 2)) for c in (cache_kv_w128, cache_kv_w512, cache_kv_w2048))
    moe = (moe_w1, moe_w2, moe_b1[:, :, None, 0::2], moe_b1[:, :, None, 1::2], moe_b2[:, :, None, :])
    for l in range(depth):
        prm = _prep_layer(l, w_in, conv_w, conv_b, dt_bias, a_log, d_skip, ssm_norm_g, hgrn_norm_g,
                          w_branch_a, w_branch_b, w_branch_c, w_out, ln1_g, ln1_b, router_w, router_b,
                          moe_w1, moe_b1, moe_w2, moe_b2, ln2_g, ln2_b)
        st = {'conv': state_conv[l], 'ssm': state_ssm[l], 'hgrn': state_hgrn[l], 'kv_t': kv_t}
        yp, ys, sp, ss, xs_buf = _layer(yp, ys, prm, lower_bounds[l][None], rel_bias, st, n_p, seq, n_s, l, moe,
                                        xs_buf)
        st_p.append(sp)
        st_s.append(ss)
    stack = lambda sts, i: jnp.stack([s[i] for s in sts], axis=0)
    return (yp.reshape(n_p, seq, D_MODEL), ys.reshape(n_s, 1, D_MODEL),
            stack(st_p, 0), stack(st_s, 0), stack(st_p, 1), stack(st_s, 1), stack(st_p, 2), stack(st_s, 2),
            stack(st_p, 3), stack(st_s, 3), stack(st_p, 4), stack(st_s, 4), stack(st_p, 5), stack(st_s, 5))
```

```python
import functools
import math

import jax
import jax.numpy as jnp
import numpy as np
from jax import lax
from jax.experimental import pallas as pl
from jax.experimental.pallas import tpu as pltpu

F32 = jnp.float32
BF16 = jnp.bfloat16
HI = lax.Precision.HIGHEST

D_MODEL = 1024
A_HEADS = 16
A_HEAD_DIM = 64
A_STATE = 128
A_CONV = 4
A_CONV_DIM = 1536
B_HEADS = 8
B_KEY_DIM = 128
C_GROUPS = ((128, 1), (512, 4), (2048, 16))
C_SPAN = 128
C_GROUP_WIDTH = 256
REL_BUCKETS = 32
REL_MAX_DIST = 2048
N_EXPERTS = 32
TOP_K = 4
SWIGLU_ALPHA = 1.702
SWIGLU_LIMIT = 7.0
DEEPNORM_ALPHA = (2.0 * 2) ** 0.25
LN_EPS = 1e-5
RMS_EPS = 1e-5
NEG_BIG = -1e30
LOG_FLOOR = 1e-30

LANE = 128
SUBLANE = 8
CHUNK = 128
SUB = 8
HGRN_HEADS_PER_STEP = 4
ATTN_ILP = 4
VMEM_LIMIT = 56 * 1024 * 1024

IN_PIECES = (('z', 1024), ('xbc', 1536), ('bq', 1024), ('bf', 1024), ('bi', 1024), ('bg', 1024),
             ('cq', 768), ('ck', 768), ('cv', 768), ('gates', 3072), ('dt', 256))
IN_PACKED = sum(w for _, w in IN_PIECES)


def _params(sem):
    return pltpu.CompilerParams(dimension_semantics=sem, vmem_limit_bytes=VMEM_LIMIT)


def _sigmoid(x):
    return 1.0 / (1.0 + jnp.exp(-x))


def _softplus(x):
    return jnp.maximum(x, 0.0) + jnp.log(1.0 + jnp.exp(-jnp.abs(x)))


def _dot(a, b):
    return jnp.dot(a.astype(BF16), b.astype(BF16), preferred_element_type=F32)


def _dot_nt(a, b):
    return lax.dot_general(a.astype(BF16), b.astype(BF16), (((1,), (1,)), ((), ())),
                           preferred_element_type=F32)


def _dot_hi(a, b):
    return jnp.dot(a, b, preferred_element_type=F32, precision=HI)


ROW_TILE = D_MODEL // LANE


def _store_row_tiles(ref, val):
    n = val.shape[0]
    for c in range(ROW_TILE):
        ref[pl.ds(c, n, stride=ROW_TILE), :] = val[:, LANE * c:LANE * (c + 1)]


def _load_row_tiles(ref, n, lead=None):
    rows = pl.ds
    parts = []
    for c in range(ROW_TILE):
        idx = (rows(c, n, stride=ROW_TILE), slice(None))
        parts.append(ref[idx] if lead is None else ref[(lead,) + idx])
    return jnp.concatenate(parts, axis=1)


def _layernorm(h, g, b):
    mu = jnp.mean(h, -1, keepdims=True)
    c = h - mu
    var = jnp.mean(c * c, -1, keepdims=True)
    return c * lax.rsqrt(var + LN_EPS) * g + b


def _in_proj_kernel(x_ref, w_ref, *o_refs):
    xb = x_ref[...].astype(BF16)
    off = 0
    for o_ref in o_refs:
        wd = o_ref.shape[1]
        o_ref[...] = jnp.dot(xb, w_ref[:, off:off + wd], preferred_element_type=F32)
        off += wd


def _in_proj(x, w_packed, tm):
    t = x.shape[0]
    outs = pl.pallas_call(
        _in_proj_kernel,
        grid=(t // tm,),
        in_specs=[pl.BlockSpec((tm, D_MODEL), lambda i: (i, 0)),
                  pl.BlockSpec(memory_space=pltpu.VMEM)],
        out_specs=[pl.BlockSpec((tm, w), lambda i: (i, 0)) for _, w in IN_PIECES],
        out_shape=[jax.ShapeDtypeStruct((t, w), F32) for _, w in IN_PIECES],
        compiler_params=_params(("arbitrary",)),
        name="in_proj",
    )(x, w_packed)
    return {name: o for (name, _), o in zip(IN_PIECES, outs)}


def _pack_w_in(w):
    dt = w[:, 2560:2576]
    dtp = jnp.zeros((D_MODEL, 256), F32).at[:, 0:8].set(dt[:, :8]).at[:, 128:136].set(dt[:, 8:])
    return jnp.concatenate([w[:, :2560], w[:, 2576:], dtp], axis=1).astype(BF16)


def _stage(ref, ci, decode, r):
    if decode:
        row = ref[pl.ds(r, 1), :]
        rows = lax.broadcasted_iota(jnp.int32, (CHUNK, row.shape[1]), 0)
        return jnp.where(rows == 0, jnp.broadcast_to(row, (CHUNK, row.shape[1])), 0.0)
    return ref[pl.ds(pl.multiple_of(ci * CHUNK, CHUNK), CHUNK), :]


def _unstage(ref, val, ci, decode, r):
    if decode:
        ref[pl.ds(r, 1), :] = val[0:1, :]
    else:
        ref[pl.ds(pl.multiple_of(ci * CHUNK, CHUNK), CHUNK), :] = val


def _ssd_kernel(xs_ref, bm_ref, cm_ref, z_ref, dt_ref, dtt_ref,
                c0x_ref, c0b_ref, c0c_ref, h0_ref,
                wx_ref, wb_ref, wc_ref, bx_ref, bb_ref, bc_ref,
                dtb_ref, alog_ref, dtbt_ref, alogt_ref, dsk_ref, ng_ref,
                y_ref, hn_ref,
                h_scr, cx_scr, cb_scr, cc_scr, px_scr, pb_scr, pc_scr, *, nchunks, decode):
    b = pl.program_id(1)
    t = pl.program_id(2)
    r = b % SUBLANE
    q = CHUNK

    @pl.when(t == 0)
    def _():
        h_scr[...] = h0_ref[0]
        cx_scr[...] = c0x_ref[0]
        cb_scr[...] = c0b_ref[0]
        cc_scr[...] = c0c_ref[0]

    rows = lax.broadcasted_iota(jnp.int32, (q, q), 0)
    cols = lax.broadcasted_iota(jnp.int32, (q, q), 1)
    tril = rows >= cols
    tril_f = tril.astype(F32)
    triu_f = (rows <= cols).astype(F32)
    lane = lax.broadcasted_iota(jnp.int32, (q, LANE), 1)
    lo = lane < A_HEAD_DIM
    row_lo = lax.broadcasted_iota(jnp.int32, (LANE, LANE), 0) < A_HEAD_DIM
    valid_col = lax.broadcasted_iota(jnp.int32, (q, LANE), 0) == 0
    valid_row = lax.broadcasted_iota(jnp.int32, (SUBLANE, q), 1) == 0

    def conv(x, carry, pad, w_ref, b_ref):
        pad[0:SUBLANE, :] = carry[...]
        pad[SUBLANE:SUBLANE + q, :] = x
        if not decode:
            carry[...] = pad[q:q + SUBLANE, :]
        acc = b_ref[...]
        for j in range(A_CONV):
            acc = acc + pad[5 + j:5 + j + q, :] * w_ref[j:j + 1, :]
        return acc * _sigmoid(acc)

    def chunk(ci, carry_unused):
        xs = conv(_stage(xs_ref, ci, decode, r), cx_scr, px_scr, wx_ref, bx_ref)
        bm = conv(_stage(bm_ref, ci, decode, r), cb_scr, pb_scr, wb_ref, bb_ref)
        cm = conv(_stage(cm_ref, ci, decode, r), cc_scr, pc_scr, wc_ref, bc_ref)
        a_lane = -jnp.exp(alog_ref[...])
        a_sub = -jnp.exp(alogt_ref[...])
        dt = _softplus(_stage(dt_ref, ci, decode, r) + dtb_ref[...])
        dtt = _softplus(dtt_ref[ci] + dtbt_ref[...])
        if decode:
            dt = jnp.where(valid_col, dt, 0.0)
            dtt = jnp.where(valid_row, dtt, 0.0)
        if decode:
            acum = jnp.broadcast_to((dt * a_lane)[0:1, :], (q, LANE))
            acum_t = jnp.broadcast_to((dtt * a_sub)[:, 0:1], (SUBLANE, q))
        else:
            acum = _dot_hi(tril_f, dt * a_lane)
            acum_t = _dot_hi(dtt * a_sub, triu_f)
        a_last = acum[q - 1:q, :]
        cb = _dot_nt(cm, bm)
        cm_b = cm.astype(BF16)
        bm_b = bm.astype(BF16)
        ys = []
        for j in range(4):
            xp = xs[:, LANE * j:LANE * (j + 1)]
            xp_b = xp.astype(BF16)
            yd = []
            for half in range(2):
                hl = 2 * j + half
                diff = acum[:, hl:hl + 1] - acum_t[hl:hl + 1, :]
                lm = jnp.exp(jnp.where(tril, diff, NEG_BIG))
                m = cb * lm * dtt[hl:hl + 1, :]
                yd.append(jnp.dot(m.astype(BF16), xp_b, preferred_element_type=F32))
            y = jnp.where(lo, yd[0], yd[1])
            hp = h_scr[LANE * j:LANE * (j + 1), :]
            yo = lax.dot_general(cm_b, hp.astype(BF16), (((1,), (1,)), ((), ())),
                                 preferred_element_type=F32)
            e0 = acum[:, 2 * j:2 * j + 1]
            e1 = acum[:, 2 * j + 1:2 * j + 2]
            y = y + yo * jnp.exp(jnp.where(lo, e0, e1))
            w0 = dt[:, 2 * j:2 * j + 1] * jnp.exp(a_last[:, 2 * j:2 * j + 1] - e0)
            w1 = dt[:, 2 * j + 1:2 * j + 2] * jnp.exp(a_last[:, 2 * j + 1:2 * j + 2] - e1)
            xw = xp * jnp.where(lo, w0, w1)
            upd = jnp.dot(xw.T.astype(BF16), bm_b, preferred_element_type=F32)
            dec = jnp.exp(jnp.where(row_lo, a_last[:, 2 * j:2 * j + 1], a_last[:, 2 * j + 1:2 * j + 2]))
            h_scr[LANE * j:LANE * (j + 1), :] = hp * dec + upd
            ys.append(y)
        y = jnp.concatenate(ys, axis=1) + dsk_ref[...] * xs
        z = _stage(z_ref, ci, decode, r)
        y = y * (z * _sigmoid(z))
        y = y * lax.rsqrt(jnp.mean(y * y, -1, keepdims=True) + RMS_EPS) * ng_ref[...]
        _unstage(y_ref, y, ci, decode, r)
        return carry_unused

    lax.fori_loop(0, nchunks, chunk, 0)

    @pl.when(t == pl.num_programs(2) - 1)
    def _():
        hn_ref[0] = h_scr[...]


def _ssd(u, dt_t, conv0_8, h0, prm, n, seq, decode):
    tb = CHUNK if decode else min(512, seq)
    nb = 1 if decode else seq // tb
    nchunks = tb // CHUNK
    rb = SUBLANE if decode else tb

    def rowmap(lane_block):
        if decode:
            return lambda g, b, t: (b // SUBLANE, lane_block(g))
        return lambda g, b, t: (b * nb + t, lane_block(g))

    chunk_map = (lambda g, b, t: (b, g, 0)) if decode else (lambda g, b, t: (b * nb + t, g, 0))
    cw = (512, LANE, LANE)
    lane_blocks = (lambda g: g, lambda g: 8 + g, lambda g: 10 + g)
    in_specs = [
        pl.BlockSpec((rb, 512), rowmap(lane_blocks[0])),
        pl.BlockSpec((rb, LANE), rowmap(lane_blocks[1])),
        pl.BlockSpec((rb, LANE), rowmap(lane_blocks[2])),
        pl.BlockSpec((rb, 512), rowmap(lambda g: g)),
        pl.BlockSpec((rb, LANE), rowmap(lambda g: g)),
        pl.BlockSpec((nchunks, SUBLANE, CHUNK), chunk_map),
    ]
    in_specs += [pl.BlockSpec((1, SUBLANE, w), (lambda lb: (lambda g, b, t: (b, 0, lb(g))))(lb))
                 for w, lb in zip(cw, lane_blocks)]
    in_specs += [pl.BlockSpec((1, 512, A_STATE), lambda g, b, t: (b, g, 0))]
    in_specs += [pl.BlockSpec((SUBLANE, w), (lambda lb: (lambda g, b, t: (0, lb(g))))(lb))
                 for w, lb in zip(cw, lane_blocks)]
    in_specs += [pl.BlockSpec((1, w), (lambda lb: (lambda g, b, t: (0, lb(g))))(lb))
                 for w, lb in zip(cw, lane_blocks)]
    in_specs += [
        pl.BlockSpec((1, LANE), lambda g, b, t: (0, g)),
        pl.BlockSpec((1, LANE), lambda g, b, t: (0, g)),
        pl.BlockSpec((SUBLANE, LANE), lambda g, b, t: (g, 0)),
        pl.BlockSpec((SUBLANE, LANE), lambda g, b, t: (g, 0)),
        pl.BlockSpec((1, 512), lambda g, b, t: (0, g)),
        pl.BlockSpec((1, 512), lambda g, b, t: (0, g)),
    ]
    rows_total = u['z'].shape[0]
    y, hn = pl.pallas_call(
        functools.partial(_ssd_kernel, nchunks=nchunks, decode=decode),
        grid=(2, n, nb),
        in_specs=in_specs,
        out_specs=[pl.BlockSpec((rb, 512), rowmap(lambda g: g)),
                   pl.BlockSpec((1, 512, A_STATE), lambda g, b, t: (b, g, 0))],
        out_shape=[jax.ShapeDtypeStruct((rows_total, D_MODEL), F32),
                   jax.ShapeDtypeStruct((n, D_MODEL, A_STATE), F32)],
        scratch_shapes=[pltpu.VMEM((512, A_STATE), F32),
                        pltpu.VMEM((SUBLANE, 512), F32), pltpu.VMEM((SUBLANE, LANE), F32),
                        pltpu.VMEM((SUBLANE, LANE), F32),
                        pltpu.VMEM((CHUNK + SUBLANE, 512), F32), pltpu.VMEM((CHUNK + SUBLANE, LANE), F32),
                        pltpu.VMEM((CHUNK + SUBLANE, LANE), F32)],
        compiler_params=_params(("arbitrary", "arbitrary", "arbitrary")),
        name="ssd_decode" if decode else "ssd_prompt",
    )(u['xbc'], u['xbc'], u['xbc'], u['z'], u['dt'], dt_t,
      conv0_8, conv0_8, conv0_8, h0,
      prm['conv_w8'], prm['conv_w8'], prm['conv_w8'], prm['conv_b'], prm['conv_b'], prm['conv_b'],
      prm['dt_bias_l'], prm['a_log_l'], prm['dt_bias_t'], prm['a_log_t'], prm['d_skip_l'], prm['ssm_norm_g'])
    return y, hn


def _hgrn_kernel(q_ref, f_ref, i_ref, g_ref, lb_ref, ng_ref, s0_ref, y_ref, sn_ref, st_scr,
                 *, nchunks, decode, nheads):
    b = pl.program_id(1)
    t = pl.program_id(2)
    r = b % SUBLANE
    c = CHUNK

    @pl.when(t == 0)
    def _():
        for hh in range(nheads):
            st_scr[hh] = s0_ref[0, LANE * hh:LANE * (hh + 1), :].T

    rows = lax.broadcasted_iota(jnp.int32, (c, c), 0)
    cols = lax.broadcasted_iota(jnp.int32, (c, c), 1)
    tril_f = (rows >= cols).astype(F32)
    row_c = lax.broadcasted_iota(jnp.int32, (c, LANE), 0)
    row_s = lax.broadcasted_iota(jnp.int32, (SUB, LANE), 0)
    blk_xor = (rows // SUB) ^ (cols // SUB)
    level = jnp.where(blk_xor >= 8, 3, jnp.where(blk_xor >= 4, 2, jnp.where(blk_xor >= 2, 1, 0)))
    level = jnp.where((rows // SUB) > (cols // SUB), level, -1)
    nlevels = 4
    assert SUB << nlevels == c

    def one_head(hh, ci):
        ls = slice(LANE * hh, LANE * (hh + 1))
        lbv = lb_ref[:, ls]
        f = _stage(f_ref, ci, decode, r)[:, ls]
        qr = _stage(q_ref, ci, decode, r)[:, ls]
        v = _stage(i_ref, ci, decode, r)[:, ls]
        gr = _stage(g_ref, ci, decode, r)[:, ls]
        q = qr * _sigmoid(qr) * (B_KEY_DIM ** -0.5)
        e_f = jnp.exp(-jnp.abs(f))
        r_f = 1.0 / (1.0 + e_f)
        pos = f >= 0.0
        sig_p = jnp.where(pos, r_f, e_f * r_f)
        sig_n = jnp.where(pos, e_f * r_f, r_f)
        logf = jnp.log(jnp.maximum(lbv + (1.0 - lbv) * sig_p, LOG_FLOOR))
        k = (1.0 - lbv) * sig_n
        if decode:
            logf = jnp.where(row_c == 0, logf, 0.0)
            k = jnp.where(row_c == 0, k, 0.0)
        if decode:
            bcum = jnp.broadcast_to(logf[0:1, :], (c, LANE))
        else:
            bcum = _dot_hi(tril_f, logf)
        b_last = bcum[c - 1:c, :]
        st = st_scr[hh]
        o = _dot_nt(q * jnp.exp(bcum), st)
        if not decode:
            att = jnp.zeros((c, c), F32)
            for lv in range(nlevels):
                h = SUB << lv
                ref = jnp.concatenate(
                    [jnp.broadcast_to(bcum[2 * h * m + h - 1:2 * h * m + h, :], (2 * h, LANE))
                     for m in range(c // (2 * h))], axis=0)
                x = jnp.exp(-jnp.abs(bcum - ref))
                att = jnp.where(level == lv, _dot_nt(q * x, k * x), att)
            o = o + _dot(att, v)
        diag = []
        for i in range(c // SUB):
            if decode and i > 0:
                diag.append(jnp.zeros((SUB, LANE), F32))
                continue
            sl = slice(SUB * i, SUB * (i + 1))
            qb, kb, vb, bb = q[sl, :], k[sl, :], v[sl, :], bcum[sl, :]
            od = jnp.zeros((SUB, LANE), F32)
            for s in range(1 if decode else SUB):
                e = jnp.exp(jnp.where(row_s >= s, bb - bb[s:s + 1, :], NEG_BIG))
                rs = jnp.sum(qb * kb[s:s + 1, :] * e, -1, keepdims=True)
                od = od + rs * vb[s:s + 1, :]
            diag.append(od)
        o = o + jnp.concatenate(diag, axis=0)
        k2 = k * jnp.exp(b_last - bcum)
        st_scr[hh] = jnp.exp(b_last) * st + jnp.dot(v.T.astype(BF16), k2.astype(BF16),
                                                     preferred_element_type=F32)
        gate = gr * _sigmoid(gr)
        return o * lax.rsqrt(jnp.mean(o * o, -1, keepdims=True) + RMS_EPS) * ng_ref[...] * gate

    def chunk(ci, carry_unused):
        y = [one_head(hh, ci) for hh in range(nheads)]
        _unstage(y_ref, y[0] if nheads == 1 else jnp.concatenate(y, axis=1), ci, decode, r)
        return carry_unused

    lax.fori_loop(0, nchunks, chunk, 0)

    @pl.when(t == pl.num_programs(2) - 1)
    def _():
        for hh in range(nheads):
            sn_ref[0, LANE * hh:LANE * (hh + 1), :] = st_scr[hh].T


def _hgrn(u, s0, lb, ng, n, seq, decode):
    tb = CHUNK if decode else min(512, seq)
    nb = 1 if decode else seq // tb
    nchunks = tb // CHUNK
    rb = SUBLANE if decode else tb
    rowmap = (lambda h, b, t: (b // SUBLANE, h)) if decode else (lambda h, b, t: (b * nb + t, h))
    rows_total = u['bq'].shape[0]
    nheads = B_HEADS if decode else HGRN_HEADS_PER_STEP
    wd = LANE * nheads
    y, sn = pl.pallas_call(
        functools.partial(_hgrn_kernel, nchunks=nchunks, decode=decode, nheads=nheads),
        grid=(B_HEADS // nheads, n, nb),
        in_specs=[pl.BlockSpec((rb, wd), rowmap)] * 4 + [
            pl.BlockSpec((1, wd), lambda h, b, t: (0, h)),
            pl.BlockSpec((1, LANE), lambda h, b, t: (0, 0)),
            pl.BlockSpec((1, B_KEY_DIM * nheads, LANE), lambda h, b, t: (b, h, 0))],
        out_specs=[pl.BlockSpec((rb, wd), rowmap),
                   pl.BlockSpec((1, B_KEY_DIM * nheads, LANE), lambda h, b, t: (b, h, 0))],
        out_shape=[jax.ShapeDtypeStruct((rows_total, D_MODEL), F32),
                   jax.ShapeDtypeStruct((n, D_MODEL, LANE), F32)],
        scratch_shapes=[pltpu.VMEM((nheads, LANE, B_KEY_DIM), F32)],
        compiler_params=_params(("arbitrary", "arbitrary", "arbitrary")),
        name="hgrn_decode" if decode else "hgrn_prompt",
    )(u['bq'], u['bf'], u['bi'], u['bg'], lb, ng, s0)
    return y, sn


def _t5_bucket_np(dist):
    exact = REL_BUCKETS // 2
    d = np.maximum(dist, 1).astype(np.float32)
    large = exact + (np.log(d / np.float32(exact)) / np.float32(math.log(REL_MAX_DIST / exact))
                     * np.float32(REL_BUCKETS - exact)).astype(np.int32)
    large = np.clip(large, 0, REL_BUCKETS - 1)
    return np.where(dist < exact, dist, large)


def _attn_prompt_kernel(q_ref, kc_ref, kp_ref, vc_ref, vp_ref, bias_ref, o_ref, lse_ref, *, dil):
    first = pl.program_id(1) == 0
    s_q = C_SPAN
    lane = lax.broadcasted_iota(jnp.int32, (s_q, LANE), 1)
    lo = lane < 64
    kcol = lax.broadcasted_iota(jnp.int32, (s_q, 2 * s_q), 1)
    no_prev = jnp.logical_and(first, kcol < s_q)

    pair = pl.program_id(2)

    def solve(qp, kprev, kcur, vprev, vcur, mask_prev):
        kp2 = jnp.concatenate([kprev, kcur], axis=0).astype(BF16)
        vp2 = jnp.concatenate([vprev, vcur], axis=0).astype(BF16)
        oh, lh = [], []
        for half in range(2):
            qm = jnp.where(lo if half == 0 else jnp.logical_not(lo), qp, 0.0).astype(BF16)
            s = lax.dot_general(qm, kp2, (((1,), (1,)), ((), ())), preferred_element_type=F32)
            s = s * (64 ** -0.5) + bias_ref[2 * pair + half]
            if mask_prev:
                s = jnp.where(no_prev, NEG_BIG, s)
            m = jnp.max(s, -1, keepdims=True)
            p = jnp.exp(s - m)
            den = jnp.sum(p, -1, keepdims=True)
            oh.append(jnp.dot((p / den).astype(BF16), vp2, preferred_element_type=F32))
            lh.append(m + jnp.log(den))
        return jnp.where(lo, oh[0], oh[1]), jnp.where(lo, lh[0], lh[1])

    if dil == 1:
        for sb in range(ATTN_ILP):
            cur = slice(s_q * sb, s_q * (sb + 1))
            if sb == 0:
                o, lse = solve(q_ref[cur, :], kp_ref[...], kc_ref[cur, :], vp_ref[...], vc_ref[cur, :], True)
            else:
                prv = slice(s_q * (sb - 1), s_q * sb)
                o, lse = solve(q_ref[cur, :], kc_ref[prv, :], kc_ref[cur, :], vc_ref[prv, :], vc_ref[cur, :],
                               False)
            o_ref[cur, :] = o
            lse_ref[cur, :] = lse
    else:
        def residues(it, carry):
            for jj in range(ATTN_ILP):
                sl = pl.ds(it * ATTN_ILP + jj, s_q, stride=dil)
                o, lse = solve(q_ref[sl, :], kp_ref[sl, :], kc_ref[sl, :], vp_ref[sl, :], vc_ref[sl, :], True)
                o_ref[sl, :] = o
                lse_ref[sl, :] = lse
            return carry

        lax.fori_loop(0, dil // ATTN_ILP, residues, 0)


def _attn_prompt(cq, ck, cv, bias, g, n, seq):
    dil = C_GROUPS[g][1]
    look = C_SPAN * dil
    rows_blk = look * (ATTN_ILP if dil == 1 else 1)
    nbk = seq // rows_blk
    per = rows_blk // look
    cur = lambda b, i, j: (b * nbk + i, 2 * g + j)
    prev = lambda b, i, j: (jnp.maximum((b * nbk + i) * per - 1, 0), 2 * g + j)
    blk = (rows_blk, LANE)
    pblk = (look, LANE)
    return pl.pallas_call(
        functools.partial(_attn_prompt_kernel, dil=dil),
        grid=(n, nbk, 2),
        in_specs=[pl.BlockSpec(blk, cur), pl.BlockSpec(blk, cur), pl.BlockSpec(pblk, prev),
                  pl.BlockSpec(blk, cur), pl.BlockSpec(pblk, prev),
                  pl.BlockSpec((4, C_SPAN, 2 * C_SPAN), lambda b, i, j: (0, 0, 0))],
        out_specs=[pl.BlockSpec(blk, lambda b, i, j: (b * nbk + i, j))] * 2,
        out_shape=[jax.ShapeDtypeStruct((n * seq, C_GROUP_WIDTH), F32)] * 2,
        compiler_params=_params(("arbitrary", "arbitrary", "arbitrary")),
        name=f"attn_prompt_g{g}",
    )(cq, ck, ck, cv, cv, bias)


def _prompt_bias(rel_bias, g):
    dil = C_GROUPS[g][1]
    qi = np.arange(C_SPAN)[:, None]
    kj = np.arange(2 * C_SPAN)[None, :]
    rel = qi + C_SPAN - kj
    band = (rel >= 0) & (rel <= C_SPAN)
    idx = _t5_bucket_np(np.maximum(rel, 0) * dil)
    tab = rel_bias[:, 4 * g:4 * g + 4]
    hit = jnp.asarray(idx)[None, :, :, None] == jnp.arange(REL_BUCKETS)
    vals = jnp.sum(jnp.where(hit, tab.T[:, None, None, :], 0.0), axis=-1)
    return jnp.where(band[None], vals, NEG_BIG)


def _decode_bias(rel_bias, g):
    win, dil = C_GROUPS[g]
    tab = rel_bias[:, 4 * g:4 * g + 4]
    pos = np.arange(win)
    vals = jnp.where((pos % dil == 0)[:, None], tab[_t5_bucket_np(win - pos)], NEG_BIG)
    bias_buf = jnp.zeros((2, SUBLANE, win), F32).at[:, 0:2].set(vals.T.reshape(2, 2, win))
    new = jnp.broadcast_to(tab[0].reshape(2, 2, 1), (2, 2, LANE))
    bias_new = jnp.zeros((2, SUBLANE, LANE), F32).at[:, 0:2].set(new)
    return bias_buf, bias_new


def _attn_decode_kernel(q_ref, k_ref, v_ref, c0_ref, c1_ref, c2_ref, bb0_ref, bb1_ref, bb2_ref, bn_ref,
                        o_ref, lse_ref):
    b = pl.program_id(0)
    row8 = lax.broadcasted_iota(jnp.int32, (SUBLANE, LANE), 0)
    lane8 = lax.broadcasted_iota(jnp.int32, (SUBLANE, LANE), 1)
    qmask = jnp.logical_or(jnp.logical_and(row8 == 0, lane8 < 64), jnp.logical_and(row8 == 1, lane8 >= 64))
    lo1 = lax.broadcasted_iota(jnp.int32, (1, LANE), 1) < 64
    q_all = q_ref[pl.ds(b, 1), :]
    k_all = k_ref[pl.ds(b, 1), :]
    v_all = v_ref[pl.ds(b, 1), :]
    o_parts, lse_parts = [], []
    for g, (buf_ref, bb_ref) in enumerate(((c0_ref, bb0_ref), (c1_ref, bb1_ref), (c2_ref, bb2_ref))):
        win = C_GROUPS[g][0]
        for j in range(2):
            c0 = C_GROUP_WIDTH * g + LANE * j
            qrow = q_all[:, c0:c0 + LANE]
            knew = k_all[:, c0:c0 + LANE]
            vnew = v_all[:, c0:c0 + LANE]
            q8 = jnp.where(qmask, jnp.broadcast_to(qrow, (SUBLANE, LANE)), 0.0)
            kt = buf_ref[0, 0, 0, 2 * j:2 * j + 2].reshape(LANE, win)
            vt = buf_ref[0, 0, 1, 2 * j:2 * j + 2].reshape(LANE, win)
            s = _dot(q8, kt) * (64 ** -0.5) + bb_ref[j]
            snew = jnp.sum(q8 * knew, -1, keepdims=True) * (64 ** -0.5) + bn_ref[g, j][:, 0:1]
            m = jnp.maximum(jnp.max(s, -1, keepdims=True), snew)
            p = jnp.exp(s - m)
            pn = jnp.exp(snew - m)
            den = jnp.sum(p, -1, keepdims=True) + pn
            o8 = _dot_nt(p / den, vt) + (pn / den) * vnew
            lse8 = jnp.broadcast_to(m + jnp.log(den), (SUBLANE, LANE))
            o_parts.append(jnp.where(lo1, o8[0:1, :], o8[1:2, :]))
            lse_parts.append(jnp.where(lo1, lse8[0:1, :], lse8[1:2, :]))
    o_ref[pl.ds(b, 1), :] = jnp.concatenate(o_parts, axis=1)
    lse_ref[pl.ds(b, 1), :] = jnp.concatenate(lse_parts, axis=1)


def _attn_decode(cq, ck, cv, caches_t, layer, bias_bufs, bias_new):
    n = cq.shape[0]
    full = lambda a: pl.BlockSpec(a.shape, lambda b: (0,) * a.ndim)
    cache_spec = lambda c: pl.BlockSpec((1, 1) + c.shape[2:], lambda b: (layer, b, 0, 0, 0, 0))
    return pl.pallas_call(
        _attn_decode_kernel,
        grid=(n,),
        in_specs=[full(cq), full(ck), full(cv)] + [cache_spec(c) for c in caches_t]
        + [full(bb) for bb in bias_bufs] + [full(bias_new)],
        out_specs=[pl.BlockSpec((n, 768), lambda b: (0, 0))] * 2,
        out_shape=[jax.ShapeDtypeStruct((n, 768), F32)] * 2,
        compiler_params=_params(("arbitrary",)),
        name="attn_decode",
    )(cq, ck, cv, *caches_t, *bias_bufs, bias_new)


def _merge_kernel(x_ref, ya_ref, yb_ref, o0_ref, o1_ref, o2_ref, l0_ref, l1_ref, l2_ref, gt_ref,
                  wa_ref, wb_ref, wc_ref, wo_ref, g_ref, b_ref, rw_ref, rb_ref, cnt0_ref,
                  x1_ref, te_ref, tg_ref, rk_ref, cnt_ref, cnt_scr):
    l0, l1, l2 = l0_ref[...], l1_ref[...], l2_ref[...]
    lm = jnp.maximum(jnp.maximum(l0, l1), l2)
    e0, e1, e2 = jnp.exp(l0 - lm), jnp.exp(l1 - lm), jnp.exp(l2 - lm)
    den = e0 + e1 + e2
    yc = (e0 / den) * o0_ref[...] + (e1 / den) * o1_ref[...] + (e2 / den) * o2_ref[...]
    ga = _sigmoid(gt_ref[:, 0:D_MODEL])
    gb = _sigmoid(gt_ref[:, D_MODEL:2 * D_MODEL])
    gc = _sigmoid(gt_ref[:, 2 * D_MODEL:3 * D_MODEL])
    merged = (ga * _dot(ya_ref[...], wa_ref[...]) + gb * _dot(yb_ref[...], wb_ref[...])
              + gc * _dot(yc, wc_ref[...]))
    h = DEEPNORM_ALPHA * x_ref[...] + _dot(merged, wo_ref[...])
    x1 = _layernorm(h, g_ref[...], b_ref[...])
    _store_row_tiles(x1_ref, x1)
    logits = _dot(x1, rw_ref[...]) + rb_ref[...]
    lane = lax.broadcasted_iota(jnp.int32, logits.shape, 1)
    lane_f = lane.astype(F32)
    te = jnp.zeros(logits.shape, F32)
    vals, onehots = [], []
    for k in range(TOP_K):
        m = jnp.max(logits, -1, keepdims=True)
        idx = jnp.min(jnp.where(logits == m, lane_f, float(LANE)), -1, keepdims=True)
        te = jnp.where(lane == k, idx, te)
        vals.append(m)
        hit = lane_f == idx
        onehots.append(hit.astype(F32))
        logits = jnp.where(hit, -jnp.inf, logits)
    ex = [jnp.exp(v - vals[0]) for v in vals]
    tot = ex[0] + ex[1] + ex[2] + ex[3]
    tg = jnp.zeros(logits.shape, F32)
    for k in range(TOP_K):
        tg = jnp.where(lane == k, ex[k] / tot, tg)
    te_ref[...] = te.astype(jnp.int32)
    tg_ref[...] = tg
    @pl.when(pl.program_id(0) == 0)
    def _():
        cnt_scr[...] = cnt0_ref[...]

    tm = logits.shape[0]
    oh = onehots[0] + onehots[1] + onehots[2] + onehots[3]
    earlier = (lax.broadcasted_iota(jnp.int32, (tm, tm), 0)
               > lax.broadcasted_iota(jnp.int32, (tm, tm), 1)).astype(BF16)
    before = jnp.dot(earlier, oh.astype(BF16), preferred_element_type=F32) + cnt_scr[...]
    rank = jnp.zeros(logits.shape, F32)
    for k in range(TOP_K):
        rank = jnp.where(lane == k, jnp.sum(onehots[k] * before, -1, keepdims=True), rank)
    rk_ref[...] = rank.astype(jnp.int32)
    cnt_scr[...] = cnt_scr[...] + jnp.sum(oh, 0, keepdims=True)
    cnt_ref[...] = cnt_scr[...]


def _merge(x, ya, yb, attn, gates, cnt0, prm, tm):
    t = x.shape[0]
    row = lambda w: pl.BlockSpec((tm, w), lambda i: (i, 0))
    full = lambda a: pl.BlockSpec(a.shape, lambda i: (0,) * a.ndim)
    ws = [prm['w_branch_a'], prm['w_branch_b'], prm['w_branch_c'], prm['w_out'],
          prm['ln1_g'], prm['ln1_b'], prm['router_w'], prm['router_b'], cnt0]
    (o0, l0), (o1, l1), (o2, l2) = attn
    return pl.pallas_call(
        _merge_kernel,
        grid=(t // tm,),
        in_specs=[row(D_MODEL)] * 3 + [row(C_GROUP_WIDTH)] * 6 + [row(3 * D_MODEL)] + [full(a) for a in ws],
        out_specs=[pl.BlockSpec((tm * ROW_TILE, LANE), lambda i: (i, 0)), row(LANE), row(LANE), row(LANE),
                   pl.BlockSpec((1, LANE), lambda i: (0, 0))],
        out_shape=[jax.ShapeDtypeStruct((t * ROW_TILE, LANE), F32), jax.ShapeDtypeStruct((t, LANE), jnp.int32),
                   jax.ShapeDtypeStruct((t, LANE), F32), jax.ShapeDtypeStruct((t, LANE), jnp.int32),
                   jax.ShapeDtypeStruct((1, LANE), F32)],
        scratch_shapes=[pltpu.VMEM((1, LANE), F32)],
        compiler_params=_params(("arbitrary",)),
        name="merge_ln_router",
    )(x, ya, yb, o0, o1, o2, l0, l1, l2, gates, *ws)


MOE_TILE = 512


def _dispatch_kernel(dest_ref, x_ref, xs_in_hbm, xs_hbm, sem, *, tm):
    del xs_in_hbm

    def row_copy(r, slot):
        src = x_ref.at[pl.ds(pl.multiple_of(r * ROW_TILE, ROW_TILE), ROW_TILE)]
        dst = xs_hbm.at[pl.ds(pl.multiple_of(slot * ROW_TILE, ROW_TILE), ROW_TILE)]
        return pltpu.make_async_copy(src, dst, sem)

    def start(r, c):
        for k in range(TOP_K):
            row_copy(r, dest_ref[0, 0, TOP_K * r + k]).start(priority=k % 2)
        return c

    def wait(r, c):
        for _ in range(TOP_K):
            row_copy(0, 0).wait()
        return c

    lax.fori_loop(0, tm, start, 0, unroll=8)
    lax.fori_loop(0, tm, wait, 0, unroll=8)


def _dispatch(x1, dest, xs, tm):
    t = x1.shape[0] // ROW_TILE
    return pl.pallas_call(
        functools.partial(_dispatch_kernel, tm=tm),
        grid=(t // tm,),
        in_specs=[pl.BlockSpec((1, 1, TOP_K * tm), lambda i: (i, 0, 0), memory_space=pltpu.SMEM),
                  pl.BlockSpec((tm * ROW_TILE, LANE), lambda i: (i, 0)),
                  pl.BlockSpec(memory_space=pl.ANY)],
        out_specs=pl.BlockSpec(memory_space=pl.ANY),
        out_shape=jax.ShapeDtypeStruct(xs.shape, F32),
        scratch_shapes=[pltpu.SemaphoreType.DMA],
        input_output_aliases={2: 0},
        compiler_params=_params(("arbitrary",)),
        name="moe_dispatch",
    )(dest.reshape(t // tm, 1, TOP_K * tm), x1, xs)


def _expert_kernel(te_ref, nv_ref, x_ref, w1_ref, w2_ref, b1g_ref, b1l_ref, b2_ref, y_ref, w1p_scr, w2b_scr):
    i = pl.program_id(0)
    nv = nv_ref[i]
    changed = jnp.logical_or(i == 0, te_ref[i] != te_ref[jnp.maximum(i - 1, 0)])
    half = LANE
    blk = 2 * LANE
    d_ff = w2b_scr.shape[0]

    @pl.when(jnp.logical_and(nv > 0, changed))
    def _():
        src_r = lax.broadcasted_iota(jnp.int32, (blk, blk), 0)
        dst_c = lax.broadcasted_iota(jnp.int32, (blk, blk), 1)
        pick = jnp.where(dst_c < half, 2 * dst_c, 2 * (dst_c - half) + 1)
        perm = (src_r == pick).astype(BF16)
        for c in range(2 * d_ff // blk):
            wb = w1_ref[0, 0, :, blk * c:blk * (c + 1)].astype(BF16)
            w1p_scr[:, blk * c:blk * (c + 1)] = jnp.dot(wb, perm, preferred_element_type=F32).astype(BF16)
        w2b_scr[...] = w2_ref[0, 0].astype(BF16)

    @pl.when(nv > 0)
    def _():
        tm = x_ref.shape[0] // ROW_TILE
        x = _load_row_tiles(x_ref, tm).astype(BF16)
        u = jnp.dot(x, w1p_scr[...], preferred_element_type=F32)
        nblk = 2 * d_ff // blk
        ug = jnp.concatenate([u[:, blk * c:blk * c + half] for c in range(nblk)], axis=1) + b1g_ref[0, 0]
        ul = jnp.concatenate([u[:, blk * c + half:blk * (c + 1)] for c in range(nblk)], axis=1) + b1l_ref[0, 0]
        glu = jnp.minimum(ug, SWIGLU_LIMIT)
        lin = jnp.clip(ul, -SWIGLU_LIMIT, SWIGLU_LIMIT)
        act = glu * _sigmoid(SWIGLU_ALPHA * glu) * (lin + 1.0)
        _store_row_tiles(y_ref, jnp.dot(act.astype(BF16), w2b_scr[...], preferred_element_type=F32)
                         + b2_ref[0, 0])

    @pl.when(nv == 0)
    def _():
        y_ref[...] = jnp.zeros(y_ref.shape, F32)


def _moe_plan(counts, n_tiles):
    tm = MOE_TILE
    padded = (counts + tm - 1) // tm * tm
    pend = jnp.cumsum(padded)
    pstart = pend - padded
    tile_row0 = jnp.arange(n_tiles, dtype=jnp.int32) * tm
    tile_e = jnp.minimum(jnp.sum(tile_row0[:, None] >= pend[None, :], axis=1), N_EXPERTS - 1).astype(jnp.int32)
    tile_nv = jnp.clip(pstart[tile_e] + counts[tile_e] - tile_row0, 0, tm)
    tile_nv = jnp.where(tile_row0 < pend[-1], tile_nv, 0).astype(jnp.int32)
    return pstart, tile_e, tile_nv


def _slots(top_e, rank, pstart):
    experts = jnp.arange(N_EXPERTS, dtype=jnp.int32)
    base = jnp.sum(jnp.where(top_e[:, :TOP_K, None] == experts, pstart, 0), axis=-1)
    return (base + rank[:, :TOP_K]).astype(jnp.int32)


def _moe_experts(xs, tile_e, tile_nv, layer, w1, w2, b1g, b1l, b2):
    tm = MOE_TILE
    n_tiles = xs.shape[0] // (tm * ROW_TILE)
    d_ff = w2.shape[2]
    rows_spec = pl.BlockSpec((tm * ROW_TILE, LANE), lambda i, te, nv: (i, 0))
    wspec = lambda a: pl.BlockSpec((1, 1) + a.shape[2:], lambda i, te, nv: (layer, te[i], 0, 0))
    return pl.pallas_call(
        _expert_kernel,
        grid_spec=pltpu.PrefetchScalarGridSpec(
            num_scalar_prefetch=2,
            grid=(n_tiles,),
            in_specs=[rows_spec] + [wspec(a) for a in (w1, w2, b1g, b1l, b2)],
            out_specs=rows_spec,
            scratch_shapes=[pltpu.VMEM((D_MODEL, 2 * d_ff), BF16), pltpu.VMEM((d_ff, D_MODEL), BF16)]),
        out_shape=jax.ShapeDtypeStruct(xs.shape, F32),
        compiler_params=_params(("arbitrary",)),
        name="moe_experts",
    )(tile_e, tile_nv, xs, w1, w2, b1g, b1l, b2)


def _combine_kernel(dest_ref, x_ref, tg_ref, g_ref, b_ref, ys_hbm, o_ref, ybuf, sem, *, tm):
    def row_copy(r, k, slot):
        src = ys_hbm.at[pl.ds(pl.multiple_of(slot * ROW_TILE, ROW_TILE), ROW_TILE)]
        dst = ybuf.at[k, pl.ds(pl.multiple_of(r * ROW_TILE, ROW_TILE), ROW_TILE)]
        return pltpu.make_async_copy(src, dst, sem)

    def start(r, c):
        for k in range(TOP_K):
            row_copy(r, k, dest_ref[0, 0, TOP_K * r + k]).start(priority=k % 2)
        return c

    def wait(r, c):
        for k in range(TOP_K):
            row_copy(0, k, 0).wait()
        return c

    lax.fori_loop(0, tm, start, 0, unroll=8)
    lax.fori_loop(0, tm, wait, 0, unroll=8)
    f = tg_ref[:, 0:1] * _load_row_tiles(ybuf, tm, 0)
    for k in range(1, TOP_K):
        f = f + tg_ref[:, k:k + 1] * _load_row_tiles(ybuf, tm, k)
    o_ref[...] = _layernorm(DEEPNORM_ALPHA * _load_row_tiles(x_ref, tm) + f, g_ref[...], b_ref[...])


def _combine(x1, tg, dest, ys, prm, tm):
    t = x1.shape[0] // ROW_TILE
    row = lambda w: pl.BlockSpec((tm, w), lambda i: (i, 0))
    full = lambda a: pl.BlockSpec(a.shape, lambda i: (0,) * a.ndim)
    return pl.pallas_call(
        functools.partial(_combine_kernel, tm=tm),
        grid=(t // tm,),
        in_specs=[pl.BlockSpec((1, 1, TOP_K * tm), lambda i: (i, 0, 0), memory_space=pltpu.SMEM),
                  pl.BlockSpec((tm * ROW_TILE, LANE), lambda i: (i, 0)), row(LANE),
                  full(prm['ln2_g']), full(prm['ln2_b']), pl.BlockSpec(memory_space=pl.ANY)],
        out_specs=row(D_MODEL),
        out_shape=jax.ShapeDtypeStruct((t, D_MODEL), F32),
        scratch_shapes=[pltpu.VMEM((TOP_K, tm * ROW_TILE, LANE), F32), pltpu.SemaphoreType.DMA],
        compiler_params=_params(("arbitrary",)),
        name="combine_ln",
    )(dest.reshape(t // tm, 1, TOP_K * tm), x1, tg, prm['ln2_g'], prm['ln2_b'], ys)


def _dt_pieces(dt_piece):
    return jnp.concatenate([dt_piece[:, 0:8], dt_piece[:, 128:136]], axis=1)


def _layer(xp, xs, prm, lb, rel_bias, st, n_p, seq, n_s, layer, moe, xs_buf):
    tp = n_p * seq
    up = _in_proj(xp, prm['w_in'], 128)
    dt16 = _dt_pieces(up['dt'])
    dt_t = jnp.transpose(dt16.reshape(tp // CHUNK, CHUNK, A_HEADS), (0, 2, 1))
    zeros_conv = jnp.zeros((n_p, SUBLANE, A_CONV_DIM), F32)
    ya, ssm_p = _ssd(up, dt_t, zeros_conv, jnp.zeros((n_p, D_MODEL, A_STATE), F32), prm, n_p, seq, False)
    yb, hg_p = _hgrn(up, jnp.zeros((n_p, D_MODEL, LANE), F32), lb, prm['hgrn_norm_g'], n_p, seq, False)
    attn = [_attn_prompt(up['cq'], up['ck'], up['cv'], _prompt_bias(rel_bias, g), g, n_p, seq)
            for g in range(3)]
    x1p, tep, tgp, rkp, cnt_p = _merge(xp, ya, yb, attn, up['gates'], jnp.zeros((1, LANE), F32), prm, 256)
    xbc3 = up['xbc'].reshape(n_p, seq, A_CONV_DIM)
    conv_p = xbc3[:, seq - (A_CONV - 1):]
    k3 = up['ck'].reshape(n_p, seq, 768)
    v3 = up['cv'].reshape(n_p, seq, 768)

    def last_rows(a, g, w):
        return a[:, seq - min(w, seq):, 256 * g:256 * (g + 1)].reshape(n_p, min(w, seq), 4, 64)

    kv_p = [jnp.stack([last_rows(k3, g, w), last_rows(v3, g, w)], axis=2) for g, (w, _) in enumerate(C_GROUPS)]
    us = _in_proj(xs, prm['w_in'], n_s)
    dt16s = _dt_pieces(us['dt'])
    dt_ts = jnp.zeros((n_s, A_HEADS, CHUNK), F32).at[:, :, 0].set(dt16s)
    conv0 = jnp.pad(st['conv'], ((0, 0), (SUBLANE - (A_CONV - 1), 0), (0, 0)))
    yas, ssm_s = _ssd(us, dt_ts, conv0, st['ssm'].reshape(n_s, D_MODEL, A_STATE), prm, n_s, 1, True)
    ybs, hg_s = _hgrn(us, st['hgrn'].reshape(n_s, D_MODEL, LANE), lb, prm['hgrn_norm_g'], n_s, 1, True)
    bias_d = [_decode_bias(rel_bias, g) for g in range(3)]
    o_s, lse_s = _attn_decode(us['cq'], us['ck'], us['cv'], st['kv_t'], layer,
                              [b[0] for b in bias_d], jnp.stack([b[1] for b in bias_d]))
    attn_s = [(o_s[:, 256 * g:256 * (g + 1)], lse_s[:, 256 * g:256 * (g + 1)]) for g in range(3)]
    x1s, tes, tgs, rks, cnt = _merge(xs, yas, ybs, attn_s, us['gates'], cnt_p, prm, n_s)
    conv_s = jnp.concatenate([st['conv'][:, 1:], us['xbc'][:, None]], axis=1)
    ks4 = us['ck'].reshape(n_s, 1, 3, 4, 64)
    vs4 = us['cv'].reshape(n_s, 1, 3, 4, 64)
    kv_s = [jnp.stack([ks4[:, :, g], vs4[:, :, g]], axis=2) for g in range(3)]
    n_tiles = -(-(tp + n_s) * TOP_K // MOE_TILE) + N_EXPERTS
    pstart, tile_e, tile_nv = _moe_plan(cnt[0, :N_EXPERTS].astype(jnp.int32), n_tiles)
    dest_p = _slots(tep, rkp, pstart)
    dest_s = _slots(tes, rks, pstart)
    x_sorted = jnp.zeros((n_tiles * MOE_TILE * ROW_TILE, LANE), F32) if xs_buf is None else xs_buf
    x_sorted = _dispatch(x1p, dest_p, x_sorted, 256)
    x_sorted = _dispatch(x1s, dest_s, x_sorted, n_s)
    y_sorted = _moe_experts(x_sorted, tile_e, tile_nv, layer, *moe)
    yp = _combine(x1p, tgp, dest_p, y_sorted, prm, 256)
    ys = _combine(x1s, tgs, dest_s, y_sorted, prm, n_s)
    states_p = (conv_p, ssm_p.reshape(n_p, A_HEADS, A_HEAD_DIM, A_STATE),
                hg_p.reshape(n_p, B_HEADS, B_KEY_DIM, LANE), kv_p[0], kv_p[1], kv_p[2])
    states_s = (conv_s, ssm_s.reshape(n_s, A_HEADS, A_HEAD_DIM, A_STATE),
                hg_s.reshape(n_s, B_HEADS, B_KEY_DIM, LANE), kv_s[0], kv_s[1], kv_s[2])
    return yp, ys, states_p, states_s, x_sorted


def _prep_layer(l, w_in, conv_w, conv_b, dt_bias, a_log, d_skip, ssm_norm_g, hgrn_norm_g,
                w_branch_a, w_branch_b, w_branch_c, w_out, ln1_g, ln1_b, router_w, router_b,
                moe_w1, moe_b1, moe_w2, moe_b2, ln2_g, ln2_b):
    def lanes_per_group(v):
        return jnp.zeros((1, 256), F32).at[0, 0:8].set(v[:8]).at[0, 128:136].set(v[8:])

    def sublanes_per_group(v):
        return jnp.broadcast_to(v[:, None], (A_HEADS, LANE))

    return {
        'w_in': _pack_w_in(w_in[l]),
        'conv_w8': jnp.pad(conv_w[l], ((0, SUBLANE - A_CONV), (0, 0))),
        'conv_b': conv_b[l][None],
        'dt_bias_l': lanes_per_group(dt_bias[l]), 'a_log_l': lanes_per_group(a_log[l]),
        'dt_bias_t': sublanes_per_group(dt_bias[l]), 'a_log_t': sublanes_per_group(a_log[l]),
        'd_skip_l': jnp.repeat(d_skip[l], A_HEAD_DIM)[None],
        'ssm_norm_g': ssm_norm_g[l][None],
        'hgrn_norm_g': hgrn_norm_g[l][None],
        'w_branch_a': w_branch_a[l].astype(BF16), 'w_branch_b': w_branch_b[l].astype(BF16),
        'w_branch_c': w_branch_c[l].astype(BF16), 'w_out': w_out[l].astype(BF16),
        'ln1_g': ln1_g[l][None], 'ln1_b': ln1_b[l][None],
        'router_w': jnp.pad(router_w[l], ((0, 0), (0, LANE - N_EXPERTS))),
        'router_b': jnp.pad(router_b[l], (0, LANE - N_EXPERTS), constant_values=-jnp.inf)[None],
        'ln2_g': ln2_g[l][None], 'ln2_b': ln2_b[l][None],
    }


def kernel(x_prompt, x_sample, state_conv, state_ssm, state_hgrn, cache_kv_w128, cache_kv_w512, cache_kv_w2048, w_in, conv_w, conv_b, dt_bias, a_log, d_skip, ssm_norm_g, hgrn_lb, hgrn_norm_g, rel_bias, w_branch_a, w_branch_b, w_branch_c, w_out, ln1_g, ln1_b, router_w, router_b, moe_w1, moe_b1, moe_w2, moe_b2, ln2_g, ln2_b):
    n_p, seq, _ = x_prompt.shape
    n_s = x_sample.shape[0]
    depth = w_in.shape[0]
    p_lb = jax.nn.softmax(hgrn_lb.astype(F32), axis=0)
    lower_bounds = jnp.cumsum(p_lb, axis=0) - p_lb[0]
    yp = x_prompt.reshape(n_p * seq, D_MODEL)
    ys = x_sample.reshape(n_s, D_MODEL)
    st_p, st_s = [], []
    xs_buf = None
    kv_t = tuple(jnp.transpose(c, (0, 1, 3, 4, 5, 2)) for c in (cache_kv_w128, cache_kv_w512, cache_kv_w2048))
    moe = (moe_w1, moe_w2, moe_b1[:, :, None, 0::2], moe_b1[:, :, None, 1::2], moe_b2[:, :, None, :])
    for l in range(depth):
        prm = _prep_layer(l, w_in, conv_w, conv_b, dt_bias, a_log, d_skip, ssm_norm_g, hgrn_norm_g,
                          w_branch_a, w_branch_b, w_branch_c, w_out, ln1_g, ln1_b, router_w, router_b,
                          moe_w1, moe_b1, moe_w2, moe_b2, ln2_g, ln2_b)
        st = {'conv': state_conv[l], 'ssm': state_ssm[l], 'hgrn': state_hgrn[l], 'kv_t': kv_t}
        yp, ys, sp, ss, xs_buf = _layer(yp, ys, prm, lower_bounds[l][None], rel_bias, st, n_p, seq, n_s, l, moe,
                                        xs_buf)
        st_p.append(sp)
        st_s.append(ss)
    stack = lambda sts, i: jnp.stack([s[i] for s in sts], axis=0)
    return (yp.reshape(n_p, seq, D_MODEL), ys.reshape(n_s, 1, D_MODEL),
            stack(st_p, 0), stack(st_s, 0), stack(st_p, 1), stack(st_s, 1), stack(st_p, 2), stack(st_s, 2),
            stack(st_p, 3), stack(st_s, 3), stack(st_p, 4), stack(st_s, 4), stack(st_p, 5), stack(st_s, 5))
```

```python
import functools
import math

import jax
import jax.numpy as jnp
import numpy as np
from jax import lax
from jax.experimental import pallas as pl
from jax.experimental.pallas import tpu as pltpu

F32 = jnp.float32
BF16 = jnp.bfloat16
HI = lax.Precision.HIGHEST

D_MODEL = 1024
A_HEADS = 16
A_HEAD_DIM = 64
A_STATE = 128
A_CONV = 4
A_CONV_DIM = 1536
B_HEADS = 8
B_KEY_DIM = 128
C_GROUPS = ((128, 1), (512, 4), (2048, 16))
C_SPAN = 128
C_GROUP_WIDTH = 256
REL_BUCKETS = 32
REL_MAX_DIST = 2048
N_EXPERTS = 32
TOP_K = 4
SWIGLU_ALPHA = 1.702
SWIGLU_LIMIT = 7.0
DEEPNORM_ALPHA = (2.0 * 2) ** 0.25
LN_EPS = 1e-5
RMS_EPS = 1e-5
NEG_BIG = -1e30
LOG_FLOOR = 1e-30

LANE = 128
SUBLANE = 8
CHUNK = 128
SUB = 8
HGRN_HEADS_PER_STEP = 4
ATTN_ILP = 4
VMEM_LIMIT = 56 * 1024 * 1024

IN_PIECES = (('z', 1024), ('xbc', 1536), ('bq', 1024), ('bf', 1024), ('bi', 1024), ('bg', 1024),
             ('cq', 768), ('ck', 768), ('cv', 768), ('gates', 3072), ('dt', 256))
IN_PACKED = sum(w for _, w in IN_PIECES)


def _params(sem):
    return pltpu.CompilerParams(dimension_semantics=sem, vmem_limit_bytes=VMEM_LIMIT)


def _sigmoid(x):
    return 1.0 / (1.0 + jnp.exp(-x))


def _softplus(x):
    return jnp.maximum(x, 0.0) + jnp.log(1.0 + jnp.exp(-jnp.abs(x)))


def _dot(a, b):
    return jnp.dot(a.astype(BF16), b.astype(BF16), preferred_element_type=F32)


def _dot_nt(a, b):
    return lax.dot_general(a.astype(BF16), b.astype(BF16), (((1,), (1,)), ((), ())),
                           preferred_element_type=F32)


def _dot_hi(a, b):
    return jnp.dot(a, b, preferred_element_type=F32, precision=HI)


ROW_TILE = D_MODEL // LANE


def _store_row_tiles(ref, val):
    n = val.shape[0]
    for c in range(ROW_TILE):
        ref[pl.ds(c, n, stride=ROW_TILE), :] = val[:, LANE * c:LANE * (c + 1)]


def _load_row_tiles(ref, n, lead=None):
    rows = pl.ds
    parts = []
    for c in range(ROW_TILE):
        idx = (rows(c, n, stride=ROW_TILE), slice(None))
        parts.append(ref[idx] if lead is None else ref[(lead,) + idx])
    return jnp.concatenate(parts, axis=1)


def _layernorm(h, g, b):
    mu = jnp.mean(h, -1, keepdims=True)
    c = h - mu
    var = jnp.mean(c * c, -1, keepdims=True)
    return c * lax.rsqrt(var + LN_EPS) * g + b


def _in_proj_kernel(x_ref, w_ref, *o_refs):
    xb = x_ref[...].astype(BF16)
    off = 0
    for o_ref in o_refs:
        wd = o_ref.shape[1]
        o_ref[...] = jnp.dot(xb, w_ref[:, off:off + wd], preferred_element_type=F32)
        off += wd


def _in_proj(x, w_packed, tm):
    t = x.shape[0]
    outs = pl.pallas_call(
        _in_proj_kernel,
        grid=(t // tm,),
        in_specs=[pl.BlockSpec((tm, D_MODEL), lambda i: (i, 0)),
                  pl.BlockSpec(memory_space=pltpu.VMEM)],
        out_specs=[pl.BlockSpec((tm, w), lambda i: (i, 0)) for _, w in IN_PIECES],
        out_shape=[jax.ShapeDtypeStruct((t, w), F32) for _, w in IN_PIECES],
        compiler_params=_params(("arbitrary",)),
        name="in_proj",
    )(x, w_packed)
    return {name: o for (name, _), o in zip(IN_PIECES, outs)}


def _pack_w_in(w):
    dt = w[:, 2560:2576]
    dtp = jnp.zeros((D_MODEL, 256), F32).at[:, 0:8].set(dt[:, :8]).at[:, 128:136].set(dt[:, 8:])
    return jnp.concatenate([w[:, :2560], w[:, 2576:], dtp], axis=1).astype(BF16)


def _stage(ref, ci, decode, r):
    if decode:
        row = ref[pl.ds(r, 1), :]
        rows = lax.broadcasted_iota(jnp.int32, (CHUNK, row.shape[1]), 0)
        return jnp.where(rows == 0, jnp.broadcast_to(row, (CHUNK, row.shape[1])), 0.0)
    return ref[pl.ds(pl.multiple_of(ci * CHUNK, CHUNK), CHUNK), :]


def _unstage(ref, val, ci, decode, r):
    if decode:
        ref[pl.ds(r, 1), :] = val[0:1, :]
    else:
        ref[pl.ds(pl.multiple_of(ci * CHUNK, CHUNK), CHUNK), :] = val


def _ssd_kernel(xs_ref, bm_ref, cm_ref, z_ref, dt_ref, dtt_ref,
                c0x_ref, c0b_ref, c0c_ref, h0_ref,
                wx_ref, wb_ref, wc_ref, bx_ref, bb_ref, bc_ref,
                dtb_ref, alog_ref, dtbt_ref, alogt_ref, dsk_ref, ng_ref,
                y_ref, hn_ref,
                h_scr, cx_scr, cb_scr, cc_scr, px_scr, pb_scr, pc_scr, *, nchunks, decode):
    b = pl.program_id(1)
    t = pl.program_id(2)
    r = b % SUBLANE
    q = CHUNK

    @pl.when(t == 0)
    def _():
        h_scr[...] = h0_ref[0]
        cx_scr[...] = c0x_ref[0]
        cb_scr[...] = c0b_ref[0]
        cc_scr[...] = c0c_ref[0]

    rows = lax.broadcasted_iota(jnp.int32, (q, q), 0)
    cols = lax.broadcasted_iota(jnp.int32, (q, q), 1)
    tril = rows >= cols
    tril_f = tril.astype(F32)
    triu_f = (rows <= cols).astype(F32)
    lane = lax.broadcasted_iota(jnp.int32, (q, LANE), 1)
    lo = lane < A_HEAD_DIM
    row_lo = lax.broadcasted_iota(jnp.int32, (LANE, LANE), 0) < A_HEAD_DIM
    valid_col = lax.broadcasted_iota(jnp.int32, (q, LANE), 0) == 0
    valid_row = lax.broadcasted_iota(jnp.int32, (SUBLANE, q), 1) == 0

    def conv(x, carry, pad, w_ref, b_ref):
        pad[0:SUBLANE, :] = carry[...]
        pad[SUBLANE:SUBLANE + q, :] = x
        if not decode:
            carry[...] = pad[q:q + SUBLANE, :]
        acc = b_ref[...]
        for j in range(A_CONV):
            acc = acc + pad[5 + j:5 + j + q, :] * w_ref[j:j + 1, :]
        return acc * _sigmoid(acc)

    def chunk(ci, carry_unused):
        xs = conv(_stage(xs_ref, ci, decode, r), cx_scr, px_scr, wx_ref, bx_ref)
        bm = conv(_stage(bm_ref, ci, decode, r), cb_scr, pb_scr, wb_ref, bb_ref)
        cm = conv(_stage(cm_ref, ci, decode, r), cc_scr, pc_scr, wc_ref, bc_ref)
        a_lane = -jnp.exp(alog_ref[...])
        a_sub = -jnp.exp(alogt_ref[...])
        dt = _softplus(_stage(dt_ref, ci, decode, r) + dtb_ref[...])
        dtt = _softplus(dtt_ref[ci] + dtbt_ref[...])
        if decode:
            dt = jnp.where(valid_col, dt, 0.0)
            dtt = jnp.where(valid_row, dtt, 0.0)
        if decode:
            acum = jnp.broadcast_to((dt * a_lane)[0:1, :], (q, LANE))
            acum_t = jnp.broadcast_to((dtt * a_sub)[:, 0:1], (SUBLANE, q))
        else:
            acum = _dot_hi(tril_f, dt * a_lane)
            acum_t = _dot_hi(dtt * a_sub, triu_f)
        a_last = acum[q - 1:q, :]
        cb = _dot_nt(cm, bm)
        cm_b = cm.astype(BF16)
        bm_b = bm.astype(BF16)
        ys = []
        for j in range(4):
            xp = xs[:, LANE * j:LANE * (j + 1)]
            xp_b = xp.astype(BF16)
            yd = []
            for half in range(2):
                hl = 2 * j + half
                diff = acum[:, hl:hl + 1] - acum_t[hl:hl + 1, :]
                lm = jnp.exp(jnp.where(tril, diff, NEG_BIG))
                m = cb * lm * dtt[hl:hl + 1, :]
                yd.append(jnp.dot(m.astype(BF16), xp_b, preferred_element_type=F32))
            y = jnp.where(lo, yd[0], yd[1])
            hp = h_scr[LANE * j:LANE * (j + 1), :]
            yo = lax.dot_general(cm_b, hp.astype(BF16), (((1,), (1,)), ((), ())),
                                 preferred_element_type=F32)
            e0 = acum[:, 2 * j:2 * j + 1]
            e1 = acum[:, 2 * j + 1:2 * j + 2]
            y = y + yo * jnp.exp(jnp.where(lo, e0, e1))
            w0 = dt[:, 2 * j:2 * j + 1] * jnp.exp(a_last[:, 2 * j:2 * j + 1] - e0)
            w1 = dt[:, 2 * j + 1:2 * j + 2] * jnp.exp(a_last[:, 2 * j + 1:2 * j + 2] - e1)
            xw = xp * jnp.where(lo, w0, w1)
            upd = jnp.dot(xw.T.astype(BF16), bm_b, preferred_element_type=F32)
            dec = jnp.exp(jnp.where(row_lo, a_last[:, 2 * j:2 * j + 1], a_last[:, 2 * j + 1:2 * j + 2]))
            h_scr[LANE * j:LANE * (j + 1), :] = hp * dec + upd
            ys.append(y)
        y = jnp.concatenate(ys, axis=1) + dsk_ref[...] * xs
        z = _stage(z_ref, ci, decode, r)
        y = y * (z * _sigmoid(z))
        y = y * lax.rsqrt(jnp.mean(y * y, -1, keepdims=True) + RMS_EPS) * ng_ref[...]
        _unstage(y_ref, y, ci, decode, r)
        return carry_unused

    lax.fori_loop(0, nchunks, chunk, 0)

    @pl.when(t == pl.num_programs(2) - 1)
    def _():
        hn_ref[0] = h_scr[...]


def _ssd(u, dt_t, conv0_8, h0, prm, n, seq, decode):
    tb = CHUNK if decode else min(512, seq)
    nb = 1 if decode else seq // tb
    nchunks = tb // CHUNK
    rb = SUBLANE if decode else tb

    def rowmap(lane_block):
        if decode:
            return lambda g, b, t: (b // SUBLANE, lane_block(g))
        return lambda g, b, t: (b * nb + t, lane_block(g))

    chunk_map = (lambda g, b, t: (b, g, 0)) if decode else (lambda g, b, t: (b * nb + t, g, 0))
    cw = (512, LANE, LANE)
    lane_blocks = (lambda g: g, lambda g: 8 + g, lambda g: 10 + g)
    in_specs = [
        pl.BlockSpec((rb, 512), rowmap(lane_blocks[0])),
        pl.BlockSpec((rb, LANE), rowmap(lane_blocks[1])),
        pl.BlockSpec((rb, LANE), rowmap(lane_blocks[2])),
        pl.BlockSpec((rb, 512), rowmap(lambda g: g)),
        pl.BlockSpec((rb, LANE), rowmap(lambda g: g)),
        pl.BlockSpec((nchunks, SUBLANE, CHUNK), chunk_map),
    ]
    in_specs += [pl.BlockSpec((1, SUBLANE, w), (lambda lb: (lambda g, b, t: (b, 0, lb(g))))(lb))
                 for w, lb in zip(cw, lane_blocks)]
    in_specs += [pl.BlockSpec((1, 512, A_STATE), lambda g, b, t: (b, g, 0))]
    in_specs += [pl.BlockSpec((SUBLANE, w), (lambda lb: (lambda g, b, t: (0, lb(g))))(lb))
                 for w, lb in zip(cw, lane_blocks)]
    in_specs += [pl.BlockSpec((1, w), (lambda lb: (lambda g, b, t: (0, lb(g))))(lb))
                 for w, lb in zip(cw, lane_blocks)]
    in_specs += [
        pl.BlockSpec((1, LANE), lambda g, b, t: (0, g)),
        pl.BlockSpec((1, LANE), lambda g, b, t: (0, g)),
        pl.BlockSpec((SUBLANE, LANE), lambda g, b, t: (g, 0)),
        pl.BlockSpec((SUBLANE, LANE), lambda g, b, t: (g, 0)),
        pl.BlockSpec((1, 512), lambda g, b, t: (0, g)),
        pl.BlockSpec((1, 512), lambda g, b, t: (0, g)),
    ]
    rows_total = u['z'].shape[0]
    y, hn = pl.pallas_call(
        functools.partial(_ssd_kernel, nchunks=nchunks, decode=decode),
        grid=(2, n, nb),
        in_specs=in_specs,
        out_specs=[pl.BlockSpec((rb, 512), rowmap(lambda g: g)),
                   pl.BlockSpec((1, 512, A_STATE), lambda g, b, t: (b, g, 0))],
        out_shape=[jax.ShapeDtypeStruct((rows_total, D_MODEL), F32),
                   jax.ShapeDtypeStruct((n, D_MODEL, A_STATE), F32)],
        scratch_shapes=[pltpu.VMEM((512, A_STATE), F32),
                        pltpu.VMEM((SUBLANE, 512), F32), pltpu.VMEM((SUBLANE, LANE), F32),
                        pltpu.VMEM((SUBLANE, LANE), F32),
                        pltpu.VMEM((CHUNK + SUBLANE, 512), F32), pltpu.VMEM((CHUNK + SUBLANE, LANE), F32),
                        pltpu.VMEM((CHUNK + SUBLANE, LANE), F32)],
        compiler_params=_params(("arbitrary", "arbitrary", "arbitrary")),
        name="ssd_decode" if decode else "ssd_prompt",
    )(u['xbc'], u['xbc'], u['xbc'], u['z'], u['dt'], dt_t,
      conv0_8, conv0_8, conv0_8, h0,
      prm['conv_w8'], prm['conv_w8'], prm['conv_w8'], prm['conv_b'], prm['conv_b'], prm['conv_b'],
      prm['dt_bias_l'], prm['a_log_l'], prm['dt_bias_t'], prm['a_log_t'], prm['d_skip_l'], prm['ssm_norm_g'])
    return y, hn


def _hgrn_kernel(q_ref, f_ref, i_ref, g_ref, lb_ref, ng_ref, s0_ref, y_ref, sn_ref, st_scr,
                 *, nchunks, decode, nheads):
    b = pl.program_id(1)
    t = pl.program_id(2)
    r = b % SUBLANE
    c = CHUNK

    @pl.when(t == 0)
    def _():
        for hh in range(nheads):
            st_scr[hh] = s0_ref[0, LANE * hh:LANE * (hh + 1), :].T

    rows = lax.broadcasted_iota(jnp.int32, (c, c), 0)
    cols = lax.broadcasted_iota(jnp.int32, (c, c), 1)
    tril_f = (rows >= cols).astype(F32)
    row_c = lax.broadcasted_iota(jnp.int32, (c, LANE), 0)
    row_s = lax.broadcasted_iota(jnp.int32, (SUB, LANE), 0)
    blk_xor = (rows // SUB) ^ (cols // SUB)
    level = jnp.where(blk_xor >= 8, 3, jnp.where(blk_xor >= 4, 2, jnp.where(blk_xor >= 2, 1, 0)))
    level = jnp.where((rows // SUB) > (cols // SUB), level, -1)
    nlevels = 4
    assert SUB << nlevels == c

    def one_head(hh, ci):
        ls = slice(LANE * hh, LANE * (hh + 1))
        lbv = lb_ref[:, ls]
        f = _stage(f_ref, ci, decode, r)[:, ls]
        qr = _stage(q_ref, ci, decode, r)[:, ls]
        v = _stage(i_ref, ci, decode, r)[:, ls]
        gr = _stage(g_ref, ci, decode, r)[:, ls]
        q = qr * _sigmoid(qr) * (B_KEY_DIM ** -0.5)
        e_f = jnp.exp(-jnp.abs(f))
        r_f = 1.0 / (1.0 + e_f)
        pos = f >= 0.0
        sig_p = jnp.where(pos, r_f, e_f * r_f)
        sig_n = jnp.where(pos, e_f * r_f, r_f)
        logf = jnp.log(jnp.maximum(lbv + (1.0 - lbv) * sig_p, LOG_FLOOR))
        k = (1.0 - lbv) * sig_n
        if decode:
            logf = jnp.where(row_c == 0, logf, 0.0)
            k = jnp.where(row_c == 0, k, 0.0)
        if decode:
            bcum = jnp.broadcast_to(logf[0:1, :], (c, LANE))
        else:
            bcum = _dot_hi(tril_f, logf)
        b_last = bcum[c - 1:c, :]
        st = st_scr[hh]
        o = _dot_nt(q * jnp.exp(bcum), st)
        if not decode:
            att = jnp.zeros((c, c), F32)
            for lv in range(nlevels):
                h = SUB << lv
                ref = jnp.concatenate(
                    [jnp.broadcast_to(bcum[2 * h * m + h - 1:2 * h * m + h, :], (2 * h, LANE))
                     for m in range(c // (2 * h))], axis=0)
                x = jnp.exp(-jnp.abs(bcum - ref))
                att = jnp.where(level == lv, _dot_nt(q * x, k * x), att)
            o = o + _dot(att, v)
        diag = []
        for i in range(c // SUB):
            if decode and i > 0:
                diag.append(jnp.zeros((SUB, LANE), F32))
                continue
            sl = slice(SUB * i, SUB * (i + 1))
            qb, kb, vb, bb = q[sl, :], k[sl, :], v[sl, :], bcum[sl, :]
            od = jnp.zeros((SUB, LANE), F32)
            for s in range(1 if decode else SUB):
                e = jnp.exp(jnp.where(row_s >= s, bb - bb[s:s + 1, :], NEG_BIG))
                rs = jnp.sum(qb * kb[s:s + 1, :] * e, -1, keepdims=True)
                od = od + rs * vb[s:s + 1, :]
            diag.append(od)
        o = o + jnp.concatenate(diag, axis=0)
        k2 = k * jnp.exp(b_last - bcum)
        st_scr[hh] = jnp.exp(b_last) * st + jnp.dot(v.T.astype(BF16), k2.astype(BF16),
                                                     preferred_element_type=F32)
        gate = gr * _sigmoid(gr)
        return o * lax.rsqrt(jnp.mean(o * o, -1, keepdims=True) + RMS_EPS) * ng_ref[...] * gate

    def chunk(ci, carry_unused):
        y = [one_head(hh, ci) for hh in range(nheads)]
        _unstage(y_ref, y[0] if nheads == 1 else jnp.concatenate(y, axis=1), ci, decode, r)
        return carry_unused

    lax.fori_loop(0, nchunks, chunk, 0)

    @pl.when(t == pl.num_programs(2) - 1)
    def _():
        for hh in range(nheads):
            sn_ref[0, LANE * hh:LANE * (hh + 1), :] = st_scr[hh].T


def _hgrn(u, s0, lb, ng, n, seq, decode):
    tb = CHUNK if decode else min(512, seq)
    nb = 1 if decode else seq // tb
    nchunks = tb // CHUNK
    rb = SUBLANE if decode else tb
    rowmap = (lambda h, b, t: (b // SUBLANE, h)) if decode else (lambda h, b, t: (b * nb + t, h))
    rows_total = u['bq'].shape[0]
    nheads = B_HEADS if decode else HGRN_HEADS_PER_STEP
    wd = LANE * nheads
    y, sn = pl.pallas_call(
        functools.partial(_hgrn_kernel, nchunks=nchunks, decode=decode, nheads=nheads),
        grid=(B_HEADS // nheads, n, nb),
        in_specs=[pl.BlockSpec((rb, wd), rowmap)] * 4 + [
            pl.BlockSpec((1, wd), lambda h, b, t: (0, h)),
            pl.BlockSpec((1, LANE), lambda h, b, t: (0, 0)),
            pl.BlockSpec((1, B_KEY_DIM * nheads, LANE), lambda h, b, t: (b, h, 0))],
        out_specs=[pl.BlockSpec((rb, wd), rowmap),
                   pl.BlockSpec((1, B_KEY_DIM * nheads, LANE), lambda h, b, t: (b, h, 0))],
        out_shape=[jax.ShapeDtypeStruct((rows_total, D_MODEL), F32),
                   jax.ShapeDtypeStruct((n, D_MODEL, LANE), F32)],
        scratch_shapes=[pltpu.VMEM((nheads, LANE, B_KEY_DIM), F32)],
        compiler_params=_params(("arbitrary", "arbitrary", "arbitrary")),
        name="hgrn_decode" if decode else "hgrn_prompt",
    )(u['bq'], u['bf'], u['bi'], u['bg'], lb, ng, s0)
    return y, sn


def _t5_bucket_np(dist):
    exact = REL_BUCKETS // 2
    d = np.maximum(dist, 1).astype(np.float32)
    large = exact + (np.log(d / np.float32(exact)) / np.float32(math.log(REL_MAX_DIST / exact))
                     * np.float32(REL_BUCKETS - exact)).astype(np.int32)
    large = np.clip(large, 0, REL_BUCKETS - 1)
    return np.where(dist < exact, dist, large)


def _attn_prompt_kernel(q_ref, kc_ref, kp_ref, vc_ref, vp_ref, bias_ref, o_ref, lse_ref, *, dil):
    first = pl.program_id(1) == 0
    s_q = C_SPAN
    lane = lax.broadcasted_iota(jnp.int32, (s_q, LANE), 1)
    lo = lane < 64
    kcol = lax.broadcasted_iota(jnp.int32, (s_q, 2 * s_q), 1)
    no_prev = jnp.logical_and(first, kcol < s_q)

    pair = pl.program_id(2)

    def solve(qp, kprev, kcur, vprev, vcur, mask_prev):
        kp2 = jnp.concatenate([kprev, kcur], axis=0).astype(BF16)
        vp2 = jnp.concatenate([vprev, vcur], axis=0).astype(BF16)
        oh, lh = [], []
        for half in range(2):
            qm = jnp.where(lo if half == 0 else jnp.logical_not(lo), qp, 0.0).astype(BF16)
            s = lax.dot_general(qm, kp2, (((1,), (1,)), ((), ())), preferred_element_type=F32)
            s = s * (64 ** -0.5) + bias_ref[2 * pair + half]
            if mask_prev:
                s = jnp.where(no_prev, NEG_BIG, s)
            m = jnp.max(s, -1, keepdims=True)
            p = jnp.exp(s - m)
            den = jnp.sum(p, -1, keepdims=True)
            oh.append(jnp.dot((p / den).astype(BF16), vp2, preferred_element_type=F32))
            lh.append(m + jnp.log(den))
        return jnp.where(lo, oh[0], oh[1]), jnp.where(lo, lh[0], lh[1])

    if dil == 1:
        for sb in range(ATTN_ILP):
            cur = slice(s_q * sb, s_q * (sb + 1))
            if sb == 0:
                o, lse = solve(q_ref[cur, :], kp_ref[...], kc_ref[cur, :], vp_ref[...], vc_ref[cur, :], True)
            else:
                prv = slice(s_q * (sb - 1), s_q * sb)
                o, lse = solve(q_ref[cur, :], kc_ref[prv, :], kc_ref[cur, :], vc_ref[prv, :], vc_ref[cur, :],
                               False)
            o_ref[cur, :] = o
            lse_ref[cur, :] = lse
    else:
        def residues(it, carry):
            for jj in range(ATTN_ILP):
                sl = pl.ds(it * ATTN_ILP + jj, s_q, stride=dil)
                o, lse = solve(q_ref[sl, :], kp_ref[sl, :], kc_ref[sl, :], vp_ref[sl, :], vc_ref[sl, :], True)
                o_ref[sl, :] = o
                lse_ref[sl, :] = lse
            return carry

        lax.fori_loop(0, dil // ATTN_ILP, residues, 0)


def _attn_prompt(cq, ck, cv, bias, g, n, seq):
    dil = C_GROUPS[g][1]
    look = C_SPAN * dil
    rows_blk = look * (ATTN_ILP if dil == 1 else 1)
    nbk = seq // rows_blk
    per = rows_blk // look
    cur = lambda b, i, j: (b * nbk + i, 2 * g + j)
    prev = lambda b, i, j: (jnp.maximum((b * nbk + i) * per - 1, 0), 2 * g + j)
    blk = (rows_blk, LANE)
    pblk = (look, LANE)
    return pl.pallas_call(
        functools.partial(_attn_prompt_kernel, dil=dil),
        grid=(n, nbk, 2),
        in_specs=[pl.BlockSpec(blk, cur), pl.BlockSpec(blk, cur), pl.BlockSpec(pblk, prev),
                  pl.BlockSpec(blk, cur), pl.BlockSpec(pblk, prev),
                  pl.BlockSpec((4, C_SPAN, 2 * C_SPAN), lambda b, i, j: (0, 0, 0))],
        out_specs=[pl.BlockSpec(blk, lambda b, i, j: (b * nbk + i, j))] * 2,
        out_shape=[jax.ShapeDtypeStruct((n * seq, C_GROUP_WIDTH), F32)] * 2,
        compiler_params=_params(("arbitrary", "arbitrary", "arbitrary")),
        name=f"attn_prompt_g{g}",
    )(cq, ck, ck, cv, cv, bias)


def _prompt_bias(rel_bias, g):
    dil = C_GROUPS[g][1]
    qi = np.arange(C_SPAN)[:, None]
    kj = np.arange(2 * C_SPAN)[None, :]
    rel = qi + C_SPAN - kj
    band = (rel >= 0) & (rel <= C_SPAN)
    idx = _t5_bucket_np(np.maximum(rel, 0) * dil)
    tab = rel_bias[:, 4 * g:4 * g + 4]
    hit = jnp.asarray(idx)[None, :, :, None] == jnp.arange(REL_BUCKETS)
    vals = jnp.sum(jnp.where(hit, tab.T[:, None, None, :], 0.0), axis=-1)
    return jnp.where(band[None], vals, NEG_BIG)


def _decode_bias(rel_bias, g):
    win, dil = C_GROUPS[g]
    tab = rel_bias[:, 4 * g:4 * g + 4]
    pos = np.arange(win)
    vals = jnp.where((pos % dil == 0)[:, None], tab[_t5_bucket_np(win - pos)], NEG_BIG)
    bias_buf = jnp.zeros((2, SUBLANE, win), F32).at[:, 0:2].set(vals.T.reshape(2, 2, win))
    new = jnp.broadcast_to(tab[0].reshape(2, 2, 1), (2, 2, LANE))
    bias_new = jnp.zeros((2, SUBLANE, LANE), F32).at[:, 0:2].set(new)
    return bias_buf, bias_new


def _attn_decode_kernel(q_ref, k_ref, v_ref, c0_ref, c1_ref, c2_ref, bb0_ref, bb1_ref, bb2_ref, bn_ref,
                        o_ref, lse_ref):
    b = pl.program_id(0)
    row8 = lax.broadcasted_iota(jnp.int32, (SUBLANE, LANE), 0)
    lane8 = lax.broadcasted_iota(jnp.int32, (SUBLANE, LANE), 1)
    qmask = jnp.logical_or(jnp.logical_and(row8 == 0, lane8 < 64), jnp.logical_and(row8 == 1, lane8 >= 64))
    lo1 = lax.broadcasted_iota(jnp.int32, (1, LANE), 1) < 64
    q_all = q_ref[pl.ds(b, 1), :]
    k_all = k_ref[pl.ds(b, 1), :]
    v_all = v_ref[pl.ds(b, 1), :]
    o_parts, lse_parts = [], []
    for g, (buf_ref, bb_ref) in enumerate(((c0_ref, bb0_ref), (c1_ref, bb1_ref), (c2_ref, bb2_ref))):
        win = C_GROUPS[g][0]
        for j in range(2):
            c0 = C_GROUP_WIDTH * g + LANE * j
            qrow = q_all[:, c0:c0 + LANE]
            knew = k_all[:, c0:c0 + LANE]
            vnew = v_all[:, c0:c0 + LANE]
            q8 = jnp.where(qmask, jnp.broadcast_to(qrow, (SUBLANE, LANE)), 0.0)
            kt = buf_ref[0, 0, 0, 2 * j:2 * j + 2].reshape(LANE, win)
            vt = buf_ref[0, 0, 1, 2 * j:2 * j + 2].reshape(LANE, win)
            s = _dot(q8, kt) * (64 ** -0.5) + bb_ref[j]
            snew = jnp.sum(q8 * knew, -1, keepdims=True) * (64 ** -0.5) + bn_ref[g, j][:, 0:1]
            m = jnp.maximum(jnp.max(s, -1, keepdims=True), snew)
            p = jnp.exp(s - m)
            pn = jnp.exp(snew - m)
            den = jnp.sum(p, -1, keepdims=True) + pn
            o8 = _dot_nt(p / den, vt) + (pn / den) * vnew
            lse8 = jnp.broadcast_to(m + jnp.log(den), (SUBLANE, LANE))
            o_parts.append(jnp.where(lo1, o8[0:1, :], o8[1:2, :]))
            lse_parts.append(jnp.where(lo1, lse8[0:1, :], lse8[1:2, :]))
    o_ref[pl.ds(b, 1), :] = jnp.concatenate(o_parts, axis=1)
    lse_ref[pl.ds(b, 1), :] = jnp.concatenate(lse_parts, axis=1)


def _attn_decode(cq, ck, cv, caches_t, layer, bias_bufs, bias_new):
    n = cq.shape[0]
    full = lambda a: pl.BlockSpec(a.shape, lambda b: (0,) * a.ndim)
    cache_spec = lambda c: pl.BlockSpec((1, 1) + c.shape[2:], lambda b: (layer, b, 0, 0, 0, 0))
    return pl.pallas_call(
        _attn_decode_kernel,
        grid=(n,),
        in_specs=[full(cq), full(ck), full(cv)] + [cache_spec(c) for c in caches_t]
        + [full(bb) for bb in bias_bufs] + [full(bias_new)],
        out_specs=[pl.BlockSpec((n, 768), lambda b: (0, 0))] * 2,
        out_shape=[jax.ShapeDtypeStruct((n, 768), F32)] * 2,
        compiler_params=_params(("arbitrary",)),
        name="attn_decode",
    )(cq, ck, cv, *caches_t, *bias_bufs, bias_new)


def _merge_kernel(x_ref, ya_ref, yb_ref, o0_ref, o1_ref, o2_ref, l0_ref, l1_ref, l2_ref, gt_ref,
                  wa_ref, wb_ref, wc_ref, wo_ref, g_ref, b_ref, rw_ref, rb_ref, cnt0_ref,
                  x1_ref, te_ref, tg_ref, rk_ref, cnt_ref, cnt_scr):
    l0, l1, l2 = l0_ref[...], l1_ref[...], l2_ref[...]
    lm = jnp.maximum(jnp.maximum(l0, l1), l2)
    e0, e1, e2 = jnp.exp(l0 - lm), jnp.exp(l1 - lm), jnp.exp(l2 - lm)
    den = e0 + e1 + e2
    yc = (e0 / den) * o0_ref[...] + (e1 / den) * o1_ref[...] + (e2 / den) * o2_ref[...]
    ga = _sigmoid(gt_ref[:, 0:D_MODEL])
    gb = _sigmoid(gt_ref[:, D_MODEL:2 * D_MODEL])
    gc = _sigmoid(gt_ref[:, 2 * D_MODEL:3 * D_MODEL])
    merged = (ga * _dot(ya_ref[...], wa_ref[...]) + gb * _dot(yb_ref[...], wb_ref[...])
              + gc * _dot(yc, wc_ref[...]))
    h = DEEPNORM_ALPHA * x_ref[...] + _dot(merged, wo_ref[...])
    x1 = _layernorm(h, g_ref[...], b_ref[...])
    _store_row_tiles(x1_ref, x1)
    logits = _dot(x1, rw_ref[...]) + rb_ref[...]
    lane = lax.broadcasted_iota(jnp.int32, logits.shape, 1)
    lane_f = lane.astype(F32)
    te = jnp.zeros(logits.shape, F32)
    vals, onehots = [], []
    for k in range(TOP_K):
        m = jnp.max(logits, -1, keepdims=True)
        idx = jnp.min(jnp.where(logits == m, lane_f, float(LANE)), -1, keepdims=True)
        te = jnp.where(lane == k, idx, te)
        vals.append(m)
        hit = lane_f == idx
        onehots.append(hit.astype(F32))
        logits = jnp.where(hit, -jnp.inf, logits)
    ex = [jnp.exp(v - vals[0]) for v in vals]
    tot = ex[0] + ex[1] + ex[2] + ex[3]
    tg = jnp.zeros(logits.shape, F32)
    for k in range(TOP_K):
        tg = jnp.where(lane == k, ex[k] / tot, tg)
    te_ref[...] = te.astype(jnp.int32)
    tg_ref[...] = tg
    @pl.when(pl.program_id(0) == 0)
    def _():
        cnt_scr[...] = cnt0_ref[...]

    tm = logits.shape[0]
    oh = onehots[0] + onehots[1] + onehots[2] + onehots[3]
    earlier = (lax.broadcasted_iota(jnp.int32, (tm, tm), 0)
               > lax.broadcasted_iota(jnp.int32, (tm, tm), 1)).astype(BF16)
    before = jnp.dot(earlier, oh.astype(BF16), preferred_element_type=F32) + cnt_scr[...]
    rank = jnp.zeros(logits.shape, F32)
    for k in range(TOP_K):
        rank = jnp.where(lane == k, jnp.sum(onehots[k] * before, -1, keepdims=True), rank)
    rk_ref[...] = rank.astype(jnp.int32)
    cnt_scr[...] = cnt_scr[...] + jnp.sum(oh, 0, keepdims=True)
    cnt_ref[...] = cnt_scr[...]


def _merge(x, ya, yb, attn, gates, cnt0, prm, tm):
    t = x.shape[0]
    row = lambda w: pl.BlockSpec((tm, w), lambda i: (i, 0))
    full = lambda a: pl.BlockSpec(a.shape, lambda i: (0,) * a.ndim)
    ws = [prm['w_branch_a'], prm['w_branch_b'], prm['w_branch_c'], prm['w_out'],
          prm['ln1_g'], prm['ln1_b'], prm['router_w'], prm['router_b'], cnt0]
    (o0, l0), (o1, l1), (o2, l2) = attn
    return pl.pallas_call(
        _merge_kernel,
        grid=(t // tm,),
        in_specs=[row(D_MODEL)] * 3 + [row(C_GROUP_WIDTH)] * 6 + [row(3 * D_MODEL)] + [full(a) for a in ws],
        out_specs=[pl.BlockSpec((tm * ROW_TILE, LANE), lambda i: (i, 0)), row(LANE), row(LANE), row(LANE),
                   pl.BlockSpec((1, LANE), lambda i: (0, 0))],
        out_shape=[jax.ShapeDtypeStruct((t * ROW_TILE, LANE), F32), jax.ShapeDtypeStruct((t, LANE), jnp.int32),
                   jax.ShapeDtypeStruct((t, LANE), F32), jax.ShapeDtypeStruct((t, LANE), jnp.int32),
                   jax.ShapeDtypeStruct((1, LANE), F32)],
        scratch_shapes=[pltpu.VMEM((1, LANE), F32)],
        compiler_params=_params(("arbitrary",)),
        name="merge_ln_router",
    )(x, ya, yb, o0, o1, o2, l0, l1, l2, gates, *ws)


MOE_TILE = 512


def _dispatch_kernel(dest_ref, x_ref, xs_in_hbm, xs_hbm, sem, *, tm):
    del xs_in_hbm

    def row_copy(r, slot):
        src = x_ref.at[pl.ds(pl.multiple_of(r * ROW_TILE, ROW_TILE), ROW_TILE)]
        dst = xs_hbm.at[pl.ds(pl.multiple_of(slot * ROW_TILE, ROW_TILE), ROW_TILE)]
        return pltpu.make_async_copy(src, dst, sem)

    def start(r, c):
        for k in range(TOP_K):
            row_copy(r, dest_ref[0, 0, TOP_K * r + k]).start(priority=k % 2)
        return c

    def wait(r, c):
        for _ in range(TOP_K):
            row_copy(0, 0).wait()
        return c

    lax.fori_loop(0, tm, start, 0, unroll=8)
    lax.fori_loop(0, tm, wait, 0, unroll=8)


def _dispatch(x1, dest, xs, tm):
    t = x1.shape[0] // ROW_TILE
    return pl.pallas_call(
        functools.partial(_dispatch_kernel, tm=tm),
        grid=(t // tm,),
        in_specs=[pl.BlockSpec((1, 1, TOP_K * tm), lambda i: (i, 0, 0), memory_space=pltpu.SMEM),
                  pl.BlockSpec((tm * ROW_TILE, LANE), lambda i: (i, 0)),
                  pl.BlockSpec(memory_space=pl.ANY)],
        out_specs=pl.BlockSpec(memory_space=pl.ANY),
        out_shape=jax.ShapeDtypeStruct(xs.shape, F32),
        scratch_shapes=[pltpu.SemaphoreType.DMA],
        input_output_aliases={2: 0},
        compiler_params=_params(("arbitrary",)),
        name="moe_dispatch",
    )(dest.reshape(t // tm, 1, TOP_K * tm), x1, xs)


def _expert_kernel(te_ref, nv_ref, nx_ref, sl_ref, x_ref, w1_hbm, w2_hbm, b1g_ref, b1l_ref, b2_ref, y_ref,
                   w1_buf, w2_buf, w1p_scr, w2b_scr, sem, *, layer):
    i = pl.program_id(0)
    nv = nv_ref[i]
    changed = jnp.logical_or(i == 0, te_ref[i] != te_ref[jnp.maximum(i - 1, 0)])
    half = LANE
    blk = 2 * LANE
    d_ff = w2b_scr.shape[0]

    def fetch(e, s):
        return (pltpu.make_async_copy(w1_hbm.at[layer, e], w1_buf.at[s], sem.at[0, s]),
                pltpu.make_async_copy(w2_hbm.at[layer, e], w2_buf.at[s], sem.at[1, s]))

    @pl.when(i == 0)
    def _():
        for d in fetch(te_ref[0], sl_ref[0]):
            d.start()

    @pl.when(jnp.logical_and(nv > 0, changed))
    def _():
        s = sl_ref[i]
        for d in fetch(te_ref[i], s):
            d.wait()
        nx = nx_ref[i]

        @pl.when(nx >= 0)
        def _():
            for d in fetch(nx, 1 - s):
                d.start()

        src_r = lax.broadcasted_iota(jnp.int32, (blk, blk), 0)
        dst_c = lax.broadcasted_iota(jnp.int32, (blk, blk), 1)
        pick = jnp.where(dst_c < half, 2 * dst_c, 2 * (dst_c - half) + 1)
        perm = (src_r == pick).astype(BF16)
        for c in range(2 * d_ff // blk):
            wb = w1_buf[s, :, blk * c:blk * (c + 1)].astype(BF16)
            w1p_scr[:, blk * c:blk * (c + 1)] = jnp.dot(wb, perm, preferred_element_type=F32).astype(BF16)
        w2b_scr[...] = w2_buf[s].astype(BF16)

    @pl.when(nv > 0)
    def _():
        tm = x_ref.shape[0] // ROW_TILE
        x = _load_row_tiles(x_ref, tm).astype(BF16)
        u = jnp.dot(x, w1p_scr[...], preferred_element_type=F32)
        nblk = 2 * d_ff // blk
        ug = jnp.concatenate([u[:, blk * c:blk * c + half] for c in range(nblk)], axis=1) + b1g_ref[0, 0]
        ul = jnp.concatenate([u[:, blk * c + half:blk * (c + 1)] for c in range(nblk)], axis=1) + b1l_ref[0, 0]
        glu = jnp.minimum(ug, SWIGLU_LIMIT)
        lin = jnp.clip(ul, -SWIGLU_LIMIT, SWIGLU_LIMIT)
        act = glu * _sigmoid(SWIGLU_ALPHA * glu) * (lin + 1.0)
        _store_row_tiles(y_ref, jnp.dot(act.astype(BF16), w2b_scr[...], preferred_element_type=F32)
                         + b2_ref[0, 0])

    @pl.when(nv == 0)
    def _():
        y_ref[...] = jnp.zeros(y_ref.shape, F32)


def _moe_plan(counts, n_tiles):
    tm = MOE_TILE
    padded = (counts + tm - 1) // tm * tm
    pend = jnp.cumsum(padded)
    pstart = pend - padded
    tile_row0 = jnp.arange(n_tiles, dtype=jnp.int32) * tm
    tile_e = jnp.minimum(jnp.sum(tile_row0[:, None] >= pend[None, :], axis=1), N_EXPERTS - 1).astype(jnp.int32)
    tile_nv = jnp.clip(pstart[tile_e] + counts[tile_e] - tile_row0, 0, tm)
    tile_nv = jnp.where(tile_row0 < pend[-1], tile_nv, 0).astype(jnp.int32)
    experts = jnp.arange(N_EXPERTS, dtype=jnp.int32)
    active = counts > 0
    later = jnp.logical_and(active[None, :], experts[None, :] > experts[:, None])
    nxt = jnp.min(jnp.where(later, experts[None, :], N_EXPERTS), axis=1)
    nxt = jnp.where(nxt < N_EXPERTS, nxt, -1).astype(jnp.int32)
    ordinal = jnp.cumsum(active.astype(jnp.int32)) - 1
    return pstart, tile_e, tile_nv, nxt[tile_e], (ordinal[tile_e] % 2).astype(jnp.int32)


def _slots(top_e, rank, pstart):
    experts = jnp.arange(N_EXPERTS, dtype=jnp.int32)
    base = jnp.sum(jnp.where(top_e[:, :TOP_K, None] == experts, pstart, 0), axis=-1)
    return (base + rank[:, :TOP_K]).astype(jnp.int32)


def _moe_experts(xs, tile_e, tile_nv, tile_nxt, tile_slot, layer, w1, w2, b1g, b1l, b2):
    tm = MOE_TILE
    n_tiles = xs.shape[0] // (tm * ROW_TILE)
    d_ff = w2.shape[2]
    rows_spec = pl.BlockSpec((tm * ROW_TILE, LANE), lambda i, *_: (i, 0))
    bspec = lambda a: pl.BlockSpec((1, 1) + a.shape[2:], lambda i, te, *_: (layer, te[i], 0, 0))
    hbm = pl.BlockSpec(memory_space=pl.ANY)
    return pl.pallas_call(
        functools.partial(_expert_kernel, layer=layer),
        grid_spec=pltpu.PrefetchScalarGridSpec(
            num_scalar_prefetch=4,
            grid=(n_tiles,),
            in_specs=[rows_spec, hbm, hbm] + [bspec(a) for a in (b1g, b1l, b2)],
            out_specs=rows_spec,
            scratch_shapes=[pltpu.VMEM((2, D_MODEL, 2 * d_ff), F32), pltpu.VMEM((2, d_ff, D_MODEL), F32),
                            pltpu.VMEM((D_MODEL, 2 * d_ff), BF16), pltpu.VMEM((d_ff, D_MODEL), BF16),
                            pltpu.SemaphoreType.DMA((2, 2))]),
        out_shape=jax.ShapeDtypeStruct(xs.shape, F32),
        compiler_params=_params(("arbitrary",)),
        name="moe_experts",
    )(tile_e, tile_nv, tile_nxt, tile_slot, xs, w1, w2, b1g, b1l, b2)


def _combine_kernel(dest_ref, x_ref, tg_ref, g_ref, b_ref, ys_hbm, o_ref, ybuf, sem, *, tm):
    def row_copy(r, k, slot):
        src = ys_hbm.at[pl.ds(pl.multiple_of(slot * ROW_TILE, ROW_TILE), ROW_TILE)]
        dst = ybuf.at[k, pl.ds(pl.multiple_of(r * ROW_TILE, ROW_TILE), ROW_TILE)]
        return pltpu.make_async_copy(src, dst, sem)

    def start(r, c):
        for k in range(TOP_K):
            row_copy(r, k, dest_ref[0, 0, TOP_K * r + k]).start(priority=k % 2)
        return c

    def wait(r, c):
        for k in range(TOP_K):
            row_copy(0, k, 0).wait()
        return c

    lax.fori_loop(0, tm, start, 0, unroll=8)
    lax.fori_loop(0, tm, wait, 0, unroll=8)
    f = tg_ref[:, 0:1] * _load_row_tiles(ybuf, tm, 0)
    for k in range(1, TOP_K):
        f = f + tg_ref[:, k:k + 1] * _load_row_tiles(ybuf, tm, k)
    o_ref[...] = _layernorm(DEEPNORM_ALPHA * _load_row_tiles(x_ref, tm) + f, g_ref[...], b_ref[...])


def _combine(x1, tg, dest, ys, prm, tm):
    t = x1.shape[0] // ROW_TILE
    row = lambda w: pl.BlockSpec((tm, w), lambda i: (i, 0))
    full = lambda a: pl.BlockSpec(a.shape, lambda i: (0,) * a.ndim)
    return pl.pallas_call(
        functools.partial(_combine_kernel, tm=tm),
        grid=(t // tm,),
        in_specs=[pl.BlockSpec((1, 1, TOP_K * tm), lambda i: (i, 0, 0), memory_space=pltpu.SMEM),
                  pl.BlockSpec((tm * ROW_TILE, LANE), lambda i: (i, 0)), row(LANE),
                  full(prm['ln2_g']), full(prm['ln2_b']), pl.BlockSpec(memory_space=pl.ANY)],
        out_specs=row(D_MODEL),
        out_shape=jax.ShapeDtypeStruct((t, D_MODEL), F32),
        scratch_shapes=[pltpu.VMEM((TOP_K, tm * ROW_TILE, LANE), F32), pltpu.SemaphoreType.DMA],
        compiler_params=_params(("arbitrary",)),
        name="combine_ln",
    )(dest.reshape(t // tm, 1, TOP_K * tm), x1, tg, prm['ln2_g'], prm['ln2_b'], ys)


def _dt_pieces(dt_piece):
    return jnp.concatenate([dt_piece[:, 0:8], dt_piece[:, 128:136]], axis=1)


def _layer(xp, xs, prm, lb, rel_bias, st, n_p, seq, n_s, layer, moe, xs_buf):
    tp = n_p * seq
    up = _in_proj(xp, prm['w_in'], 128)
    dt16 = _dt_pieces(up['dt'])
    dt_t = jnp.transpose(dt16.reshape(tp // CHUNK, CHUNK, A_HEADS), (0, 2, 1))
    zeros_conv = jnp.zeros((n_p, SUBLANE, A_CONV_DIM), F32)
    ya, ssm_p = _ssd(up, dt_t, zeros_conv, jnp.zeros((n_p, D_MODEL, A_STATE), F32), prm, n_p, seq, False)
    yb, hg_p = _hgrn(up, jnp.zeros((n_p, D_MODEL, LANE), F32), lb, prm['hgrn_norm_g'], n_p, seq, False)
    attn = [_attn_prompt(up['cq'], up['ck'], up['cv'], _prompt_bias(rel_bias, g), g, n_p, seq)
            for g in range(3)]
    x1p, tep, tgp, rkp, cnt_p = _merge(xp, ya, yb, attn, up['gates'], jnp.zeros((1, LANE), F32), prm, 256)
    xbc3 = up['xbc'].reshape(n_p, seq, A_CONV_DIM)
    conv_p = xbc3[:, seq - (A_CONV - 1):]
    k3 = up['ck'].reshape(n_p, seq, 768)
    v3 = up['cv'].reshape(n_p, seq, 768)

    def last_rows(a, g, w):
        return a[:, seq - min(w, seq):, 256 * g:256 * (g + 1)].reshape(n_p, min(w, seq), 4, 64)

    kv_p = [jnp.stack([last_rows(k3, g, w), last_rows(v3, g, w)], axis=2) for g, (w, _) in enumerate(C_GROUPS)]
    us = _in_proj(xs, prm['w_in'], n_s)
    dt16s = _dt_pieces(us['dt'])
    dt_ts = jnp.zeros((n_s, A_HEADS, CHUNK), F32).at[:, :, 0].set(dt16s)
    conv0 = jnp.pad(st['conv'], ((0, 0), (SUBLANE - (A_CONV - 1), 0), (0, 0)))
    yas, ssm_s = _ssd(us, dt_ts, conv0, st['ssm'].reshape(n_s, D_MODEL, A_STATE), prm, n_s, 1, True)
    ybs, hg_s = _hgrn(us, st['hgrn'].reshape(n_s, D_MODEL, LANE), lb, prm['hgrn_norm_g'], n_s, 1, True)
    bias_d = [_decode_bias(rel_bias, g) for g in range(3)]
    o_s, lse_s = _attn_decode(us['cq'], us['ck'], us['cv'], st['kv_t'], layer,
                              [b[0] for b in bias_d], jnp.stack([b[1] for b in bias_d]))
    attn_s = [(o_s[:, 256 * g:256 * (g + 1)], lse_s[:, 256 * g:256 * (g + 1)]) for g in range(3)]
    x1s, tes, tgs, rks, cnt = _merge(xs, yas, ybs, attn_s, us['gates'], cnt_p, prm, n_s)
    conv_s = jnp.concatenate([st['conv'][:, 1:], us['xbc'][:, None]], axis=1)
    ks4 = us['ck'].reshape(n_s, 1, 3, 4, 64)
    vs4 = us['cv'].reshape(n_s, 1, 3, 4, 64)
    kv_s = [jnp.stack([ks4[:, :, g], vs4[:, :, g]], axis=2) for g in range(3)]
    n_tiles = -(-(tp + n_s) * TOP_K // MOE_TILE) + N_EXPERTS
    pstart, tile_e, tile_nv, tile_nxt, tile_slot = _moe_plan(cnt[0, :N_EXPERTS].astype(jnp.int32), n_tiles)
    dest_p = _slots(tep, rkp, pstart)
    dest_s = _slots(tes, rks, pstart)
    x_sorted = jnp.zeros((n_tiles * MOE_TILE * ROW_TILE, LANE), F32) if xs_buf is None else xs_buf
    x_sorted = _dispatch(x1p, dest_p, x_sorted, 256)
    x_sorted = _dispatch(x1s, dest_s, x_sorted, n_s)
    y_sorted = _moe_experts(x_sorted, tile_e, tile_nv, tile_nxt, tile_slot, layer, *moe)
    yp = _combine(x1p, tgp, dest_p, y_sorted, prm, 256)
    ys = _combine(x1s, tgs, dest_s, y_sorted, prm, n_s)
    states_p = (conv_p, ssm_p.reshape(n_p, A_HEADS, A_HEAD_DIM, A_STATE),
                hg_p.reshape(n_p, B_HEADS, B_KEY_DIM, LANE), kv_p[0], kv_p[1], kv_p[2])
    states_s = (conv_s, ssm_s.reshape(n_s, A_HEADS, A_HEAD_DIM, A_STATE),
                hg_s.reshape(n_s, B_HEADS, B_KEY_DIM, LANE), kv_s[0], kv_s[1], kv_s[2])
    return yp, ys, states_p, states_s, x_sorted


def _prep_layer(l, w_in, conv_w, conv_b, dt_bias, a_log, d_skip, ssm_norm_g, hgrn_norm_g,
                w_branch_a, w_branch_b, w_branch_c, w_out, ln1_g, ln1_b, router_w, router_b,
                moe_w1, moe_b1, moe_w2, moe_b2, ln2_g, ln2_b):
    def lanes_per_group(v):
        return jnp.zeros((1, 256), F32).at[0, 0:8].set(v[:8]).at[0, 128:136].set(v[8:])

    def sublanes_per_group(v):
        return jnp.broadcast_to(v[:, None], (A_HEADS, LANE))

    return {
        'w_in': _pack_w_in(w_in[l]),
        'conv_w8': jnp.pad(conv_w[l], ((0, SUBLANE - A_CONV), (0, 0))),
        'conv_b': conv_b[l][None],
        'dt_bias_l': lanes_per_group(dt_bias[l]), 'a_log_l': lanes_per_group(a_log[l]),
        'dt_bias_t': sublanes_per_group(dt_bias[l]), 'a_log_t': sublanes_per_group(a_log[l]),
        'd_skip_l': jnp.repeat(d_skip[l], A_HEAD_DIM)[None],
        'ssm_norm_g': ssm_norm_g[l][None],
        'hgrn_norm_g': hgrn_norm_g[l][None],
        'w_branch_a': w_branch_a[l].astype(BF16), 'w_branch_b': w_branch_b[l].astype(BF16),
        'w_branch_c': w_branch_c[l].astype(BF16), 'w_out': w_out[l].astype(BF16),
        'ln1_g': ln1_g[l][None], 'ln1_b': ln1_b[l][None],
        'router_w': jnp.pad(router_w[l], ((0, 0), (0, LANE - N_EXPERTS))),
        'router_b': jnp.pad(router_b[l], (0, LANE - N_EXPERTS), constant_values=-jnp.inf)[None],
        'ln2_g': ln2_g[l][None], 'ln2_b': ln2_b[l][None],
    }


def kernel(x_prompt, x_sample, state_conv, state_ssm, state_hgrn, cache_kv_w128, cache_kv_w512, cache_kv_w2048, w_in, conv_w, conv_b, dt_bias, a_log, d_skip, ssm_norm_g, hgrn_lb, hgrn_norm_g, rel_bias, w_branch_a, w_branch_b, w_branch_c, w_out, ln1_g, ln1_b, router_w, router_b, moe_w1, moe_b1, moe_w2, moe_b2, ln2_g, ln2_b):
    n_p, seq, _ = x_prompt.shape
    n_s = x_sample.shape[0]
    depth = w_in.shape[0]
    p_lb = jax.nn.softmax(hgrn_lb.astype(F32), axis=0)
    lower_bounds = jnp.cumsum(p_lb, axis=0) - p_lb[0]
    yp = x_prompt.reshape(n_p * seq, D_MODEL)
    ys = x_sample.reshape(n_s, D_MODEL)
    st_p, st_s = [], []
    xs_buf = None
    kv_t = tuple(jnp.transpose(c, (0, 1, 3, 4, 5, 2)) for c in (cache_kv_w128, cache_kv_w512, cache_kv_w2048))
    moe = (moe_w1, moe_w2, moe_b1[:, :, None, 0::2], moe_b1[:, :, None, 1::2], moe_b2[:, :, None, :])
    for l in range(depth):
        prm = _prep_layer(l, w_in, conv_w, conv_b, dt_bias, a_log, d_skip, ssm_norm_g, hgrn_norm_g,
                          w_branch_a, w_branch_b, w_branch_c, w_out, ln1_g, ln1_b, router_w, router_b,
                          moe_w1, moe_b1, moe_w2, moe_b2, ln2_g, ln2_b)
        st = {'conv': state_conv[l], 'ssm': state_ssm[l], 'hgrn': state_hgrn[l], 'kv_t': kv_t}
        yp, ys, sp, ss, xs_buf = _layer(yp, ys, prm, lower_bounds[l][None], rel_bias, st, n_p, seq, n_s, l, moe,
                                        xs_buf)
        st_p.append(sp)
        st_s.append(ss)
    stack = lambda sts, i: jnp.stack([s[i] for s in sts], axis=0)
    return (yp.reshape(n_p, seq, D_MODEL), ys.reshape(n_s, 1, D_MODEL),
            stack(st_p, 0), stack(st_s, 0), stack(st_p, 1), stack(st_s, 1), stack(st_p, 2), stack(st_s, 2),
            stack(st_p, 3), stack(st_s, 3), stack(st_p, 4), stack(st_s, 4), stack(st_p, 5), stack(st_s, 5))
```

```python
import functools
import math

import jax
import jax.numpy as jnp
import numpy as np
from jax import lax
from jax.experimental import pallas as pl
from jax.experimental.pallas import tpu as pltpu

F32 = jnp.float32
BF16 = jnp.bfloat16
HI = lax.Precision.HIGHEST

D_MODEL = 1024
A_HEADS = 16
A_HEAD_DIM = 64
A_STATE = 128
A_CONV = 4
A_CONV_DIM = 1536
B_HEADS = 8
B_KEY_DIM = 128
C_GROUPS = ((128, 1), (512, 4), (2048, 16))
C_SPAN = 128
C_GROUP_WIDTH = 256
REL_BUCKETS = 32
REL_MAX_DIST = 2048
N_EXPERTS = 32
TOP_K = 4
SWIGLU_ALPHA = 1.702
SWIGLU_LIMIT = 7.0
DEEPNORM_ALPHA = (2.0 * 2) ** 0.25
LN_EPS = 1e-5
RMS_EPS = 1e-5
NEG_BIG = -1e30
LOG_FLOOR = 1e-30

LANE = 128
SUBLANE = 8
CHUNK = 128
SUB = 8
HGRN_HEADS_PER_STEP = 4
ATTN_ILP = 4
VMEM_LIMIT = 56 * 1024 * 1024

IN_PIECES = (('z', 1024), ('xbc', 1536), ('bq', 1024), ('bf', 1024), ('bi', 1024), ('bg', 1024),
             ('cq', 768), ('ck', 768), ('cv', 768), ('gates', 3072), ('dt', 256))
IN_PACKED = sum(w for _, w in IN_PIECES)


def _params(sem):
    return pltpu.CompilerParams(dimension_semantics=sem, vmem_limit_bytes=VMEM_LIMIT)


def _sigmoid(x):
    return 1.0 / (1.0 + jnp.exp(-x))


def _softplus(x):
    return jnp.maximum(x, 0.0) + jnp.log(1.0 + jnp.exp(-jnp.abs(x)))


def _dot(a, b):
    return jnp.dot(a.astype(BF16), b.astype(BF16), preferred_element_type=F32)


def _dot_nt(a, b):
    return lax.dot_general(a.astype(BF16), b.astype(BF16), (((1,), (1,)), ((), ())),
                           preferred_element_type=F32)


def _dot_hi(a, b):
    return jnp.dot(a, b, preferred_element_type=F32, precision=HI)


ROW_TILE = D_MODEL // LANE


def _store_row_tiles(ref, val):
    n = val.shape[0]
    for c in range(ROW_TILE):
        ref[pl.ds(c, n, stride=ROW_TILE), :] = val[:, LANE * c:LANE * (c + 1)]


def _load_row_tiles(ref, n, lead=None):
    rows = pl.ds
    parts = []
    for c in range(ROW_TILE):
        idx = (rows(c, n, stride=ROW_TILE), slice(None))
        parts.append(ref[idx] if lead is None else ref[(lead,) + idx])
    return jnp.concatenate(parts, axis=1)


def _layernorm(h, g, b):
    mu = jnp.mean(h, -1, keepdims=True)
    c = h - mu
    var = jnp.mean(c * c, -1, keepdims=True)
    return c * lax.rsqrt(var + LN_EPS) * g + b


def _in_proj_kernel(x_ref, w_ref, *o_refs):
    xb = x_ref[...].astype(BF16)
    off = 0
    for o_ref in o_refs:
        wd = o_ref.shape[1]
        o_ref[...] = jnp.dot(xb, w_ref[:, off:off + wd], preferred_element_type=F32)
        off += wd


def _in_proj(x, w_packed, tm):
    t = x.shape[0]
    outs = pl.pallas_call(
        _in_proj_kernel,
        grid=(t // tm,),
        in_specs=[pl.BlockSpec((tm, D_MODEL), lambda i: (i, 0)),
                  pl.BlockSpec(memory_space=pltpu.VMEM)],
        out_specs=[pl.BlockSpec((tm, w), lambda i: (i, 0)) for _, w in IN_PIECES],
        out_shape=[jax.ShapeDtypeStruct((t, w), F32) for _, w in IN_PIECES],
        compiler_params=_params(("arbitrary",)),
        name="in_proj",
    )(x, w_packed)
    return {name: o for (name, _), o in zip(IN_PIECES, outs)}


def _pack_w_in(w):
    dt = w[:, 2560:2576]
    dtp = jnp.zeros((D_MODEL, 256), F32).at[:, 0:8].set(dt[:, :8]).at[:, 128:136].set(dt[:, 8:])
    return jnp.concatenate([w[:, :2560], w[:, 2576:], dtp], axis=1).astype(BF16)


def _stage(ref, ci, decode, r):
    if decode:
        row = ref[pl.ds(r, 1), :]
        rows = lax.broadcasted_iota(jnp.int32, (CHUNK, row.shape[1]), 0)
        return jnp.where(rows == 0, jnp.broadcast_to(row, (CHUNK, row.shape[1])), 0.0)
    return ref[pl.ds(pl.multiple_of(ci * CHUNK, CHUNK), CHUNK), :]


def _unstage(ref, val, ci, decode, r):
    if decode:
        ref[pl.ds(r, 1), :] = val[0:1, :]
    else:
        ref[pl.ds(pl.multiple_of(ci * CHUNK, CHUNK), CHUNK), :] = val


def _ssd_kernel(xs_ref, bm_ref, cm_ref, z_ref, dt_ref, dtt_ref,
                c0x_ref, c0b_ref, c0c_ref, h0_ref,
                wx_ref, wb_ref, wc_ref, bx_ref, bb_ref, bc_ref,
                dtb_ref, alog_ref, dtbt_ref, alogt_ref, dsk_ref, ng_ref,
                y_ref, hn_ref,
                h_scr, cx_scr, cb_scr, cc_scr, px_scr, pb_scr, pc_scr, *, nchunks, decode):
    b = pl.program_id(1)
    t = pl.program_id(2)
    r = b % SUBLANE
    q = CHUNK

    @pl.when(t == 0)
    def _():
        h_scr[...] = h0_ref[0]
        cx_scr[...] = c0x_ref[0]
        cb_scr[...] = c0b_ref[0]
        cc_scr[...] = c0c_ref[0]

    rows = lax.broadcasted_iota(jnp.int32, (q, q), 0)
    cols = lax.broadcasted_iota(jnp.int32, (q, q), 1)
    tril = rows >= cols
    tril_f = tril.astype(F32)
    triu_f = (rows <= cols).astype(F32)
    lane = lax.broadcasted_iota(jnp.int32, (q, LANE), 1)
    lo = lane < A_HEAD_DIM
    row_lo = lax.broadcasted_iota(jnp.int32, (LANE, LANE), 0) < A_HEAD_DIM
    valid_col = lax.broadcasted_iota(jnp.int32, (q, LANE), 0) == 0
    valid_row = lax.broadcasted_iota(jnp.int32, (SUBLANE, q), 1) == 0

    def conv(x, carry, pad, w_ref, b_ref):
        pad[0:SUBLANE, :] = carry[...]
        pad[SUBLANE:SUBLANE + q, :] = x
        if not decode:
            carry[...] = pad[q:q + SUBLANE, :]
        acc = b_ref[...]
        for j in range(A_CONV):
            acc = acc + pad[5 + j:5 + j + q, :] * w_ref[j:j + 1, :]
        return acc * _sigmoid(acc)

    def chunk(ci, carry_unused):
        xs = conv(_stage(xs_ref, ci, decode, r), cx_scr, px_scr, wx_ref, bx_ref)
        bm = conv(_stage(bm_ref, ci, decode, r), cb_scr, pb_scr, wb_ref, bb_ref)
        cm = conv(_stage(cm_ref, ci, decode, r), cc_scr, pc_scr, wc_ref, bc_ref)
        a_lane = -jnp.exp(alog_ref[...])
        a_sub = -jnp.exp(alogt_ref[...])
        dt = _softplus(_stage(dt_ref, ci, decode, r) + dtb_ref[...])
        dtt = _softplus(dtt_ref[ci] + dtbt_ref[...])
        if decode:
            dt = jnp.where(valid_col, dt, 0.0)
            dtt = jnp.where(valid_row, dtt, 0.0)
        if decode:
            acum = jnp.broadcast_to((dt * a_lane)[0:1, :], (q, LANE))
            acum_t = jnp.broadcast_to((dtt * a_sub)[:, 0:1], (SUBLANE, q))
        else:
            acum = _dot_hi(tril_f, dt * a_lane)
            acum_t = _dot_hi(dtt * a_sub, triu_f)
        a_last = acum[q - 1:q, :]
        cb = _dot_nt(cm, bm)
        cm_b = cm.astype(BF16)
        bm_b = bm.astype(BF16)
        ys = []
        for j in range(4):
            xp = xs[:, LANE * j:LANE * (j + 1)]
            xp_b = xp.astype(BF16)
            yd = []
            for half in range(2):
                hl = 2 * j + half
                diff = acum[:, hl:hl + 1] - acum_t[hl:hl + 1, :]
                lm = jnp.exp(jnp.where(tril, diff, NEG_BIG))
                m = cb * lm * dtt[hl:hl + 1, :]
                yd.append(jnp.dot(m.astype(BF16), xp_b, preferred_element_type=F32))
            y = jnp.where(lo, yd[0], yd[1])
            hp = h_scr[LANE * j:LANE * (j + 1), :]
            yo = lax.dot_general(cm_b, hp.astype(BF16), (((1,), (1,)), ((), ())),
                                 preferred_element_type=F32)
            e0 = acum[:, 2 * j:2 * j + 1]
            e1 = acum[:, 2 * j + 1:2 * j + 2]
            y = y + yo * jnp.exp(jnp.where(lo, e0, e1))
            w0 = dt[:, 2 * j:2 * j + 1] * jnp.exp(a_last[:, 2 * j:2 * j + 1] - e0)
            w1 = dt[:, 2 * j + 1:2 * j + 2] * jnp.exp(a_last[:, 2 * j + 1:2 * j + 2] - e1)
            xw = xp * jnp.where(lo, w0, w1)
            upd = jnp.dot(xw.T.astype(BF16), bm_b, preferred_element_type=F32)
            dec = jnp.exp(jnp.where(row_lo, a_last[:, 2 * j:2 * j + 1], a_last[:, 2 * j + 1:2 * j + 2]))
            h_scr[LANE * j:LANE * (j + 1), :] = hp * dec + upd
            ys.append(y)
        y = jnp.concatenate(ys, axis=1) + dsk_ref[...] * xs
        z = _stage(z_ref, ci, decode, r)
        y = y * (z * _sigmoid(z))
        y = y * lax.rsqrt(jnp.mean(y * y, -1, keepdims=True) + RMS_EPS) * ng_ref[...]
        _unstage(y_ref, y, ci, decode, r)
        return carry_unused

    lax.fori_loop(0, nchunks, chunk, 0)

    @pl.when(t == pl.num_programs(2) - 1)
    def _():
        hn_ref[0] = h_scr[...]


def _ssd(u, dt_t, conv0_8, h0, prm, n, seq, decode):
    tb = CHUNK if decode else min(512, seq)
    nb = 1 if decode else seq // tb
    nchunks = tb // CHUNK
    rb = SUBLANE if decode else tb

    def rowmap(lane_block):
        if decode:
            return lambda g, b, t: (b // SUBLANE, lane_block(g))
        return lambda g, b, t: (b * nb + t, lane_block(g))

    chunk_map = (lambda g, b, t: (b, g, 0)) if decode else (lambda g, b, t: (b * nb + t, g, 0))
    cw = (512, LANE, LANE)
    lane_blocks = (lambda g: g, lambda g: 8 + g, lambda g: 10 + g)
    in_specs = [
        pl.BlockSpec((rb, 512), rowmap(lane_blocks[0])),
        pl.BlockSpec((rb, LANE), rowmap(lane_blocks[1])),
        pl.BlockSpec((rb, LANE), rowmap(lane_blocks[2])),
        pl.BlockSpec((rb, 512), rowmap(lambda g: g)),
        pl.BlockSpec((rb, LANE), rowmap(lambda g: g)),
        pl.BlockSpec((nchunks, SUBLANE, CHUNK), chunk_map),
    ]
    in_specs += [pl.BlockSpec((1, SUBLANE, w), (lambda lb: (lambda g, b, t: (b, 0, lb(g))))(lb))
                 for w, lb in zip(cw, lane_blocks)]
    in_specs += [pl.BlockSpec((1, 512, A_STATE), lambda g, b, t: (b, g, 0))]
    in_specs += [pl.BlockSpec((SUBLANE, w), (lambda lb: (lambda g, b, t: (0, lb(g))))(lb))
                 for w, lb in zip(cw, lane_blocks)]
    in_specs += [pl.BlockSpec((1, w), (lambda lb: (lambda g, b, t: (0, lb(g))))(lb))
                 for w, lb in zip(cw, lane_blocks)]
    in_specs += [
        pl.BlockSpec((1, LANE), lambda g, b, t: (0, g)),
        pl.BlockSpec((1, LANE), lambda g, b, t: (0, g)),
        pl.BlockSpec((SUBLANE, LANE), lambda g, b, t: (g, 0)),
        pl.BlockSpec((SUBLANE, LANE), lambda g, b, t: (g, 0)),
        pl.BlockSpec((1, 512), lambda g, b, t: (0, g)),
        pl.BlockSpec((1, 512), lambda g, b, t: (0, g)),
    ]
    rows_total = u['z'].shape[0]
    y, hn = pl.pallas_call(
        functools.partial(_ssd_kernel, nchunks=nchunks, decode=decode),
        grid=(2, n, nb),
        in_specs=in_specs,
        out_specs=[pl.BlockSpec((rb, 512), rowmap(lambda g: g)),
                   pl.BlockSpec((1, 512, A_STATE), lambda g, b, t: (b, g, 0))],
        out_shape=[jax.ShapeDtypeStruct((rows_total, D_MODEL), F32),
                   jax.ShapeDtypeStruct((n, D_MODEL, A_STATE), F32)],
        scratch_shapes=[pltpu.VMEM((512, A_STATE), F32),
                        pltpu.VMEM((SUBLANE, 512), F32), pltpu.VMEM((SUBLANE, LANE), F32),
                        pltpu.VMEM((SUBLANE, LANE), F32),
                        pltpu.VMEM((CHUNK + SUBLANE, 512), F32), pltpu.VMEM((CHUNK + SUBLANE, LANE), F32),
                        pltpu.VMEM((CHUNK + SUBLANE, LANE), F32)],
        compiler_params=_params(("arbitrary", "arbitrary", "arbitrary")),
        name="ssd_decode" if decode else "ssd_prompt",
    )(u['xbc'], u['xbc'], u['xbc'], u['z'], u['dt'], dt_t,
      conv0_8, conv0_8, conv0_8, h0,
      prm['conv_w8'], prm['conv_w8'], prm['conv_w8'], prm['conv_b'], prm['conv_b'], prm['conv_b'],
      prm['dt_bias_l'], prm['a_log_l'], prm['dt_bias_t'], prm['a_log_t'], prm['d_skip_l'], prm['ssm_norm_g'])
    return y, hn


def _hgrn_kernel(q_ref, f_ref, i_ref, g_ref, lb_ref, ng_ref, s0_ref, y_ref, sn_ref, st_scr,
                 *, nchunks, decode, nheads):
    b = pl.program_id(1)
    t = pl.program_id(2)
    r = b % SUBLANE
    c = CHUNK

    @pl.when(t == 0)
    def _():
        for hh in range(nheads):
            st_scr[hh] = s0_ref[0, LANE * hh:LANE * (hh + 1), :].T

    rows = lax.broadcasted_iota(jnp.int32, (c, c), 0)
    cols = lax.broadcasted_iota(jnp.int32, (c, c), 1)
    tril_f = (rows >= cols).astype(F32)
    row_c = lax.broadcasted_iota(jnp.int32, (c, LANE), 0)
    row_s = lax.broadcasted_iota(jnp.int32, (SUB, LANE), 0)
    blk_xor = (rows // SUB) ^ (cols // SUB)
    level = jnp.where(blk_xor >= 8, 3, jnp.where(blk_xor >= 4, 2, jnp.where(blk_xor >= 2, 1, 0)))
    level = jnp.where((rows // SUB) > (cols // SUB), level, -1)
    nlevels = 4
    assert SUB << nlevels == c

    def one_head(hh, ci):
        ls = slice(LANE * hh, LANE * (hh + 1))
        lbv = lb_ref[:, ls]
        f = _stage(f_ref, ci, decode, r)[:, ls]
        qr = _stage(q_ref, ci, decode, r)[:, ls]
        v = _stage(i_ref, ci, decode, r)[:, ls]
        gr = _stage(g_ref, ci, decode, r)[:, ls]
        q = qr * _sigmoid(qr) * (B_KEY_DIM ** -0.5)
        e_f = jnp.exp(-jnp.abs(f))
        r_f = 1.0 / (1.0 + e_f)
        pos = f >= 0.0
        sig_p = jnp.where(pos, r_f, e_f * r_f)
        sig_n = jnp.where(pos, e_f * r_f, r_f)
        logf = jnp.log(jnp.maximum(lbv + (1.0 - lbv) * sig_p, LOG_FLOOR))
        k = (1.0 - lbv) * sig_n
        if decode:
            logf = jnp.where(row_c == 0, logf, 0.0)
            k = jnp.where(row_c == 0, k, 0.0)
        if decode:
            bcum = jnp.broadcast_to(logf[0:1, :], (c, LANE))
        else:
            bcum = _dot_hi(tril_f, logf)
        b_last = bcum[c - 1:c, :]
        st = st_scr[hh]
        o = _dot_nt(q * jnp.exp(bcum), st)
        if not decode:
            att = jnp.zeros((c, c), F32)
            for lv in range(nlevels):
                h = SUB << lv
                ref = jnp.concatenate(
                    [jnp.broadcast_to(bcum[2 * h * m + h - 1:2 * h * m + h, :], (2 * h, LANE))
                     for m in range(c // (2 * h))], axis=0)
                x = jnp.exp(-jnp.abs(bcum - ref))
                att = jnp.where(level == lv, _dot_nt(q * x, k * x), att)
            o = o + _dot(att, v)
        diag = []
        for i in range(c // SUB):
            if decode and i > 0:
                diag.append(jnp.zeros((SUB, LANE), F32))
                continue
            sl = slice(SUB * i, SUB * (i + 1))
            qb, kb, vb, bb = q[sl, :], k[sl, :], v[sl, :], bcum[sl, :]
            od = jnp.zeros((SUB, LANE), F32)
            for s in range(1 if decode else SUB):
                e = jnp.exp(jnp.where(row_s >= s, bb - bb[s:s + 1, :], NEG_BIG))
                rs = jnp.sum(qb * kb[s:s + 1, :] * e, -1, keepdims=True)
                od = od + rs * vb[s:s + 1, :]
            diag.append(od)
        o = o + jnp.concatenate(diag, axis=0)
        k2 = k * jnp.exp(b_last - bcum)
        st_scr[hh] = jnp.exp(b_last) * st + jnp.dot(v.T.astype(BF16), k2.astype(BF16),
                                                     preferred_element_type=F32)
        gate = gr * _sigmoid(gr)
        return o * lax.rsqrt(jnp.mean(o * o, -1, keepdims=True) + RMS_EPS) * ng_ref[...] * gate

    def chunk(ci, carry_unused):
        y = [one_head(hh, ci) for hh in range(nheads)]
        _unstage(y_ref, y[0] if nheads == 1 else jnp.concatenate(y, axis=1), ci, decode, r)
        return carry_unused

    lax.fori_loop(0, nchunks, chunk, 0)

    @pl.when(t == pl.num_programs(2) - 1)
    def _():
        for hh in range(nheads):
            sn_ref[0, LANE * hh:LANE * (hh + 1), :] = st_scr[hh].T


def _hgrn(u, s0, lb, ng, n, seq, decode):
    tb = CHUNK if decode else min(512, seq)
    nb = 1 if decode else seq // tb
    nchunks = tb // CHUNK
    rb = SUBLANE if decode else tb
    rowmap = (lambda h, b, t: (b // SUBLANE, h)) if decode else (lambda h, b, t: (b * nb + t, h))
    rows_total = u['bq'].shape[0]
    nheads = B_HEADS if decode else HGRN_HEADS_PER_STEP
    wd = LANE * nheads
    y, sn = pl.pallas_call(
        functools.partial(_hgrn_kernel, nchunks=nchunks, decode=decode, nheads=nheads),
        grid=(B_HEADS // nheads, n, nb),
        in_specs=[pl.BlockSpec((rb, wd), rowmap)] * 4 + [
            pl.BlockSpec((1, wd), lambda h, b, t: (0, h)),
            pl.BlockSpec((1, LANE), lambda h, b, t: (0, 0)),
            pl.BlockSpec((1, B_KEY_DIM * nheads, LANE), lambda h, b, t: (b, h, 0))],
        out_specs=[pl.BlockSpec((rb, wd), rowmap),
                   pl.BlockSpec((1, B_KEY_DIM * nheads, LANE), lambda h, b, t: (b, h, 0))],
        out_shape=[jax.ShapeDtypeStruct((rows_total, D_MODEL), F32),
                   jax.ShapeDtypeStruct((n, D_MODEL, LANE), F32)],
        scratch_shapes=[pltpu.VMEM((nheads, LANE, B_KEY_DIM), F32)],
        compiler_params=_params(("arbitrary", "arbitrary", "arbitrary")),
        name="hgrn_decode" if decode else "hgrn_prompt",
    )(u['bq'], u['bf'], u['bi'], u['bg'], lb, ng, s0)
    return y, sn


def _t5_bucket_np(dist):
    exact = REL_BUCKETS // 2
    d = np.maximum(dist, 1).astype(np.float32)
    large = exact + (np.log(d / np.float32(exact)) / np.float32(math.log(REL_MAX_DIST / exact))
                     * np.float32(REL_BUCKETS - exact)).astype(np.int32)
    large = np.clip(large, 0, REL_BUCKETS - 1)
    return np.where(dist < exact, dist, large)


def _attn_prompt_kernel(q_ref, kc_ref, kp_ref, vc_ref, vp_ref, bias_ref, o_ref, lse_ref, *, dil):
    first = pl.program_id(1) == 0
    s_q = C_SPAN
    lane = lax.broadcasted_iota(jnp.int32, (s_q, LANE), 1)
    lo = lane < 64
    kcol = lax.broadcasted_iota(jnp.int32, (s_q, 2 * s_q), 1)
    no_prev = jnp.logical_and(first, kcol < s_q)

    pair = pl.program_id(2)

    def solve(qp, kprev, kcur, vprev, vcur, mask_prev):
        kp2 = jnp.concatenate([kprev, kcur], axis=0).astype(BF16)
        vp2 = jnp.concatenate([vprev, vcur], axis=0).astype(BF16)
        oh, lh = [], []
        for half in range(2):
            qm = jnp.where(lo if half == 0 else jnp.logical_not(lo), qp, 0.0).astype(BF16)
            s = lax.dot_general(qm, kp2, (((1,), (1,)), ((), ())), preferred_element_type=F32)
            s = s * (64 ** -0.5) + bias_ref[2 * pair + half]
            if mask_prev:
                s = jnp.where(no_prev, NEG_BIG, s)
            m = jnp.max(s, -1, keepdims=True)
            p = jnp.exp(s - m)
            den = jnp.sum(p, -1, keepdims=True)
            oh.append(jnp.dot((p / den).astype(BF16), vp2, preferred_element_type=F32))
            lh.append(m + jnp.log(den))
        return jnp.where(lo, oh[0], oh[1]), jnp.where(lo, lh[0], lh[1])

    if dil == 1:
        for sb in range(ATTN_ILP):
            cur = slice(s_q * sb, s_q * (sb + 1))
            if sb == 0:
                o, lse = solve(q_ref[cur, :], kp_ref[...], kc_ref[cur, :], vp_ref[...], vc_ref[cur, :], True)
            else:
                prv = slice(s_q * (sb - 1), s_q * sb)
                o, lse = solve(q_ref[cur, :], kc_ref[prv, :], kc_ref[cur, :], vc_ref[prv, :], vc_ref[cur, :],
                               False)
            o_ref[cur, :] = o
            lse_ref[cur, :] = lse
    else:
        def residues(it, carry):
            for jj in range(ATTN_ILP):
                sl = pl.ds(it * ATTN_ILP + jj, s_q, stride=dil)
                o, lse = solve(q_ref[sl, :], kp_ref[sl, :], kc_ref[sl, :], vp_ref[sl, :], vc_ref[sl, :], True)
                o_ref[sl, :] = o
                lse_ref[sl, :] = lse
            return carry

        lax.fori_loop(0, dil // ATTN_ILP, residues, 0)


def _attn_prompt(cq, ck, cv, bias, g, n, seq):
    dil = C_GROUPS[g][1]
    look = C_SPAN * dil
    rows_blk = look * (ATTN_ILP if dil == 1 else 1)
    nbk = seq // rows_blk
    per = rows_blk // look
    cur = lambda b, i, j: (b * nbk + i, 2 * g + j)
    prev = lambda b, i, j: (jnp.maximum((b * nbk + i) * per - 1, 0), 2 * g + j)
    blk = (rows_blk, LANE)
    pblk = (look, LANE)
    return pl.pallas_call(
        functools.partial(_attn_prompt_kernel, dil=dil),
        grid=(n, nbk, 2),
        in_specs=[pl.BlockSpec(blk, cur), pl.BlockSpec(blk, cur), pl.BlockSpec(pblk, prev),
                  pl.BlockSpec(blk, cur), pl.BlockSpec(pblk, prev),
                  pl.BlockSpec((4, C_SPAN, 2 * C_SPAN), lambda b, i, j: (0, 0, 0))],
        out_specs=[pl.BlockSpec(blk, lambda b, i, j: (b * nbk + i, j))] * 2,
        out_shape=[jax.ShapeDtypeStruct((n * seq, C_GROUP_WIDTH), F32)] * 2,
        compiler_params=_params(("arbitrary", "arbitrary", "arbitrary")),
        name=f"attn_prompt_g{g}",
    )(cq, ck, ck, cv, cv, bias)


def _prompt_bias(rel_bias, g):
    dil = C_GROUPS[g][1]
    qi = np.arange(C_SPAN)[:, None]
    kj = np.arange(2 * C_SPAN)[None, :]
    rel = qi + C_SPAN - kj
    band = (rel >= 0) & (rel <= C_SPAN)
    idx = _t5_bucket_np(np.maximum(rel, 0) * dil)
    tab = rel_bias[:, 4 * g:4 * g + 4]
    hit = jnp.asarray(idx)[None, :, :, None] == jnp.arange(REL_BUCKETS)
    vals = jnp.sum(jnp.where(hit, tab.T[:, None, None, :], 0.0), axis=-1)
    return jnp.where(band[None], vals, NEG_BIG)


def _decode_bias(rel_bias, g):
    win, dil = C_GROUPS[g]
    tab = rel_bias[:, 4 * g:4 * g + 4]
    pos = np.arange(win)
    vals = jnp.where((pos % dil == 0)[:, None], tab[_t5_bucket_np(win - pos)], NEG_BIG)
    bias_buf = jnp.zeros((2, SUBLANE, win), F32).at[:, 0:2].set(vals.T.reshape(2, 2, win))
    new = jnp.broadcast_to(tab[0].reshape(2, 2, 1), (2, 2, LANE))
    bias_new = jnp.zeros((2, SUBLANE, LANE), F32).at[:, 0:2].set(new)
    return bias_buf, bias_new


def _attn_decode_kernel(q_ref, k_ref, v_ref, c0_ref, c1_ref, c2_ref, bb0_ref, bb1_ref, bb2_ref, bn_ref,
                        o_ref, lse_ref):
    b = pl.program_id(0)
    row8 = lax.broadcasted_iota(jnp.int32, (SUBLANE, LANE), 0)
    lane8 = lax.broadcasted_iota(jnp.int32, (SUBLANE, LANE), 1)
    qmask = jnp.logical_or(jnp.logical_and(row8 == 0, lane8 < 64), jnp.logical_and(row8 == 1, lane8 >= 64))
    lo1 = lax.broadcasted_iota(jnp.int32, (1, LANE), 1) < 64
    q_all = q_ref[pl.ds(b, 1), :]
    k_all = k_ref[pl.ds(b, 1), :]
    v_all = v_ref[pl.ds(b, 1), :]
    o_parts, lse_parts = [], []
    for g, (buf_ref, bb_ref) in enumerate(((c0_ref, bb0_ref), (c1_ref, bb1_ref), (c2_ref, bb2_ref))):
        win = C_GROUPS[g][0]
        for j in range(2):
            c0 = C_GROUP_WIDTH * g + LANE * j
            qrow = q_all[:, c0:c0 + LANE]
            knew = k_all[:, c0:c0 + LANE]
            vnew = v_all[:, c0:c0 + LANE]
            q8 = jnp.where(qmask, jnp.broadcast_to(qrow, (SUBLANE, LANE)), 0.0)
            kt = buf_ref[0, 0, 0, 2 * j:2 * j + 2].reshape(LANE, win)
            vt = buf_ref[0, 0, 1, 2 * j:2 * j + 2].reshape(LANE, win)
            s = _dot(q8, kt) * (64 ** -0.5) + bb_ref[j]
            snew = jnp.sum(q8 * knew, -1, keepdims=True) * (64 ** -0.5) + bn_ref[g, j][:, 0:1]
            m = jnp.maximum(jnp.max(s, -1, keepdims=True), snew)
            p = jnp.exp(s - m)
            pn = jnp.exp(snew - m)
            den = jnp.sum(p, -1, keepdims=True) + pn
            o8 = _dot_nt(p / den, vt) + (pn / den) * vnew
            lse8 = jnp.broadcast_to(m + jnp.log(den), (SUBLANE, LANE))
            o_parts.append(jnp.where(lo1, o8[0:1, :], o8[1:2, :]))
            lse_parts.append(jnp.where(lo1, lse8[0:1, :], lse8[1:2, :]))
    o_ref[pl.ds(b, 1), :] = jnp.concatenate(o_parts, axis=1)
    lse_ref[pl.ds(b, 1), :] = jnp.concatenate(lse_parts, axis=1)


def _attn_decode(cq, ck, cv, caches_t, layer, bias_bufs, bias_new):
    n = cq.shape[0]
    full = lambda a: pl.BlockSpec(a.shape, lambda b: (0,) * a.ndim)
    cache_spec = lambda c: pl.BlockSpec((1, 1) + c.shape[2:], lambda b: (layer, b, 0, 0, 0, 0))
    return pl.pallas_call(
        _attn_decode_kernel,
        grid=(n,),
        in_specs=[full(cq), full(ck), full(cv)] + [cache_spec(c) for c in caches_t]
        + [full(bb) for bb in bias_bufs] + [full(bias_new)],
        out_specs=[pl.BlockSpec((n, 768), lambda b: (0, 0))] * 2,
        out_shape=[jax.ShapeDtypeStruct((n, 768), F32)] * 2,
        compiler_params=_params(("arbitrary",)),
        name="attn_decode",
    )(cq, ck, cv, *caches_t, *bias_bufs, bias_new)


def _merge_kernel(x_ref, ya_ref, yb_ref, o0_ref, o1_ref, o2_ref, l0_ref, l1_ref, l2_ref, gt_ref,
                  wa_ref, wb_ref, wc_ref, wo_ref, g_ref, b_ref, rw_ref, rb_ref, cnt0_ref,
                  x1_ref, te_ref, tg_ref, rk_ref, cnt_ref, cnt_scr):
    l0, l1, l2 = l0_ref[...], l1_ref[...], l2_ref[...]
    lm = jnp.maximum(jnp.maximum(l0, l1), l2)
    e0, e1, e2 = jnp.exp(l0 - lm), jnp.exp(l1 - lm), jnp.exp(l2 - lm)
    den = e0 + e1 + e2
    yc = (e0 / den) * o0_ref[...] + (e1 / den) * o1_ref[...] + (e2 / den) * o2_ref[...]
    ga = _sigmoid(gt_ref[:, 0:D_MODEL])
    gb = _sigmoid(gt_ref[:, D_MODEL:2 * D_MODEL])
    gc = _sigmoid(gt_ref[:, 2 * D_MODEL:3 * D_MODEL])
    merged = (ga * _dot(ya_ref[...], wa_ref[...]) + gb * _dot(yb_ref[...], wb_ref[...])
              + gc * _dot(yc, wc_ref[...]))
    h = DEEPNORM_ALPHA * x_ref[...] + _dot(merged, wo_ref[...])
    x1 = _layernorm(h, g_ref[...], b_ref[...])
    _store_row_tiles(x1_ref, x1)
    logits = _dot(x1, rw_ref[...]) + rb_ref[...]
    lane = lax.broadcasted_iota(jnp.int32, logits.shape, 1)
    lane_f = lane.astype(F32)
    te = jnp.zeros(logits.shape, F32)
    vals, onehots = [], []
    for k in range(TOP_K):
        m = jnp.max(logits, -1, keepdims=True)
        idx = jnp.min(jnp.where(logits == m, lane_f, float(LANE)), -1, keepdims=True)
        te = jnp.where(lane == k, idx, te)
        vals.append(m)
        hit = lane_f == idx
        onehots.append(hit.astype(F32))
        logits = jnp.where(hit, -jnp.inf, logits)
    ex = [jnp.exp(v - vals[0]) for v in vals]
    tot = ex[0] + ex[1] + ex[2] + ex[3]
    tg = jnp.zeros(logits.shape, F32)
    for k in range(TOP_K):
        tg = jnp.where(lane == k, ex[k] / tot, tg)
    te_ref[...] = te.astype(jnp.int32)
    tg_ref[...] = tg
    @pl.when(pl.program_id(0) == 0)
    def _():
        cnt_scr[...] = cnt0_ref[...]

    tm = logits.shape[0]
    oh = onehots[0] + onehots[1] + onehots[2] + onehots[3]
    earlier = (lax.broadcasted_iota(jnp.int32, (tm, tm), 0)
               > lax.broadcasted_iota(jnp.int32, (tm, tm), 1)).astype(BF16)
    before = jnp.dot(earlier, oh.astype(BF16), preferred_element_type=F32) + cnt_scr[...]
    rank = jnp.zeros(logits.shape, F32)
    for k in range(TOP_K):
        rank = jnp.where(lane == k, jnp.sum(onehots[k] * before, -1, keepdims=True), rank)
    rk_ref[...] = rank.astype(jnp.int32)
    cnt_scr[...] = cnt_scr[...] + jnp.sum(oh, 0, keepdims=True)
    cnt_ref[...] = cnt_scr[...]


def _merge(x, ya, yb, attn, gates, cnt0, prm, tm):
    t = x.shape[0]
    row = lambda w: pl.BlockSpec((tm, w), lambda i: (i, 0))
    full = lambda a: pl.BlockSpec(a.shape, lambda i: (0,) * a.ndim)
    ws = [prm['w_branch_a'], prm['w_branch_b'], prm['w_branch_c'], prm['w_out'],
          prm['ln1_g'], prm['ln1_b'], prm['router_w'], prm['router_b'], cnt0]
    (o0, l0), (o1, l1), (o2, l2) = attn
    return pl.pallas_call(
        _merge_kernel,
        grid=(t // tm,),
        in_specs=[row(D_MODEL)] * 3 + [row(C_GROUP_WIDTH)] * 6 + [row(3 * D_MODEL)] + [full(a) for a in ws],
        out_specs=[pl.BlockSpec((tm * ROW_TILE, LANE), lambda i: (i, 0)), row(LANE), row(LANE), row(LANE),
                   pl.BlockSpec((1, LANE), lambda i: (0, 0))],
        out_shape=[jax.ShapeDtypeStruct((t * ROW_TILE, LANE), F32), jax.ShapeDtypeStruct((t, LANE), jnp.int32),
                   jax.ShapeDtypeStruct((t, LANE), F32), jax.ShapeDtypeStruct((t, LANE), jnp.int32),
                   jax.ShapeDtypeStruct((1, LANE), F32)],
        scratch_shapes=[pltpu.VMEM((1, LANE), F32)],
        compiler_params=_params(("arbitrary",)),
        name="merge_ln_router",
    )(x, ya, yb, o0, o1, o2, l0, l1, l2, gates, *ws)


MOE_TILE = 512


def _dispatch_kernel(dest_ref, x_ref, xs_in_hbm, xs_hbm, sem, *, tm):
    del xs_in_hbm

    def row_copy(r, slot):
        src = x_ref.at[pl.ds(pl.multiple_of(r * ROW_TILE, ROW_TILE), ROW_TILE)]
        dst = xs_hbm.at[pl.ds(pl.multiple_of(slot * ROW_TILE, ROW_TILE), ROW_TILE)]
        return pltpu.make_async_copy(src, dst, sem)

    def start(r, c):
        for k in range(TOP_K):
            row_copy(r, dest_ref[0, 0, TOP_K * r + k]).start(priority=k % 2)
        return c

    def wait(r, c):
        for _ in range(TOP_K):
            row_copy(0, 0).wait()
        return c

    lax.fori_loop(0, tm, start, 0, unroll=8)
    lax.fori_loop(0, tm, wait, 0, unroll=8)


def _dispatch(x1, dest, xs, tm):
    t = x1.shape[0] // ROW_TILE
    return pl.pallas_call(
        functools.partial(_dispatch_kernel, tm=tm),
        grid=(t // tm,),
        in_specs=[pl.BlockSpec((1, 1, TOP_K * tm), lambda i: (i, 0, 0), memory_space=pltpu.SMEM),
                  pl.BlockSpec((tm * ROW_TILE, LANE), lambda i: (i, 0)),
                  pl.BlockSpec(memory_space=pl.ANY)],
        out_specs=pl.BlockSpec(memory_space=pl.ANY),
        out_shape=jax.ShapeDtypeStruct(xs.shape, F32),
        scratch_shapes=[pltpu.SemaphoreType.DMA],
        input_output_aliases={2: 0},
        compiler_params=_params(("arbitrary",)),
        name="moe_dispatch",
    )(dest.reshape(t // tm, 1, TOP_K * tm), x1, xs)


def _expert_kernel(te_ref, nv_ref, nx_ref, sl_ref, x_ref, w1_hbm, w2_hbm, b1g_ref, b1l_ref, b2_ref, y_ref,
                   w1_buf, w2_buf, w1p_scr, w2b_scr, sem, *, layer):
    i = pl.program_id(0)
    nv = nv_ref[i]
    changed = jnp.logical_or(i == 0, te_ref[i] != te_ref[jnp.maximum(i - 1, 0)])
    half = LANE
    blk = 2 * LANE
    d_ff = w2b_scr.shape[0]

    def fetch(e, s):
        return (pltpu.make_async_copy(w1_hbm.at[layer, e], w1_buf.at[s], sem.at[0, s]),
                pltpu.make_async_copy(w2_hbm.at[layer, e], w2_buf.at[s], sem.at[1, s]))

    @pl.when(i == 0)
    def _():
        for d in fetch(te_ref[0], sl_ref[0]):
            d.start()

    @pl.when(jnp.logical_and(nv > 0, changed))
    def _():
        s = sl_ref[i]
        for d in fetch(te_ref[i], s):
            d.wait()
        nx = nx_ref[i]

        @pl.when(nx >= 0)
        def _():
            for d in fetch(nx, 1 - s):
                d.start()

        src_r = lax.broadcasted_iota(jnp.int32, (blk, blk), 0)
        dst_c = lax.broadcasted_iota(jnp.int32, (blk, blk), 1)
        pick = jnp.where(dst_c < half, 2 * dst_c, 2 * (dst_c - half) + 1)
        perm = (src_r == pick).astype(BF16)
        for c in range(2 * d_ff // blk):
            wb = w1_buf[s, :, blk * c:blk * (c + 1)].astype(BF16)
            w1p_scr[:, blk * c:blk * (c + 1)] = jnp.dot(wb, perm, preferred_element_type=F32).astype(BF16)
        w2b_scr[...] = w2_buf[s].astype(BF16)

    tm = x_ref.shape[0] // ROW_TILE

    def mlp(rows):
        x = _load_row_tiles(x_ref, rows).astype(BF16)
        u = jnp.dot(x, w1p_scr[...], preferred_element_type=F32)
        nblk = 2 * d_ff // blk
        ug = jnp.concatenate([u[:, blk * c:blk * c + half] for c in range(nblk)], axis=1) + b1g_ref[0, 0]
        ul = jnp.concatenate([u[:, blk * c + half:blk * (c + 1)] for c in range(nblk)], axis=1) + b1l_ref[0, 0]
        glu = jnp.minimum(ug, SWIGLU_LIMIT)
        lin = jnp.clip(ul, -SWIGLU_LIMIT, SWIGLU_LIMIT)
        act = glu * _sigmoid(SWIGLU_ALPHA * glu) * (lin + 1.0)
        y = jnp.dot(act.astype(BF16), w2b_scr[...], preferred_element_type=F32) + b2_ref[0, 0]
        _store_row_tiles(y_ref, y)
        if rows < tm:
            y_ref[pl.ds(rows * ROW_TILE, (tm - rows) * ROW_TILE), :] = jnp.zeros(((tm - rows) * ROW_TILE, LANE), F32)

    @pl.when(nv > tm // 2)
    def _():
        mlp(tm)

    @pl.when(jnp.logical_and(nv > 0, nv <= tm // 2))
    def _():
        mlp(tm // 2)

    @pl.when(nv == 0)
    def _():
        y_ref[...] = jnp.zeros(y_ref.shape, F32)


def _moe_plan(counts, n_tiles):
    tm = MOE_TILE
    padded = (counts + tm - 1) // tm * tm
    pend = jnp.cumsum(padded)
    pstart = pend - padded
    tile_row0 = jnp.arange(n_tiles, dtype=jnp.int32) * tm
    tile_e = jnp.minimum(jnp.sum(tile_row0[:, None] >= pend[None, :], axis=1), N_EXPERTS - 1).astype(jnp.int32)
    tile_nv = jnp.clip(pstart[tile_e] + counts[tile_e] - tile_row0, 0, tm)
    tile_nv = jnp.where(tile_row0 < pend[-1], tile_nv, 0).astype(jnp.int32)
    experts = jnp.arange(N_EXPERTS, dtype=jnp.int32)
    active = counts > 0
    later = jnp.logical_and(active[None, :], experts[None, :] > experts[:, None])
    nxt = jnp.min(jnp.where(later, experts[None, :], N_EXPERTS), axis=1)
    nxt = jnp.where(nxt < N_EXPERTS, nxt, -1).astype(jnp.int32)
    ordinal = jnp.cumsum(active.astype(jnp.int32)) - 1
    return pstart, tile_e, tile_nv, nxt[tile_e], (ordinal[tile_e] % 2).astype(jnp.int32)


def _slots(top_e, rank, pstart):
    experts = jnp.arange(N_EXPERTS, dtype=jnp.int32)
    base = jnp.sum(jnp.where(top_e[:, :TOP_K, None] == experts, pstart, 0), axis=-1)
    return (base + rank[:, :TOP_K]).astype(jnp.int32)


def _moe_experts(xs, tile_e, tile_nv, tile_nxt, tile_slot, layer, w1, w2, b1g, b1l, b2):
    tm = MOE_TILE
    n_tiles = xs.shape[0] // (tm * ROW_TILE)
    d_ff = w2.shape[2]
    rows_spec = pl.BlockSpec((tm * ROW_TILE, LANE), lambda i, *_: (i, 0))
    bspec = lambda a: pl.BlockSpec((1, 1) + a.shape[2:], lambda i, te, *_: (layer, te[i], 0, 0))
    hbm = pl.BlockSpec(memory_space=pl.ANY)
    return pl.pallas_call(
        functools.partial(_expert_kernel, layer=layer),
        grid_spec=pltpu.PrefetchScalarGridSpec(
            num_scalar_prefetch=4,
            grid=(n_tiles,),
            in_specs=[rows_spec, hbm, hbm] + [bspec(a) for a in (b1g, b1l, b2)],
            out_specs=rows_spec,
            scratch_shapes=[pltpu.VMEM((2, D_MODEL, 2 * d_ff), F32), pltpu.VMEM((2, d_ff, D_MODEL), F32),
                            pltpu.VMEM((D_MODEL, 2 * d_ff), BF16), pltpu.VMEM((d_ff, D_MODEL), BF16),
                            pltpu.SemaphoreType.DMA((2, 2))]),
        out_shape=jax.ShapeDtypeStruct(xs.shape, F32),
        compiler_params=_params(("arbitrary",)),
        name="moe_experts",
    )(tile_e, tile_nv, tile_nxt, tile_slot, xs, w1, w2, b1g, b1l, b2)


def _combine_kernel(dest_ref, x_ref, tg_ref, g_ref, b_ref, ys_hbm, o_ref, ybuf, sem, *, tm):
    def row_copy(r, k, slot):
        src = ys_hbm.at[pl.ds(pl.multiple_of(slot * ROW_TILE, ROW_TILE), ROW_TILE)]
        dst = ybuf.at[k, pl.ds(pl.multiple_of(r * ROW_TILE, ROW_TILE), ROW_TILE)]
        return pltpu.make_async_copy(src, dst, sem)

    def start(r, c):
        for k in range(TOP_K):
            row_copy(r, k, dest_ref[0, 0, TOP_K * r + k]).start(priority=k % 2)
        return c

    def wait(r, c):
        for k in range(TOP_K):
            row_copy(0, k, 0).wait()
        return c

    lax.fori_loop(0, tm, start, 0, unroll=8)
    lax.fori_loop(0, tm, wait, 0, unroll=8)
    f = tg_ref[:, 0:1] * _load_row_tiles(ybuf, tm, 0)
    for k in range(1, TOP_K):
        f = f + tg_ref[:, k:k + 1] * _load_row_tiles(ybuf, tm, k)
    o_ref[...] = _layernorm(DEEPNORM_ALPHA * _load_row_tiles(x_ref, tm) + f, g_ref[...], b_ref[...])


def _combine(x1, tg, dest, ys, prm, tm):
    t = x1.shape[0] // ROW_TILE
    row = lambda w: pl.BlockSpec((tm, w), lambda i: (i, 0))
    full = lambda a: pl.BlockSpec(a.shape, lambda i: (0,) * a.ndim)
    return pl.pallas_call(
        functools.partial(_combine_kernel, tm=tm),
        grid=(t // tm,),
        in_specs=[pl.BlockSpec((1, 1, TOP_K * tm), lambda i: (i, 0, 0), memory_space=pltpu.SMEM),
                  pl.BlockSpec((tm * ROW_TILE, LANE), lambda i: (i, 0)), row(LANE),
                  full(prm['ln2_g']), full(prm['ln2_b']), pl.BlockSpec(memory_space=pl.ANY)],
        out_specs=row(D_MODEL),
        out_shape=jax.ShapeDtypeStruct((t, D_MODEL), F32),
        scratch_shapes=[pltpu.VMEM((TOP_K, tm * ROW_TILE, LANE), F32), pltpu.SemaphoreType.DMA],
        compiler_params=_params(("arbitrary",)),
        name="combine_ln",
    )(dest.reshape(t // tm, 1, TOP_K * tm), x1, tg, prm['ln2_g'], prm['ln2_b'], ys)


def _dt_pieces(dt_piece):
    return jnp.concatenate([dt_piece[:, 0:8], dt_piece[:, 128:136]], axis=1)


def _layer(xp, xs, prm, lb, rel_bias, st, n_p, seq, n_s, layer, moe, xs_buf):
    tp = n_p * seq
    up = _in_proj(xp, prm['w_in'], 128)
    dt16 = _dt_pieces(up['dt'])
    dt_t = jnp.transpose(dt16.reshape(tp // CHUNK, CHUNK, A_HEADS), (0, 2, 1))
    zeros_conv = jnp.zeros((n_p, SUBLANE, A_CONV_DIM), F32)
    ya, ssm_p = _ssd(up, dt_t, zeros_conv, jnp.zeros((n_p, D_MODEL, A_STATE), F32), prm, n_p, seq, False)
    yb, hg_p = _hgrn(up, jnp.zeros((n_p, D_MODEL, LANE), F32), lb, prm['hgrn_norm_g'], n_p, seq, False)
    attn = [_attn_prompt(up['cq'], up['ck'], up['cv'], _prompt_bias(rel_bias, g), g, n_p, seq)
            for g in range(3)]
    x1p, tep, tgp, rkp, cnt_p = _merge(xp, ya, yb, attn, up['gates'], jnp.zeros((1, LANE), F32), prm, 256)
    xbc3 = up['xbc'].reshape(n_p, seq, A_CONV_DIM)
    conv_p = xbc3[:, seq - (A_CONV - 1):]
    k3 = up['ck'].reshape(n_p, seq, 768)
    v3 = up['cv'].reshape(n_p, seq, 768)

    def last_rows(a, g, w):
        return a[:, seq - min(w, seq):, 256 * g:256 * (g + 1)].reshape(n_p, min(w, seq), 4, 64)

    kv_p = [jnp.stack([last_rows(k3, g, w), last_rows(v3, g, w)], axis=2) for g, (w, _) in enumerate(C_GROUPS)]
    us = _in_proj(xs, prm['w_in'], n_s)
    dt16s = _dt_pieces(us['dt'])
    dt_ts = jnp.zeros((n_s, A_HEADS, CHUNK), F32).at[:, :, 0].set(dt16s)
    conv0 = jnp.pad(st['conv'], ((0, 0), (SUBLANE - (A_CONV - 1), 0), (0, 0)))
    yas, ssm_s = _ssd(us, dt_ts, conv0, st['ssm'].reshape(n_s, D_MODEL, A_STATE), prm, n_s, 1, True)
    ybs, hg_s = _hgrn(us, st['hgrn'].reshape(n_s, D_MODEL, LANE), lb, prm['hgrn_norm_g'], n_s, 1, True)
    bias_d = [_decode_bias(rel_bias, g) for g in range(3)]
    o_s, lse_s = _attn_decode(us['cq'], us['ck'], us['cv'], st['kv_t'], layer,
                              [b[0] for b in bias_d], jnp.stack([b[1] for b in bias_d]))
    attn_s = [(o_s[:, 256 * g:256 * (g + 1)], lse_s[:, 256 * g:256 * (g + 1)]) for g in range(3)]
    x1s, tes, tgs, rks, cnt = _merge(xs, yas, ybs, attn_s, us['gates'], cnt_p, prm, n_s)
    conv_s = jnp.concatenate([st['conv'][:, 1:], us['xbc'][:, None]], axis=1)
    ks4 = us['ck'].reshape(n_s, 1, 3, 4, 64)
    vs4 = us['cv'].reshape(n_s, 1, 3, 4, 64)
    kv_s = [jnp.stack([ks4[:, :, g], vs4[:, :, g]], axis=2) for g in range(3)]
    n_tiles = -(-(tp + n_s) * TOP_K // MOE_TILE) + N_EXPERTS
    pstart, tile_e, tile_nv, tile_nxt, tile_slot = _moe_plan(cnt[0, :N_EXPERTS].astype(jnp.int32), n_tiles)
    dest_p = _slots(tep, rkp, pstart)
    dest_s = _slots(tes, rks, pstart)
    x_sorted = jnp.zeros((n_tiles * MOE_TILE * ROW_TILE, LANE), F32) if xs_buf is None else xs_buf
    x_sorted = _dispatch(x1p, dest_p, x_sorted, 512)
    x_sorted = _dispatch(x1s, dest_s, x_sorted, n_s)
    y_sorted = _moe_experts(x_sorted, tile_e, tile_nv, tile_nxt, tile_slot, layer, *moe)
    yp = _combine(x1p, tgp, dest_p, y_sorted, prm, 512)
    ys = _combine(x1s, tgs, dest_s, y_sorted, prm, n_s)
    states_p = (conv_p, ssm_p.reshape(n_p, A_HEADS, A_HEAD_DIM, A_STATE),
                hg_p.reshape(n_p, B_HEADS, B_KEY_DIM, LANE), kv_p[0], kv_p[1], kv_p[2])
    states_s = (conv_s, ssm_s.reshape(n_s, A_HEADS, A_HEAD_DIM, A_STATE),
                hg_s.reshape(n_s, B_HEADS, B_KEY_DIM, LANE), kv_s[0], kv_s[1], kv_s[2])
    return yp, ys, states_p, states_s, x_sorted


def _prep_layer(l, w_in, conv_w, conv_b, dt_bias, a_log, d_skip, ssm_norm_g, hgrn_norm_g,
                w_branch_a, w_branch_b, w_branch_c, w_out, ln1_g, ln1_b, router_w, router_b,
                moe_w1, moe_b1, moe_w2, moe_b2, ln2_g, ln2_b):
    def lanes_per_group(v):
        return jnp.zeros((1, 256), F32).at[0, 0:8].set(v[:8]).at[0, 128:136].set(v[8:])

    def sublanes_per_group(v):
        return jnp.broadcast_to(v[:, None], (A_HEADS, LANE))

    return {
        'w_in': _pack_w_in(w_in[l]),
        'conv_w8': jnp.pad(conv_w[l], ((0, SUBLANE - A_CONV), (0, 0))),
        'conv_b': conv_b[l][None],
        'dt_bias_l': lanes_per_group(dt_bias[l]), 'a_log_l': lanes_per_group(a_log[l]),
        'dt_bias_t': sublanes_per_group(dt_bias[l]), 'a_log_t': sublanes_per_group(a_log[l]),
        'd_skip_l': jnp.repeat(d_skip[l], A_HEAD_DIM)[None],
        'ssm_norm_g': ssm_norm_g[l][None],
        'hgrn_norm_g': hgrn_norm_g[l][None],
        'w_branch_a': w_branch_a[l].astype(BF16), 'w_branch_b': w_branch_b[l].astype(BF16),
        'w_branch_c': w_branch_c[l].astype(BF16), 'w_out': w_out[l].astype(BF16),
        'ln1_g': ln1_g[l][None], 'ln1_b': ln1_b[l][None],
        'router_w': jnp.pad(router_w[l], ((0, 0), (0, LANE - N_EXPERTS))),
        'router_b': jnp.pad(router_b[l], (0, LANE - N_EXPERTS), constant_values=-jnp.inf)[None],
        'ln2_g': ln2_g[l][None], 'ln2_b': ln2_b[l][None],
    }


def kernel(x_prompt, x_sample, state_conv, state_ssm, state_hgrn, cache_kv_w128, cache_kv_w512, cache_kv_w2048, w_in, conv_w, conv_b, dt_bias, a_log, d_skip, ssm_norm_g, hgrn_lb, hgrn_norm_g, rel_bias, w_branch_a, w_branch_b, w_branch_c, w_out, ln1_g, ln1_b, router_w, router_b, moe_w1, moe_b1, moe_w2, moe_b2, ln2_g, ln2_b):
    n_p, seq, _ = x_prompt.shape
    n_s = x_sample.shape[0]
    depth = w_in.shape[0]
    p_lb = jax.nn.softmax(hgrn_lb.astype(F32), axis=0)
    lower_bounds = jnp.cumsum(p_lb, axis=0) - p_lb[0]
    yp = x_prompt.reshape(n_p * seq, D_MODEL)
    ys = x_sample.reshape(n_s, D_MODEL)
    st_p, st_s = [], []
    xs_buf = None
    kv_t = tuple(jnp.transpose(c, (0, 1, 3, 4, 5, 2)) for c in (cache_kv_w128, cache_kv_w512, cache_kv_w2048))
    moe = (moe_w1, moe_w2, moe_b1[:, :, None, 0::2], moe_b1[:, :, None, 1::2], moe_b2[:, :, None, :])
    for l in range(depth):
        prm = _prep_layer(l, w_in, conv_w, conv_b, dt_bias, a_log, d_skip, ssm_norm_g, hgrn_norm_g,
                          w_branch_a, w_branch_b, w_branch_c, w_out, ln1_g, ln1_b, router_w, router_b,
                          moe_w1, moe_b1, moe_w2, moe_b2, ln2_g, ln2_b)
        st = {'conv': state_conv[l], 'ssm': state_ssm[l], 'hgrn': state_hgrn[l], 'kv_t': kv_t}
        yp, ys, sp, ss, xs_buf = _layer(yp, ys, prm, lower_bounds[l][None], rel_bias, st, n_p, seq, n_s, l, moe,
                                        xs_buf)
        st_p.append(sp)
        st_s.append(ss)
    stack = lambda sts, i: jnp.stack([s[i] for s in sts], axis=0)
    return (yp.reshape(n_p, seq, D_MODEL), ys.reshape(n_s, 1, D_MODEL),
            stack(st_p, 0), stack(st_s, 0), stack(st_p, 1), stack(st_s, 1), stack(st_p, 2), stack(st_s, 2),
            stack(st_p, 3), stack(st_s, 3), stack(st_p, 4), stack(st_s, 4), stack(st_p, 5), stack(st_s, 5))
```

```python
import functools
import math

import jax
import jax.numpy as jnp
import numpy as np
from jax import lax
from jax.experimental import pallas as pl
from jax.experimental.pallas import tpu as pltpu

F32 = jnp.float32
BF16 = jnp.bfloat16
HI = lax.Precision.HIGHEST

D_MODEL = 1024
A_HEADS = 16
A_HEAD_DIM = 64
A_STATE = 128
A_CONV = 4
A_CONV_DIM = 1536
B_HEADS = 8
B_KEY_DIM = 128
C_GROUPS = ((128, 1), (512, 4), (2048, 16))
C_SPAN = 128
C_GROUP_WIDTH = 256
REL_BUCKETS = 32
REL_MAX_DIST = 2048
N_EXPERTS = 32
TOP_K = 4
SWIGLU_ALPHA = 1.702
SWIGLU_LIMIT = 7.0
DEEPNORM_ALPHA = (2.0 * 2) ** 0.25
LN_EPS = 1e-5
RMS_EPS = 1e-5
NEG_BIG = -1e30
LOG_FLOOR = 1e-30

LANE = 128
SUBLANE = 8
CHUNK = 128
SUB = 8
HGRN_HEADS_PER_STEP = 8
ATTN_ILP = 4
VMEM_LIMIT = 56 * 1024 * 1024

IN_PIECES = (('z', 1024), ('xbc', 1536), ('bq', 1024), ('bf', 1024), ('bi', 1024), ('bg', 1024),
             ('cq', 768), ('ck', 768), ('cv', 768), ('gates', 3072), ('dt', 256))
IN_PACKED = sum(w for _, w in IN_PIECES)


def _params(sem):
    return pltpu.CompilerParams(dimension_semantics=sem, vmem_limit_bytes=VMEM_LIMIT)


def _sigmoid(x):
    return 1.0 / (1.0 + jnp.exp(-x))


def _softplus(x):
    return jnp.maximum(x, 0.0) + jnp.log(1.0 + jnp.exp(-jnp.abs(x)))


def _dot(a, b):
    return jnp.dot(a.astype(BF16), b.astype(BF16), preferred_element_type=F32)


def _dot_nt(a, b):
    return lax.dot_general(a.astype(BF16), b.astype(BF16), (((1,), (1,)), ((), ())),
                           preferred_element_type=F32)


def _dot_hi(a, b):
    return jnp.dot(a, b, preferred_element_type=F32, precision=HI)


ROW_TILE = D_MODEL // LANE


def _store_row_tiles(ref, val):
    n = val.shape[0]
    for c in range(ROW_TILE):
        ref[pl.ds(c, n, stride=ROW_TILE), :] = val[:, LANE * c:LANE * (c + 1)]


def _load_row_tiles(ref, n, lead=None):
    rows = pl.ds
    parts = []
    for c in range(ROW_TILE):
        idx = (rows(c, n, stride=ROW_TILE), slice(None))
        parts.append(ref[idx] if lead is None else ref[(lead,) + idx])
    return jnp.concatenate(parts, axis=1)


def _layernorm(h, g, b):
    mu = jnp.mean(h, -1, keepdims=True)
    c = h - mu
    var = jnp.mean(c * c, -1, keepdims=True)
    return c * lax.rsqrt(var + LN_EPS) * g + b


def _in_proj_kernel(x_ref, w_ref, *o_refs):
    xb = x_ref[...].astype(BF16)
    off = 0
    for o_ref in o_refs:
        wd = o_ref.shape[1]
        o_ref[...] = jnp.dot(xb, w_ref[:, off:off + wd], preferred_element_type=F32)
        off += wd


def _in_proj(x, w_packed, tm):
    t = x.shape[0]
    outs = pl.pallas_call(
        _in_proj_kernel,
        grid=(t // tm,),
        in_specs=[pl.BlockSpec((tm, D_MODEL), lambda i: (i, 0)),
                  pl.BlockSpec(memory_space=pltpu.VMEM)],
        out_specs=[pl.BlockSpec((tm, w), lambda i: (i, 0)) for _, w in IN_PIECES],
        out_shape=[jax.ShapeDtypeStruct((t, w), F32) for _, w in IN_PIECES],
        compiler_params=_params(("arbitrary",)),
        name="in_proj",
    )(x, w_packed)
    return {name: o for (name, _), o in zip(IN_PIECES, outs)}


def _pack_w_in(w):
    dt = w[:, 2560:2576]
    dtp = jnp.zeros((D_MODEL, 256), F32).at[:, 0:8].set(dt[:, :8]).at[:, 128:136].set(dt[:, 8:])
    return jnp.concatenate([w[:, :2560], w[:, 2576:], dtp], axis=1).astype(BF16)


def _stage(ref, ci, decode, r):
    if decode:
        row = ref[pl.ds(r, 1), :]
        rows = lax.broadcasted_iota(jnp.int32, (CHUNK, row.shape[1]), 0)
        return jnp.where(rows == 0, jnp.broadcast_to(row, (CHUNK, row.shape[1])), 0.0)
    return ref[pl.ds(pl.multiple_of(ci * CHUNK, CHUNK), CHUNK), :]


def _unstage(ref, val, ci, decode, r):
    if decode:
        ref[pl.ds(r, 1), :] = val[0:1, :]
    else:
        ref[pl.ds(pl.multiple_of(ci * CHUNK, CHUNK), CHUNK), :] = val


def _ssd_kernel(xs_ref, bm_ref, cm_ref, z_ref, dt_ref, dtt_ref,
                c0x_ref, c0b_ref, c0c_ref, h0_ref,
                wx_ref, wb_ref, wc_ref, bx_ref, bb_ref, bc_ref,
                dtb_ref, alog_ref, dtbt_ref, alogt_ref, dsk_ref, ng_ref,
                y_ref, hn_ref,
                h_scr, cx_scr, cb_scr, cc_scr, px_scr, pb_scr, pc_scr, *, nchunks, decode):
    b = pl.program_id(1)
    t = pl.program_id(2)
    r = b % SUBLANE
    q = CHUNK

    @pl.when(t == 0)
    def _():
        h_scr[...] = h0_ref[0]
        cx_scr[...] = c0x_ref[0]
        cb_scr[...] = c0b_ref[0]
        cc_scr[...] = c0c_ref[0]

    rows = lax.broadcasted_iota(jnp.int32, (q, q), 0)
    cols = lax.broadcasted_iota(jnp.int32, (q, q), 1)
    tril = rows >= cols
    tril_f = tril.astype(F32)
    triu_f = (rows <= cols).astype(F32)
    lane = lax.broadcasted_iota(jnp.int32, (q, LANE), 1)
    lo = lane < A_HEAD_DIM
    row_lo = lax.broadcasted_iota(jnp.int32, (LANE, LANE), 0) < A_HEAD_DIM
    valid_col = lax.broadcasted_iota(jnp.int32, (q, LANE), 0) == 0
    valid_row = lax.broadcasted_iota(jnp.int32, (SUBLANE, q), 1) == 0

    def conv(x, carry, pad, w_ref, b_ref):
        pad[0:SUBLANE, :] = carry[...]
        pad[SUBLANE:SUBLANE + q, :] = x
        if not decode:
            carry[...] = pad[q:q + SUBLANE, :]
        acc = b_ref[...]
        for j in range(A_CONV):
            acc = acc + pad[5 + j:5 + j + q, :] * w_ref[j:j + 1, :]
        return acc * _sigmoid(acc)

    def chunk(ci, carry_unused):
        xs = conv(_stage(xs_ref, ci, decode, r), cx_scr, px_scr, wx_ref, bx_ref)
        bm = conv(_stage(bm_ref, ci, decode, r), cb_scr, pb_scr, wb_ref, bb_ref)
        cm = conv(_stage(cm_ref, ci, decode, r), cc_scr, pc_scr, wc_ref, bc_ref)
        a_lane = -jnp.exp(alog_ref[...])
        a_sub = -jnp.exp(alogt_ref[...])
        dt = _softplus(_stage(dt_ref, ci, decode, r) + dtb_ref[...])
        dtt = _softplus(dtt_ref[ci] + dtbt_ref[...])
        if decode:
            dt = jnp.where(valid_col, dt, 0.0)
            dtt = jnp.where(valid_row, dtt, 0.0)
        if decode:
            acum = jnp.broadcast_to((dt * a_lane)[0:1, :], (q, LANE))
            acum_t = jnp.broadcast_to((dtt * a_sub)[:, 0:1], (SUBLANE, q))
        else:
            acum = _dot_hi(tril_f, dt * a_lane)
            acum_t = _dot_hi(dtt * a_sub, triu_f)
        a_last = acum[q - 1:q, :]
        cb = _dot_nt(cm, bm)
        cm_b = cm.astype(BF16)
        bm_b = bm.astype(BF16)
        ys = []
        for j in range(4):
            xp = xs[:, LANE * j:LANE * (j + 1)]
            xp_b = xp.astype(BF16)
            yd = []
            for half in range(2):
                hl = 2 * j + half
                diff = acum[:, hl:hl + 1] - acum_t[hl:hl + 1, :]
                lm = jnp.exp(jnp.where(tril, diff, NEG_BIG))
                m = cb * lm * dtt[hl:hl + 1, :]
                yd.append(jnp.dot(m.astype(BF16), xp_b, preferred_element_type=F32))
            y = jnp.where(lo, yd[0], yd[1])
            hp = h_scr[LANE * j:LANE * (j + 1), :]
            yo = lax.dot_general(cm_b, hp.astype(BF16), (((1,), (1,)), ((), ())),
                                 preferred_element_type=F32)
            e0 = acum[:, 2 * j:2 * j + 1]
            e1 = acum[:, 2 * j + 1:2 * j + 2]
            y = y + yo * jnp.exp(jnp.where(lo, e0, e1))
            w0 = dt[:, 2 * j:2 * j + 1] * jnp.exp(a_last[:, 2 * j:2 * j + 1] - e0)
            w1 = dt[:, 2 * j + 1:2 * j + 2] * jnp.exp(a_last[:, 2 * j + 1:2 * j + 2] - e1)
            xw = xp * jnp.where(lo, w0, w1)
            upd = jnp.dot(xw.T.astype(BF16), bm_b, preferred_element_type=F32)
            dec = jnp.exp(jnp.where(row_lo, a_last[:, 2 * j:2 * j + 1], a_last[:, 2 * j + 1:2 * j + 2]))
            h_scr[LANE * j:LANE * (j + 1), :] = hp * dec + upd
            ys.append(y)
        y = jnp.concatenate(ys, axis=1) + dsk_ref[...] * xs
        z = _stage(z_ref, ci, decode, r)
        y = y * (z * _sigmoid(z))
        y = y * lax.rsqrt(jnp.mean(y * y, -1, keepdims=True) + RMS_EPS) * ng_ref[...]
        _unstage(y_ref, y, ci, decode, r)
        return carry_unused

    lax.fori_loop(0, nchunks, chunk, 0)

    @pl.when(t == pl.num_programs(2) - 1)
    def _():
        hn_ref[0] = h_scr[...]


def _ssd(u, dt_t, conv0_8, h0, prm, n, seq, decode):
    tb = CHUNK if decode else min(512, seq)
    nb = 1 if decode else seq // tb
    nchunks = tb // CHUNK
    rb = SUBLANE if decode else tb

    def rowmap(lane_block):
        if decode:
            return lambda g, b, t: (b // SUBLANE, lane_block(g))
        return lambda g, b, t: (b * nb + t, lane_block(g))

    chunk_map = (lambda g, b, t: (b, g, 0)) if decode else (lambda g, b, t: (b * nb + t, g, 0))
    cw = (512, LANE, LANE)
    lane_blocks = (lambda g: g, lambda g: 8 + g, lambda g: 10 + g)
    in_specs = [
        pl.BlockSpec((rb, 512), rowmap(lane_blocks[0])),
        pl.BlockSpec((rb, LANE), rowmap(lane_blocks[1])),
        pl.BlockSpec((rb, LANE), rowmap(lane_blocks[2])),
        pl.BlockSpec((rb, 512), rowmap(lambda g: g)),
        pl.BlockSpec((rb, LANE), rowmap(lambda g: g)),
        pl.BlockSpec((nchunks, SUBLANE, CHUNK), chunk_map),
    ]
    in_specs += [pl.BlockSpec((1, SUBLANE, w), (lambda lb: (lambda g, b, t: (b, 0, lb(g))))(lb))
                 for w, lb in zip(cw, lane_blocks)]
    in_specs += [pl.BlockSpec((1, 512, A_STATE), lambda g, b, t: (b, g, 0))]
    in_specs += [pl.BlockSpec((SUBLANE, w), (lambda lb: (lambda g, b, t: (0, lb(g))))(lb))
                 for w, lb in zip(cw, lane_blocks)]
    in_specs += [pl.BlockSpec((1, w), (lambda lb: (lambda g, b, t: (0, lb(g))))(lb))
                 for w, lb in zip(cw, lane_blocks)]
    in_specs += [
        pl.BlockSpec((1, LANE), lambda g, b, t: (0, g)),
        pl.BlockSpec((1, LANE), lambda g, b, t: (0, g)),
        pl.BlockSpec((SUBLANE, LANE), lambda g, b, t: (g, 0)),
        pl.BlockSpec((SUBLANE, LANE), lambda g, b, t: (g, 0)),
        pl.BlockSpec((1, 512), lambda g, b, t: (0, g)),
        pl.BlockSpec((1, 512), lambda g, b, t: (0, g)),
    ]
    rows_total = u['z'].shape[0]
    y, hn = pl.pallas_call(
        functools.partial(_ssd_kernel, nchunks=nchunks, decode=decode),
        grid=(2, n, nb),
        in_specs=in_specs,
        out_specs=[pl.BlockSpec((rb, 512), rowmap(lambda g: g)),
                   pl.BlockSpec((1, 512, A_STATE), lambda g, b, t: (b, g, 0))],
        out_shape=[jax.ShapeDtypeStruct((rows_total, D_MODEL), F32),
                   jax.ShapeDtypeStruct((n, D_MODEL, A_STATE), F32)],
        scratch_shapes=[pltpu.VMEM((512, A_STATE), F32),
                        pltpu.VMEM((SUBLANE, 512), F32), pltpu.VMEM((SUBLANE, LANE), F32),
                        pltpu.VMEM((SUBLANE, LANE), F32),
                        pltpu.VMEM((CHUNK + SUBLANE, 512), F32), pltpu.VMEM((CHUNK + SUBLANE, LANE), F32),
                        pltpu.VMEM((CHUNK + SUBLANE, LANE), F32)],
        compiler_params=_params(("arbitrary", "arbitrary", "arbitrary")),
        name="ssd_decode" if decode else "ssd_prompt",
    )(u['xbc'], u['xbc'], u['xbc'], u['z'], u['dt'], dt_t,
      conv0_8, conv0_8, conv0_8, h0,
      prm['conv_w8'], prm['conv_w8'], prm['conv_w8'], prm['conv_b'], prm['conv_b'], prm['conv_b'],
      prm['dt_bias_l'], prm['a_log_l'], prm['dt_bias_t'], prm['a_log_t'], prm['d_skip_l'], prm['ssm_norm_g'])
    return y, hn


def _hgrn_kernel(q_ref, f_ref, i_ref, g_ref, lb_ref, ng_ref, s0_ref, y_ref, sn_ref, st_scr,
                 *, nchunks, decode, nheads):
    b = pl.program_id(1)
    t = pl.program_id(2)
    r = b % SUBLANE
    c = CHUNK

    @pl.when(t == 0)
    def _():
        for hh in range(nheads):
            st_scr[hh] = s0_ref[0, LANE * hh:LANE * (hh + 1), :].T

    rows = lax.broadcasted_iota(jnp.int32, (c, c), 0)
    cols = lax.broadcasted_iota(jnp.int32, (c, c), 1)
    tril_f = (rows >= cols).astype(F32)
    row_c = lax.broadcasted_iota(jnp.int32, (c, LANE), 0)
    row_s = lax.broadcasted_iota(jnp.int32, (SUB, LANE), 0)
    blk_xor = (rows // SUB) ^ (cols // SUB)
    level = jnp.where(blk_xor >= 8, 3, jnp.where(blk_xor >= 4, 2, jnp.where(blk_xor >= 2, 1, 0)))
    level = jnp.where((rows // SUB) > (cols // SUB), level, -1)
    nlevels = 4
    assert SUB << nlevels == c

    def one_head(hh, ci):
        ls = slice(LANE * hh, LANE * (hh + 1))
        lbv = lb_ref[:, ls]
        f = _stage(f_ref, ci, decode, r)[:, ls]
        qr = _stage(q_ref, ci, decode, r)[:, ls]
        v = _stage(i_ref, ci, decode, r)[:, ls]
        gr = _stage(g_ref, ci, decode, r)[:, ls]
        q = qr * _sigmoid(qr) * (B_KEY_DIM ** -0.5)
        e_f = jnp.exp(-jnp.abs(f))
        r_f = 1.0 / (1.0 + e_f)
        pos = f >= 0.0
        sig_p = jnp.where(pos, r_f, e_f * r_f)
        sig_n = jnp.where(pos, e_f * r_f, r_f)
        logf = jnp.log(jnp.maximum(lbv + (1.0 - lbv) * sig_p, LOG_FLOOR))
        k = (1.0 - lbv) * sig_n
        if decode:
            logf = jnp.where(row_c == 0, logf, 0.0)
            k = jnp.where(row_c == 0, k, 0.0)
        if decode:
            bcum = jnp.broadcast_to(logf[0:1, :], (c, LANE))
        else:
            bcum = _dot_hi(tril_f, logf)
        b_last = bcum[c - 1:c, :]
        st = st_scr[hh]
        o = _dot_nt(q * jnp.exp(bcum), st)
        if not decode:
            att = jnp.zeros((c, c), F32)
            for lv in range(nlevels):
                h = SUB << lv
                ref = jnp.concatenate(
                    [jnp.broadcast_to(bcum[2 * h * m + h - 1:2 * h * m + h, :], (2 * h, LANE))
                     for m in range(c // (2 * h))], axis=0)
                x = jnp.exp(-jnp.abs(bcum - ref))
                att = jnp.where(level == lv, _dot_nt(q * x, k * x), att)
            o = o + _dot(att, v)
        diag = []
        for i in range(c // SUB):
            if decode and i > 0:
                diag.append(jnp.zeros((SUB, LANE), F32))
                continue
            sl = slice(SUB * i, SUB * (i + 1))
            qb, kb, vb, bb = q[sl, :], k[sl, :], v[sl, :], bcum[sl, :]
            od = jnp.zeros((SUB, LANE), F32)
            for s in range(1 if decode else SUB):
                e = jnp.exp(jnp.where(row_s >= s, bb - bb[s:s + 1, :], NEG_BIG))
                rs = jnp.sum(qb * kb[s:s + 1, :] * e, -1, keepdims=True)
                od = od + rs * vb[s:s + 1, :]
            diag.append(od)
        o = o + jnp.concatenate(diag, axis=0)
        k2 = k * jnp.exp(b_last - bcum)
        st_scr[hh] = jnp.exp(b_last) * st + jnp.dot(v.T.astype(BF16), k2.astype(BF16),
                                                     preferred_element_type=F32)
        gate = gr * _sigmoid(gr)
        return o * lax.rsqrt(jnp.mean(o * o, -1, keepdims=True) + RMS_EPS) * ng_ref[...] * gate

    def chunk(ci, carry_unused):
        y = [one_head(hh, ci) for hh in range(nheads)]
        _unstage(y_ref, y[0] if nheads == 1 else jnp.concatenate(y, axis=1), ci, decode, r)
        return carry_unused

    lax.fori_loop(0, nchunks, chunk, 0)

    @pl.when(t == pl.num_programs(2) - 1)
    def _():
        for hh in range(nheads):
            sn_ref[0, LANE * hh:LANE * (hh + 1), :] = st_scr[hh].T


def _hgrn(u, s0, lb, ng, n, seq, decode):
    tb = CHUNK if decode else min(512, seq)
    nb = 1 if decode else seq // tb
    nchunks = tb // CHUNK
    rb = SUBLANE if decode else tb
    rowmap = (lambda h, b, t: (b // SUBLANE, h)) if decode else (lambda h, b, t: (b * nb + t, h))
    rows_total = u['bq'].shape[0]
    nheads = B_HEADS if decode else HGRN_HEADS_PER_STEP
    wd = LANE * nheads
    y, sn = pl.pallas_call(
        functools.partial(_hgrn_kernel, nchunks=nchunks, decode=decode, nheads=nheads),
        grid=(B_HEADS // nheads, n, nb),
        in_specs=[pl.BlockSpec((rb, wd), rowmap)] * 4 + [
            pl.BlockSpec((1, wd), lambda h, b, t: (0, h)),
            pl.BlockSpec((1, LANE), lambda h, b, t: (0, 0)),
            pl.BlockSpec((1, B_KEY_DIM * nheads, LANE), lambda h, b, t: (b, h, 0))],
        out_specs=[pl.BlockSpec((rb, wd), rowmap),
                   pl.BlockSpec((1, B_KEY_DIM * nheads, LANE), lambda h, b, t: (b, h, 0))],
        out_shape=[jax.ShapeDtypeStruct((rows_total, D_MODEL), F32),
                   jax.ShapeDtypeStruct((n, D_MODEL, LANE), F32)],
        scratch_shapes=[pltpu.VMEM((nheads, LANE, B_KEY_DIM), F32)],
        compiler_params=_params(("arbitrary", "arbitrary", "arbitrary")),
        name="hgrn_decode" if decode else "hgrn_prompt",
    )(u['bq'], u['bf'], u['bi'], u['bg'], lb, ng, s0)
    return y, sn


def _t5_bucket_np(dist):
    exact = REL_BUCKETS // 2
    d = np.maximum(dist, 1).astype(np.float32)
    large = exact + (np.log(d / np.float32(exact)) / np.float32(math.log(REL_MAX_DIST / exact))
                     * np.float32(REL_BUCKETS - exact)).astype(np.int32)
    large = np.clip(large, 0, REL_BUCKETS - 1)
    return np.where(dist < exact, dist, large)


def _attn_prompt_kernel(q_ref, kc_ref, kp_ref, vc_ref, vp_ref, bias_ref, o_ref, lse_ref, *, dil):
    first = pl.program_id(1) == 0
    s_q = C_SPAN
    lane = lax.broadcasted_iota(jnp.int32, (s_q, LANE), 1)
    lo = lane < 64
    kcol = lax.broadcasted_iota(jnp.int32, (s_q, 2 * s_q), 1)
    no_prev = jnp.logical_and(first, kcol < s_q)

    pair = pl.program_id(2)

    def solve(qp, kprev, kcur, vprev, vcur, mask_prev):
        kp2 = jnp.concatenate([kprev, kcur], axis=0).astype(BF16)
        vp2 = jnp.concatenate([vprev, vcur], axis=0).astype(BF16)
        oh, lh = [], []
        for half in range(2):
            qm = jnp.where(lo if half == 0 else jnp.logical_not(lo), qp, 0.0).astype(BF16)
            s = lax.dot_general(qm, kp2, (((1,), (1,)), ((), ())), preferred_element_type=F32)
            s = s * (64 ** -0.5) + bias_ref[2 * pair + half]
            if mask_prev:
                s = jnp.where(no_prev, NEG_BIG, s)
            m = jnp.max(s, -1, keepdims=True)
            p = jnp.exp(s - m)
            den = jnp.sum(p, -1, keepdims=True)
            oh.append(jnp.dot((p / den).astype(BF16), vp2, preferred_element_type=F32))
            lh.append(m + jnp.log(den))
        return jnp.where(lo, oh[0], oh[1]), jnp.where(lo, lh[0], lh[1])

    if dil == 1:
        for sb in range(ATTN_ILP):
            cur = slice(s_q * sb, s_q * (sb + 1))
            if sb == 0:
                o, lse = solve(q_ref[cur, :], kp_ref[...], kc_ref[cur, :], vp_ref[...], vc_ref[cur, :], True)
            else:
                prv = slice(s_q * (sb - 1), s_q * sb)
                o, lse = solve(q_ref[cur, :], kc_ref[prv, :], kc_ref[cur, :], vc_ref[prv, :], vc_ref[cur, :],
                               False)
            o_ref[cur, :] = o
            lse_ref[cur, :] = lse
    else:
        def residues(it, carry):
            for jj in range(ATTN_ILP):
                sl = pl.ds(it * ATTN_ILP + jj, s_q, stride=dil)
                o, lse = solve(q_ref[sl, :], kp_ref[sl, :], kc_ref[sl, :], vp_ref[sl, :], vc_ref[sl, :], True)
                o_ref[sl, :] = o
                lse_ref[sl, :] = lse
            return carry

        lax.fori_loop(0, dil // ATTN_ILP, residues, 0)


def _attn_prompt(cq, ck, cv, bias, g, n, seq):
    dil = C_GROUPS[g][1]
    look = C_SPAN * dil
    rows_blk = look * (ATTN_ILP if dil == 1 else 1)
    nbk = seq // rows_blk
    per = rows_blk // look
    cur = lambda b, i, j: (b * nbk + i, 2 * g + j)
    prev = lambda b, i, j: (jnp.maximum((b * nbk + i) * per - 1, 0), 2 * g + j)
    blk = (rows_blk, LANE)
    pblk = (look, LANE)
    return pl.pallas_call(
        functools.partial(_attn_prompt_kernel, dil=dil),
        grid=(n, nbk, 2),
        in_specs=[pl.BlockSpec(blk, cur), pl.BlockSpec(blk, cur), pl.BlockSpec(pblk, prev),
                  pl.BlockSpec(blk, cur), pl.BlockSpec(pblk, prev),
                  pl.BlockSpec((4, C_SPAN, 2 * C_SPAN), lambda b, i, j: (0, 0, 0))],
        out_specs=[pl.BlockSpec(blk, lambda b, i, j: (b * nbk + i, j))] * 2,
        out_shape=[jax.ShapeDtypeStruct((n * seq, C_GROUP_WIDTH), F32)] * 2,
        compiler_params=_params(("arbitrary", "arbitrary", "arbitrary")),
        name=f"attn_prompt_g{g}",
    )(cq, ck, ck, cv, cv, bias)


def _prompt_bias(rel_bias, g):
    dil = C_GROUPS[g][1]
    qi = np.arange(C_SPAN)[:, None]
    kj = np.arange(2 * C_SPAN)[None, :]
    rel = qi + C_SPAN - kj
    band = (rel >= 0) & (rel <= C_SPAN)
    idx = _t5_bucket_np(np.maximum(rel, 0) * dil)
    tab = rel_bias[:, 4 * g:4 * g + 4]
    hit = jnp.asarray(idx)[None, :, :, None] == jnp.arange(REL_BUCKETS)
    vals = jnp.sum(jnp.where(hit, tab.T[:, None, None, :], 0.0), axis=-1)
    return jnp.where(band[None], vals, NEG_BIG)


def _decode_bias(rel_bias, g):
    win, dil = C_GROUPS[g]
    tab = rel_bias[:, 4 * g:4 * g + 4]
    pos = np.arange(win)
    vals = jnp.where((pos % dil == 0)[:, None], tab[_t5_bucket_np(win - pos)], NEG_BIG)
    bias_buf = jnp.zeros((2, SUBLANE, win), F32).at[:, 0:2].set(vals.T.reshape(2, 2, win))
    new = jnp.broadcast_to(tab[0].reshape(2, 2, 1), (2, 2, LANE))
    bias_new = jnp.zeros((2, SUBLANE, LANE), F32).at[:, 0:2].set(new)
    return bias_buf, bias_new


def _attn_decode_kernel(q_ref, k_ref, v_ref, c0_ref, c1_ref, c2_ref, bb0_ref, bb1_ref, bb2_ref, bn_ref,
                        o_ref, lse_ref):
    b = pl.program_id(0)
    row8 = lax.broadcasted_iota(jnp.int32, (SUBLANE, LANE), 0)
    lane8 = lax.broadcasted_iota(jnp.int32, (SUBLANE, LANE), 1)
    qmask = jnp.logical_or(jnp.logical_and(row8 == 0, lane8 < 64), jnp.logical_and(row8 == 1, lane8 >= 64))
    lo1 = lax.broadcasted_iota(jnp.int32, (1, LANE), 1) < 64
    q_all = q_ref[pl.ds(b, 1), :]
    k_all = k_ref[pl.ds(b, 1), :]
    v_all = v_ref[pl.ds(b, 1), :]
    o_parts, lse_parts = [], []
    for g, (buf_ref, bb_ref) in enumerate(((c0_ref, bb0_ref), (c1_ref, bb1_ref), (c2_ref, bb2_ref))):
        win = C_GROUPS[g][0]
        for j in range(2):
            c0 = C_GROUP_WIDTH * g + LANE * j
            qrow = q_all[:, c0:c0 + LANE]
            knew = k_all[:, c0:c0 + LANE]
            vnew = v_all[:, c0:c0 + LANE]
            q8 = jnp.where(qmask, jnp.broadcast_to(qrow, (SUBLANE, LANE)), 0.0)
            kt = buf_ref[0, 0, 0, 2 * j:2 * j + 2].reshape(LANE, win)
            vt = buf_ref[0, 0, 1, 2 * j:2 * j + 2].reshape(LANE, win)
            s = _dot(q8, kt) * (64 ** -0.5) + bb_ref[j]
            snew = jnp.sum(q8 * knew, -1, keepdims=True) * (64 ** -0.5) + bn_ref[g, j][:, 0:1]
            m = jnp.maximum(jnp.max(s, -1, keepdims=True), snew)
            p = jnp.exp(s - m)
            pn = jnp.exp(snew - m)
            den = jnp.sum(p, -1, keepdims=True) + pn
            o8 = _dot_nt(p / den, vt) + (pn / den) * vnew
            lse8 = jnp.broadcast_to(m + jnp.log(den), (SUBLANE, LANE))
            o_parts.append(jnp.where(lo1, o8[0:1, :], o8[1:2, :]))
            lse_parts.append(jnp.where(lo1, lse8[0:1, :], lse8[1:2, :]))
    o_ref[pl.ds(b, 1), :] = jnp.concatenate(o_parts, axis=1)
    lse_ref[pl.ds(b, 1), :] = jnp.concatenate(lse_parts, axis=1)


def _attn_decode(cq, ck, cv, caches_t, layer, bias_bufs, bias_new):
    n = cq.shape[0]
    full = lambda a: pl.BlockSpec(a.shape, lambda b: (0,) * a.ndim)
    cache_spec = lambda c: pl.BlockSpec((1, 1) + c.shape[2:], lambda b: (layer, b, 0, 0, 0, 0))
    return pl.pallas_call(
        _attn_decode_kernel,
        grid=(n,),
        in_specs=[full(cq), full(ck), full(cv)] + [cache_spec(c) for c in caches_t]
        + [full(bb) for bb in bias_bufs] + [full(bias_new)],
        out_specs=[pl.BlockSpec((n, 768), lambda b: (0, 0))] * 2,
        out_shape=[jax.ShapeDtypeStruct((n, 768), F32)] * 2,
        compiler_params=_params(("arbitrary",)),
        name="attn_decode",
    )(cq, ck, cv, *caches_t, *bias_bufs, bias_new)


def _merge_kernel(x_ref, ya_ref, yb_ref, o0_ref, o1_ref, o2_ref, l0_ref, l1_ref, l2_ref, gt_ref,
                  wa_ref, wb_ref, wc_ref, wo_ref, g_ref, b_ref, rw_ref, rb_ref, cnt0_ref,
                  x1_ref, te_ref, tg_ref, rk_ref, cnt_ref, cnt_scr):
    l0, l1, l2 = l0_ref[...], l1_ref[...], l2_ref[...]
    lm = jnp.maximum(jnp.maximum(l0, l1), l2)
    e0, e1, e2 = jnp.exp(l0 - lm), jnp.exp(l1 - lm), jnp.exp(l2 - lm)
    den = e0 + e1 + e2
    yc = (e0 / den) * o0_ref[...] + (e1 / den) * o1_ref[...] + (e2 / den) * o2_ref[...]
    ga = _sigmoid(gt_ref[:, 0:D_MODEL])
    gb = _sigmoid(gt_ref[:, D_MODEL:2 * D_MODEL])
    gc = _sigmoid(gt_ref[:, 2 * D_MODEL:3 * D_MODEL])
    merged = (ga * _dot(ya_ref[...], wa_ref[...]) + gb * _dot(yb_ref[...], wb_ref[...])
              + gc * _dot(yc, wc_ref[...]))
    h = DEEPNORM_ALPHA * x_ref[...] + _dot(merged, wo_ref[...])
    x1 = _layernorm(h, g_ref[...], b_ref[...])
    _store_row_tiles(x1_ref, x1)
    logits = _dot(x1, rw_ref[...]) + rb_ref[...]
    lane = lax.broadcasted_iota(jnp.int32, logits.shape, 1)
    lane_f = lane.astype(F32)
    te = jnp.zeros(logits.shape, F32)
    vals, onehots = [], []
    for k in range(TOP_K):
        m = jnp.max(logits, -1, keepdims=True)
        idx = jnp.min(jnp.where(logits == m, lane_f, float(LANE)), -1, keepdims=True)
        te = jnp.where(lane == k, idx, te)
        vals.append(m)
        hit = lane_f == idx
        onehots.append(hit.astype(F32))
        logits = jnp.where(hit, -jnp.inf, logits)
    ex = [jnp.exp(v - vals[0]) for v in vals]
    tot = ex[0] + ex[1] + ex[2] + ex[3]
    tg = jnp.zeros(logits.shape, F32)
    for k in range(TOP_K):
        tg = jnp.where(lane == k, ex[k] / tot, tg)
    te_ref[...] = te.astype(jnp.int32)
    tg_ref[...] = tg
    @pl.when(pl.program_id(0) == 0)
    def _():
        cnt_scr[...] = cnt0_ref[...]

    tm = logits.shape[0]
    oh = onehots[0] + onehots[1] + onehots[2] + onehots[3]
    earlier = (lax.broadcasted_iota(jnp.int32, (tm, tm), 0)
               > lax.broadcasted_iota(jnp.int32, (tm, tm), 1)).astype(BF16)
    before = jnp.dot(earlier, oh.astype(BF16), preferred_element_type=F32) + cnt_scr[...]
    rank = jnp.zeros(logits.shape, F32)
    for k in range(TOP_K):
        rank = jnp.where(lane == k, jnp.sum(onehots[k] * before, -1, keepdims=True), rank)
    rk_ref[...] = rank.astype(jnp.int32)
    cnt_scr[...] = cnt_scr[...] + jnp.sum(oh, 0, keepdims=True)
    cnt_ref[...] = cnt_scr[...]


def _merge(x, ya, yb, attn, gates, cnt0, prm, tm):
    t = x.shape[0]
    row = lambda w: pl.BlockSpec((tm, w), lambda i: (i, 0))
    full = lambda a: pl.BlockSpec(a.shape, lambda i: (0,) * a.ndim)
    ws = [prm['w_branch_a'], prm['w_branch_b'], prm['w_branch_c'], prm['w_out'],
          prm['ln1_g'], prm['ln1_b'], prm['router_w'], prm['router_b'], cnt0]
    (o0, l0), (o1, l1), (o2, l2) = attn
    return pl.pallas_call(
        _merge_kernel,
        grid=(t // tm,),
        in_specs=[row(D_MODEL)] * 3 + [row(C_GROUP_WIDTH)] * 6 + [row(3 * D_MODEL)] + [full(a) for a in ws],
        out_specs=[pl.BlockSpec((tm * ROW_TILE, LANE), lambda i: (i, 0)), row(LANE), row(LANE), row(LANE),
                   pl.BlockSpec((1, LANE), lambda i: (0, 0))],
        out_shape=[jax.ShapeDtypeStruct((t * ROW_TILE, LANE), F32), jax.ShapeDtypeStruct((t, LANE), jnp.int32),
                   jax.ShapeDtypeStruct((t, LANE), F32), jax.ShapeDtypeStruct((t, LANE), jnp.int32),
                   jax.ShapeDtypeStruct((1, LANE), F32)],
        scratch_shapes=[pltpu.VMEM((1, LANE), F32)],
        compiler_params=_params(("arbitrary",)),
        name="merge_ln_router",
    )(x, ya, yb, o0, o1, o2, l0, l1, l2, gates, *ws)


MOE_TILE = 512


def _dispatch_kernel(dest_ref, x_ref, xs_in_hbm, xs_hbm, sem, *, tm):
    del xs_in_hbm

    def row_copy(r, slot):
        src = x_ref.at[pl.ds(pl.multiple_of(r * ROW_TILE, ROW_TILE), ROW_TILE)]
        dst = xs_hbm.at[pl.ds(pl.multiple_of(slot * ROW_TILE, ROW_TILE), ROW_TILE)]
        return pltpu.make_async_copy(src, dst, sem)

    def start(r, c):
        for k in range(TOP_K):
            row_copy(r, dest_ref[0, 0, TOP_K * r + k]).start(priority=k % 2)
        return c

    def wait(r, c):
        for _ in range(TOP_K):
            row_copy(0, 0).wait()
        return c

    lax.fori_loop(0, tm, start, 0, unroll=8)
    lax.fori_loop(0, tm, wait, 0, unroll=8)


def _dispatch(x1, dest, xs, tm):
    t = x1.shape[0] // ROW_TILE
    return pl.pallas_call(
        functools.partial(_dispatch_kernel, tm=tm),
        grid=(t // tm,),
        in_specs=[pl.BlockSpec((1, 1, TOP_K * tm), lambda i: (i, 0, 0), memory_space=pltpu.SMEM),
                  pl.BlockSpec((tm * ROW_TILE, LANE), lambda i: (i, 0)),
                  pl.BlockSpec(memory_space=pl.ANY)],
        out_specs=pl.BlockSpec(memory_space=pl.ANY),
        out_shape=jax.ShapeDtypeStruct(xs.shape, F32),
        scratch_shapes=[pltpu.SemaphoreType.DMA],
        input_output_aliases={2: 0},
        compiler_params=_params(("arbitrary",)),
        name="moe_dispatch",
    )(dest.reshape(t // tm, 1, TOP_K * tm), x1, xs)


def _expert_kernel(te_ref, nv_ref, nx_ref, sl_ref, x_ref, w1_hbm, w2_hbm, b1g_ref, b1l_ref, b2_ref, y_ref,
                   w1_buf, w2_buf, w1p_scr, w2b_scr, sem, *, layer):
    i = pl.program_id(0)
    nv = nv_ref[i]
    changed = jnp.logical_or(i == 0, te_ref[i] != te_ref[jnp.maximum(i - 1, 0)])
    half = LANE
    blk = 2 * LANE
    d_ff = w2b_scr.shape[0]

    def fetch(e, s):
        return (pltpu.make_async_copy(w1_hbm.at[layer, e], w1_buf.at[s], sem.at[0, s]),
                pltpu.make_async_copy(w2_hbm.at[layer, e], w2_buf.at[s], sem.at[1, s]))

    @pl.when(i == 0)
    def _():
        for d in fetch(te_ref[0], sl_ref[0]):
            d.start()

    @pl.when(jnp.logical_and(nv > 0, changed))
    def _():
        s = sl_ref[i]
        for d in fetch(te_ref[i], s):
            d.wait()
        nx = nx_ref[i]

        @pl.when(nx >= 0)
        def _():
            for d in fetch(nx, 1 - s):
                d.start()

        src_r = lax.broadcasted_iota(jnp.int32, (blk, blk), 0)
        dst_c = lax.broadcasted_iota(jnp.int32, (blk, blk), 1)
        pick = jnp.where(dst_c < half, 2 * dst_c, 2 * (dst_c - half) + 1)
        perm = (src_r == pick).astype(BF16)
        for c in range(2 * d_ff // blk):
            wb = w1_buf[s, :, blk * c:blk * (c + 1)].astype(BF16)
            w1p_scr[:, blk * c:blk * (c + 1)] = jnp.dot(wb, perm, preferred_element_type=F32).astype(BF16)
        w2b_scr[...] = w2_buf[s].astype(BF16)

    tm = x_ref.shape[0] // ROW_TILE

    def mlp(rows):
        x = _load_row_tiles(x_ref, rows).astype(BF16)
        u = jnp.dot(x, w1p_scr[...], preferred_element_type=F32)
        nblk = 2 * d_ff // blk
        ug = jnp.concatenate([u[:, blk * c:blk * c + half] for c in range(nblk)], axis=1) + b1g_ref[0, 0]
        ul = jnp.concatenate([u[:, blk * c + half:blk * (c + 1)] for c in range(nblk)], axis=1) + b1l_ref[0, 0]
        glu = jnp.minimum(ug, SWIGLU_LIMIT)
        lin = jnp.clip(ul, -SWIGLU_LIMIT, SWIGLU_LIMIT)
        act = glu * _sigmoid(SWIGLU_ALPHA * glu) * (lin + 1.0)
        y = jnp.dot(act.astype(BF16), w2b_scr[...], preferred_element_type=F32) + b2_ref[0, 0]
        _store_row_tiles(y_ref, y)
        if rows < tm:
            y_ref[pl.ds(rows * ROW_TILE, (tm - rows) * ROW_TILE), :] = jnp.zeros(((tm - rows) * ROW_TILE, LANE), F32)

    @pl.when(nv > tm // 2)
    def _():
        mlp(tm)

    @pl.when(jnp.logical_and(nv > 0, nv <= tm // 2))
    def _():
        mlp(tm // 2)

    @pl.when(nv == 0)
    def _():
        y_ref[...] = jnp.zeros(y_ref.shape, F32)


def _moe_plan(counts, n_tiles):
    tm = MOE_TILE
    padded = (counts + tm - 1) // tm * tm
    pend = jnp.cumsum(padded)
    pstart = pend - padded
    tile_row0 = jnp.arange(n_tiles, dtype=jnp.int32) * tm
    tile_e = jnp.minimum(jnp.sum(tile_row0[:, None] >= pend[None, :], axis=1), N_EXPERTS - 1).astype(jnp.int32)
    tile_nv = jnp.clip(pstart[tile_e] + counts[tile_e] - tile_row0, 0, tm)
    tile_nv = jnp.where(tile_row0 < pend[-1], tile_nv, 0).astype(jnp.int32)
    experts = jnp.arange(N_EXPERTS, dtype=jnp.int32)
    active = counts > 0
    later = jnp.logical_and(active[None, :], experts[None, :] > experts[:, None])
    nxt = jnp.min(jnp.where(later, experts[None, :], N_EXPERTS), axis=1)
    nxt = jnp.where(nxt < N_EXPERTS, nxt, -1).astype(jnp.int32)
    ordinal = jnp.cumsum(active.astype(jnp.int32)) - 1
    return pstart, tile_e, tile_nv, nxt[tile_e], (ordinal[tile_e] % 2).astype(jnp.int32)


def _slots(top_e, rank, pstart):
    experts = jnp.arange(N_EXPERTS, dtype=jnp.int32)
    base = jnp.sum(jnp.where(top_e[:, :TOP_K, None] == experts, pstart, 0), axis=-1)
    return (base + rank[:, :TOP_K]).astype(jnp.int32)


def _moe_experts(xs, tile_e, tile_nv, tile_nxt, tile_slot, layer, w1, w2, b1g, b1l, b2):
    tm = MOE_TILE
    n_tiles = xs.shape[0] // (tm * ROW_TILE)
    d_ff = w2.shape[2]
    rows_spec = pl.BlockSpec((tm * ROW_TILE, LANE), lambda i, *_: (i, 0))
    bspec = lambda a: pl.BlockSpec((1, 1) + a.shape[2:], lambda i, te, *_: (layer, te[i], 0, 0))
    hbm = pl.BlockSpec(memory_space=pl.ANY)
    return pl.pallas_call(
        functools.partial(_expert_kernel, layer=layer),
        grid_spec=pltpu.PrefetchScalarGridSpec(
            num_scalar_prefetch=4,
            grid=(n_tiles,),
            in_specs=[rows_spec, hbm, hbm] + [bspec(a) for a in (b1g, b1l, b2)],
            out_specs=rows_spec,
            scratch_shapes=[pltpu.VMEM((2, D_MODEL, 2 * d_ff), F32), pltpu.VMEM((2, d_ff, D_MODEL), F32),
                            pltpu.VMEM((D_MODEL, 2 * d_ff), BF16), pltpu.VMEM((d_ff, D_MODEL), BF16),
                            pltpu.SemaphoreType.DMA((2, 2))]),
        out_shape=jax.ShapeDtypeStruct(xs.shape, F32),
        compiler_params=_params(("arbitrary",)),
        name="moe_experts",
    )(tile_e, tile_nv, tile_nxt, tile_slot, xs, w1, w2, b1g, b1l, b2)


def _combine_kernel(dest_ref, x_ref, tg_ref, g_ref, b_ref, ys_hbm, o_ref, ybuf, sem, *, tm):
    def row_copy(r, k, slot):
        src = ys_hbm.at[pl.ds(pl.multiple_of(slot * ROW_TILE, ROW_TILE), ROW_TILE)]
        dst = ybuf.at[k, pl.ds(pl.multiple_of(r * ROW_TILE, ROW_TILE), ROW_TILE)]
        return pltpu.make_async_copy(src, dst, sem)

    def start(r, c):
        for k in range(TOP_K):
            row_copy(r, k, dest_ref[0, 0, TOP_K * r + k]).start(priority=k % 2)
        return c

    def wait(r, c):
        for k in range(TOP_K):
            row_copy(0, k, 0).wait()
        return c

    lax.fori_loop(0, tm, start, 0, unroll=8)
    lax.fori_loop(0, tm, wait, 0, unroll=8)
    f = tg_ref[:, 0:1] * _load_row_tiles(ybuf, tm, 0)
    for k in range(1, TOP_K):
        f = f + tg_ref[:, k:k + 1] * _load_row_tiles(ybuf, tm, k)
    o_ref[...] = _layernorm(DEEPNORM_ALPHA * _load_row_tiles(x_ref, tm) + f, g_ref[...], b_ref[...])


def _combine(x1, tg, dest, ys, prm, tm):
    t = x1.shape[0] // ROW_TILE
    row = lambda w: pl.BlockSpec((tm, w), lambda i: (i, 0))
    full = lambda a: pl.BlockSpec(a.shape, lambda i: (0,) * a.ndim)
    return pl.pallas_call(
        functools.partial(_combine_kernel, tm=tm),
        grid=(t // tm,),
        in_specs=[pl.BlockSpec((1, 1, TOP_K * tm), lambda i: (i, 0, 0), memory_space=pltpu.SMEM),
                  pl.BlockSpec((tm * ROW_TILE, LANE), lambda i: (i, 0)), row(LANE),
                  full(prm['ln2_g']), full(prm['ln2_b']), pl.BlockSpec(memory_space=pl.ANY)],
        out_specs=row(D_MODEL),
        out_shape=jax.ShapeDtypeStruct((t, D_MODEL), F32),
        scratch_shapes=[pltpu.VMEM((TOP_K, tm * ROW_TILE, LANE), F32), pltpu.SemaphoreType.DMA],
        compiler_params=_params(("arbitrary",)),
        name="combine_ln",
    )(dest.reshape(t // tm, 1, TOP_K * tm), x1, tg, prm['ln2_g'], prm['ln2_b'], ys)


def _dt_pieces(dt_piece):
    return jnp.concatenate([dt_piece[:, 0:8], dt_piece[:, 128:136]], axis=1)


def _layer(xp, xs, prm, lb, rel_bias, st, n_p, seq, n_s, layer, moe, xs_buf):
    tp = n_p * seq
    up = _in_proj(xp, prm['w_in'], 128)
    dt16 = _dt_pieces(up['dt'])
    dt_t = jnp.transpose(dt16.reshape(tp // CHUNK, CHUNK, A_HEADS), (0, 2, 1))
    zeros_conv = jnp.zeros((n_p, SUBLANE, A_CONV_DIM), F32)
    ya, ssm_p = _ssd(up, dt_t, zeros_conv, jnp.zeros((n_p, D_MODEL, A_STATE), F32), prm, n_p, seq, False)
    yb, hg_p = _hgrn(up, jnp.zeros((n_p, D_MODEL, LANE), F32), lb, prm['hgrn_norm_g'], n_p, seq, False)
    attn = [_attn_prompt(up['cq'], up['ck'], up['cv'], _prompt_bias(rel_bias, g), g, n_p, seq)
            for g in range(3)]
    x1p, tep, tgp, rkp, cnt_p = _merge(xp, ya, yb, attn, up['gates'], jnp.zeros((1, LANE), F32), prm, 256)
    xbc3 = up['xbc'].reshape(n_p, seq, A_CONV_DIM)
    conv_p = xbc3[:, seq - (A_CONV - 1):]
    k3 = up['ck'].reshape(n_p, seq, 768)
    v3 = up['cv'].reshape(n_p, seq, 768)

    def last_rows(a, g, w):
        return a[:, seq - min(w, seq):, 256 * g:256 * (g + 1)].reshape(n_p, min(w, seq), 4, 64)

    kv_p = [jnp.stack([last_rows(k3, g, w), last_rows(v3, g, w)], axis=2) for g, (w, _) in enumerate(C_GROUPS)]
    us = _in_proj(xs, prm['w_in'], n_s)
    dt16s = _dt_pieces(us['dt'])
    dt_ts = jnp.zeros((n_s, A_HEADS, CHUNK), F32).at[:, :, 0].set(dt16s)
    conv0 = jnp.pad(st['conv'], ((0, 0), (SUBLANE - (A_CONV - 1), 0), (0, 0)))
    yas, ssm_s = _ssd(us, dt_ts, conv0, st['ssm'].reshape(n_s, D_MODEL, A_STATE), prm, n_s, 1, True)
    ybs, hg_s = _hgrn(us, st['hgrn'].reshape(n_s, D_MODEL, LANE), lb, prm['hgrn_norm_g'], n_s, 1, True)
    bias_d = [_decode_bias(rel_bias, g) for g in range(3)]
    o_s, lse_s = _attn_decode(us['cq'], us['ck'], us['cv'], st['kv_t'], layer,
                              [b[0] for b in bias_d], jnp.stack([b[1] for b in bias_d]))
    attn_s = [(o_s[:, 256 * g:256 * (g + 1)], lse_s[:, 256 * g:256 * (g + 1)]) for g in range(3)]
    x1s, tes, tgs, rks, cnt = _merge(xs, yas, ybs, attn_s, us['gates'], cnt_p, prm, n_s)
    conv_s = jnp.concatenate([st['conv'][:, 1:], us['xbc'][:, None]], axis=1)
    ks4 = us['ck'].reshape(n_s, 1, 3, 4, 64)
    vs4 = us['cv'].reshape(n_s, 1, 3, 4, 64)
    kv_s = [jnp.stack([ks4[:, :, g], vs4[:, :, g]], axis=2) for g in range(3)]
    n_tiles = -(-(tp + n_s) * TOP_K // MOE_TILE) + N_EXPERTS
    pstart, tile_e, tile_nv, tile_nxt, tile_slot = _moe_plan(cnt[0, :N_EXPERTS].astype(jnp.int32), n_tiles)
    dest_p = _slots(tep, rkp, pstart)
    dest_s = _slots(tes, rks, pstart)
    x_sorted = jnp.zeros((n_tiles * MOE_TILE * ROW_TILE, LANE), F32) if xs_buf is None else xs_buf
    x_sorted = _dispatch(x1p, dest_p, x_sorted, 512)
    x_sorted = _dispatch(x1s, dest_s, x_sorted, n_s)
    y_sorted = _moe_experts(x_sorted, tile_e, tile_nv, tile_nxt, tile_slot, layer, *moe)
    yp = _combine(x1p, tgp, dest_p, y_sorted, prm, 512)
    ys = _combine(x1s, tgs, dest_s, y_sorted, prm, n_s)
    states_p = (conv_p, ssm_p.reshape(n_p, A_HEADS, A_HEAD_DIM, A_STATE),
                hg_p.reshape(n_p, B_HEADS, B_KEY_DIM, LANE), kv_p[0], kv_p[1], kv_p[2])
    states_s = (conv_s, ssm_s.reshape(n_s, A_HEADS, A_HEAD_DIM, A_STATE),
                hg_s.reshape(n_s, B_HEADS, B_KEY_DIM, LANE), kv_s[0], kv_s[1], kv_s[2])
    return yp, ys, states_p, states_s, x_sorted


def _prep_layer(l, w_in, conv_w, conv_b, dt_bias, a_log, d_skip, ssm_norm_g, hgrn_norm_g,
                w_branch_a, w_branch_b, w_branch_c, w_out, ln1_g, ln1_b, router_w, router_b,
                moe_w1, moe_b1, moe_w2, moe_b2, ln2_g, ln2_b):
    def lanes_per_group(v):
        return jnp.zeros((1, 256), F32).at[0, 0:8].set(v[:8]).at[0, 128:136].set(v[8:])

    def sublanes_per_group(v):
        return jnp.broadcast_to(v[:, None], (A_HEADS, LANE))

    return {
        'w_in': _pack_w_in(w_in[l]),
        'conv_w8': jnp.pad(conv_w[l], ((0, SUBLANE - A_CONV), (0, 0))),
        'conv_b': conv_b[l][None],
        'dt_bias_l': lanes_per_group(dt_bias[l]), 'a_log_l': lanes_per_group(a_log[l]),
        'dt_bias_t': sublanes_per_group(dt_bias[l]), 'a_log_t': sublanes_per_group(a_log[l]),
        'd_skip_l': jnp.repeat(d_skip[l], A_HEAD_DIM)[None],
        'ssm_norm_g': ssm_norm_g[l][None],
        'hgrn_norm_g': hgrn_norm_g[l][None],
        'w_branch_a': w_branch_a[l].astype(BF16), 'w_branch_b': w_branch_b[l].astype(BF16),
        'w_branch_c': w_branch_c[l].astype(BF16), 'w_out': w_out[l].astype(BF16),
        'ln1_g': ln1_g[l][None], 'ln1_b': ln1_b[l][None],
        'router_w': jnp.pad(router_w[l], ((0, 0), (0, LANE - N_EXPERTS))),
        'router_b': jnp.pad(router_b[l], (0, LANE - N_EXPERTS), constant_values=-jnp.inf)[None],
        'ln2_g': ln2_g[l][None], 'ln2_b': ln2_b[l][None],
    }


def kernel(x_prompt, x_sample, state_conv, state_ssm, state_hgrn, cache_kv_w128, cache_kv_w512, cache_kv_w2048, w_in, conv_w, conv_b, dt_bias, a_log, d_skip, ssm_norm_g, hgrn_lb, hgrn_norm_g, rel_bias, w_branch_a, w_branch_b, w_branch_c, w_out, ln1_g, ln1_b, router_w, router_b, moe_w1, moe_b1, moe_w2, moe_b2, ln2_g, ln2_b):
    n_p, seq, _ = x_prompt.shape
    n_s = x_sample.shape[0]
    depth = w_in.shape[0]
    p_lb = jax.nn.softmax(hgrn_lb.astype(F32), axis=0)
    lower_bounds = jnp.cumsum(p_lb, axis=0) - p_lb[0]
    yp = x_prompt.reshape(n_p * seq, D_MODEL)
    ys = x_sample.reshape(n_s, D_MODEL)
    st_p, st_s = [], []
    xs_buf = None
    kv_t = tuple(jnp.transpose(c, (0, 1, 3, 4, 5, 2)) for c in (cache_kv_w128, cache_kv_w512, cache_kv_w2048))
    moe = (moe_w1, moe_w2, moe_b1[:, :, None, 0::2], moe_b1[:, :, None, 1::2], moe_b2[:, :, None, :])
    for l in range(depth):
        prm = _prep_layer(l, w_in, conv_w, conv_b, dt_bias, a_log, d_skip, ssm_norm_g, hgrn_norm_g,
                          w_branch_a, w_branch_b, w_branch_c, w_out, ln1_g, ln1_b, router_w, router_b,
                          moe_w1, moe_b1, moe_w2, moe_b2, ln2_g, ln2_b)
        st = {'conv': state_conv[l], 'ssm': state_ssm[l], 'hgrn': state_hgrn[l], 'kv_t': kv_t}
        yp, ys, sp, ss, xs_buf = _layer(yp, ys, prm, lower_bounds[l][None], rel_bias, st, n_p, seq, n_s, l, moe,
                                        xs_buf)
        st_p.append(sp)
        st_s.append(ss)
    stack = lambda sts, i: jnp.stack([s[i] for s in sts], axis=0)
    return (yp.reshape(n_p, seq, D_MODEL), ys.reshape(n_s, 1, D_MODEL),
            stack(st_p, 0), stack(st_s, 0), stack(st_p, 1), stack(st_s, 1), stack(st_p, 2), stack(st_s, 2),
            stack(st_p, 3), stack(st_s, 3), stack(st_p, 4), stack(st_s, 4), stack(st_p, 5), stack(st_s, 5))
```

```python
import functools
import math

import jax
import jax.numpy as jnp
import numpy as np
from jax import lax
from jax.experimental import pallas as pl
from jax.experimental.pallas import tpu as pltpu

F32 = jnp.float32
BF16 = jnp.bfloat16
HI = lax.Precision.HIGHEST

D_MODEL = 1024
A_HEADS = 16
A_HEAD_DIM = 64
A_STATE = 128
A_CONV = 4
A_CONV_DIM = 1536
B_HEADS = 8
B_KEY_DIM = 128
C_GROUPS = ((128, 1), (512, 4), (2048, 16))
C_SPAN = 128
C_GROUP_WIDTH = 256
REL_BUCKETS = 32
REL_MAX_DIST = 2048
N_EXPERTS = 32
TOP_K = 4
SWIGLU_ALPHA = 1.702
SWIGLU_LIMIT = 7.0
DEEPNORM_ALPHA = (2.0 * 2) ** 0.25
LN_EPS = 1e-5
RMS_EPS = 1e-5
NEG_BIG = -1e30
LOG_FLOOR = 1e-30

LANE = 128
SUBLANE = 8
CHUNK = 128
SUB = 8
HGRN_HEADS_PER_STEP = 8
ATTN_ILP = 4
VMEM_LIMIT = 56 * 1024 * 1024

IN_PIECES = (('z', 1024), ('xbc', 1536), ('bq', 1024), ('bf', 1024), ('bi', 1024), ('bg', 1024),
             ('cq', 768), ('ck', 768), ('cv', 768), ('gates', 3072), ('dt', 256))
IN_PACKED = sum(w for _, w in IN_PIECES)


def _params(sem):
    return pltpu.CompilerParams(dimension_semantics=sem, vmem_limit_bytes=VMEM_LIMIT)


def _sigmoid(x):
    return 1.0 / (1.0 + jnp.exp(-x))


def _sigmoid_tanh(x):
    return 0.5 * jnp.tanh(0.5 * x) + 0.5


def _softplus(x):
    return jnp.maximum(x, 0.0) + jnp.log(1.0 + jnp.exp(-jnp.abs(x)))


def _dot(a, b):
    return jnp.dot(a.astype(BF16), b.astype(BF16), preferred_element_type=F32)


def _dot_nt(a, b):
    return lax.dot_general(a.astype(BF16), b.astype(BF16), (((1,), (1,)), ((), ())),
                           preferred_element_type=F32)


def _dot_hi(a, b):
    return jnp.dot(a, b, preferred_element_type=F32, precision=HI)


ROW_TILE = D_MODEL // LANE


def _store_row_tiles(ref, val):
    n = val.shape[0]
    for c in range(ROW_TILE):
        ref[pl.ds(c, n, stride=ROW_TILE), :] = val[:, LANE * c:LANE * (c + 1)]


def _load_row_tiles(ref, n, lead=None):
    rows = pl.ds
    parts = []
    for c in range(ROW_TILE):
        idx = (rows(c, n, stride=ROW_TILE), slice(None))
        parts.append(ref[idx] if lead is None else ref[(lead,) + idx])
    return jnp.concatenate(parts, axis=1)


def _layernorm(h, g, b):
    mu = jnp.mean(h, -1, keepdims=True)
    c = h - mu
    var = jnp.mean(c * c, -1, keepdims=True)
    return c * lax.rsqrt(var + LN_EPS) * g + b


def _in_proj_kernel(x_ref, w_ref, *o_refs):
    xb = x_ref[...].astype(BF16)
    off = 0
    for o_ref in o_refs:
        wd = o_ref.shape[1]
        o_ref[...] = jnp.dot(xb, w_ref[:, off:off + wd], preferred_element_type=F32)
        off += wd


def _in_proj(x, w_packed, tm):
    t = x.shape[0]
    outs = pl.pallas_call(
        _in_proj_kernel,
        grid=(t // tm,),
        in_specs=[pl.BlockSpec((tm, D_MODEL), lambda i: (i, 0)),
                  pl.BlockSpec(memory_space=pltpu.VMEM)],
        out_specs=[pl.BlockSpec((tm, w), lambda i: (i, 0)) for _, w in IN_PIECES],
        out_shape=[jax.ShapeDtypeStruct((t, w), F32) for _, w in IN_PIECES],
        compiler_params=_params(("arbitrary",)),
        name="in_proj",
    )(x, w_packed)
    return {name: o for (name, _), o in zip(IN_PIECES, outs)}


def _pack_w_in(w):
    dt = w[:, 2560:2576]
    dtp = jnp.zeros((D_MODEL, 256), F32).at[:, 0:8].set(dt[:, :8]).at[:, 128:136].set(dt[:, 8:])
    return jnp.concatenate([w[:, :2560], w[:, 2576:], dtp], axis=1).astype(BF16)


def _stage(ref, ci, decode, r):
    if decode:
        row = ref[pl.ds(r, 1), :]
        rows = lax.broadcasted_iota(jnp.int32, (CHUNK, row.shape[1]), 0)
        return jnp.where(rows == 0, jnp.broadcast_to(row, (CHUNK, row.shape[1])), 0.0)
    return ref[pl.ds(pl.multiple_of(ci * CHUNK, CHUNK), CHUNK), :]


def _unstage(ref, val, ci, decode, r):
    if decode:
        ref[pl.ds(r, 1), :] = val[0:1, :]
    else:
        ref[pl.ds(pl.multiple_of(ci * CHUNK, CHUNK), CHUNK), :] = val


def _ssd_kernel(xs_ref, bm_ref, cm_ref, z_ref, dt_ref, dtt_ref,
                c0x_ref, c0b_ref, c0c_ref, h0_ref,
                wx_ref, wb_ref, wc_ref, bx_ref, bb_ref, bc_ref,
                dtb_ref, alog_ref, dtbt_ref, alogt_ref, dsk_ref, ng_ref,
                y_ref, hn_ref,
                h_scr, cx_scr, cb_scr, cc_scr, px_scr, pb_scr, pc_scr, *, nchunks, decode):
    b = pl.program_id(1)
    t = pl.program_id(2)
    r = b % SUBLANE
    q = CHUNK

    @pl.when(t == 0)
    def _():
        h_scr[...] = h0_ref[0]
        cx_scr[...] = c0x_ref[0]
        cb_scr[...] = c0b_ref[0]
        cc_scr[...] = c0c_ref[0]

    rows = lax.broadcasted_iota(jnp.int32, (q, q), 0)
    cols = lax.broadcasted_iota(jnp.int32, (q, q), 1)
    tril = rows >= cols
    tril_f = tril.astype(F32)
    triu_f = (rows <= cols).astype(F32)
    lane = lax.broadcasted_iota(jnp.int32, (q, LANE), 1)
    lo = lane < A_HEAD_DIM
    row_lo = lax.broadcasted_iota(jnp.int32, (LANE, LANE), 0) < A_HEAD_DIM
    valid_col = lax.broadcasted_iota(jnp.int32, (q, LANE), 0) == 0
    valid_row = lax.broadcasted_iota(jnp.int32, (SUBLANE, q), 1) == 0

    def conv(x, carry, pad, w_ref, b_ref):
        pad[0:SUBLANE, :] = carry[...]
        pad[SUBLANE:SUBLANE + q, :] = x
        if not decode:
            carry[...] = pad[q:q + SUBLANE, :]
        acc = b_ref[...]
        for j in range(A_CONV):
            acc = acc + pad[5 + j:5 + j + q, :] * w_ref[j:j + 1, :]
        return acc * _sigmoid(acc)

    def chunk(ci, carry_unused):
        xs = conv(_stage(xs_ref, ci, decode, r), cx_scr, px_scr, wx_ref, bx_ref)
        bm = conv(_stage(bm_ref, ci, decode, r), cb_scr, pb_scr, wb_ref, bb_ref)
        cm = conv(_stage(cm_ref, ci, decode, r), cc_scr, pc_scr, wc_ref, bc_ref)
        a_lane = -jnp.exp(alog_ref[...])
        a_sub = -jnp.exp(alogt_ref[...])
        dt = _softplus(_stage(dt_ref, ci, decode, r) + dtb_ref[...])
        dtt = _softplus(dtt_ref[ci] + dtbt_ref[...])
        if decode:
            dt = jnp.where(valid_col, dt, 0.0)
            dtt = jnp.where(valid_row, dtt, 0.0)
        if decode:
            acum = jnp.broadcast_to((dt * a_lane)[0:1, :], (q, LANE))
            acum_t = jnp.broadcast_to((dtt * a_sub)[:, 0:1], (SUBLANE, q))
        else:
            acum = _dot_hi(tril_f, dt * a_lane)
            acum_t = _dot_hi(dtt * a_sub, triu_f)
        a_last = acum[q - 1:q, :]
        cb = _dot_nt(cm, bm)
        cm_b = cm.astype(BF16)
        bm_b = bm.astype(BF16)
        ys = []
        for j in range(4):
            xp = xs[:, LANE * j:LANE * (j + 1)]
            xp_b = xp.astype(BF16)
            yd = []
            for half in range(2):
                hl = 2 * j + half
                diff = acum[:, hl:hl + 1] - acum_t[hl:hl + 1, :]
                lm = jnp.exp(jnp.where(tril, diff, NEG_BIG))
                m = cb * lm * dtt[hl:hl + 1, :]
                yd.append(jnp.dot(m.astype(BF16), xp_b, preferred_element_type=F32))
            y = jnp.where(lo, yd[0], yd[1])
            hp = h_scr[LANE * j:LANE * (j + 1), :]
            yo = lax.dot_general(cm_b, hp.astype(BF16), (((1,), (1,)), ((), ())),
                                 preferred_element_type=F32)
            e0 = acum[:, 2 * j:2 * j + 1]
            e1 = acum[:, 2 * j + 1:2 * j + 2]
            y = y + yo * jnp.exp(jnp.where(lo, e0, e1))
            w0 = dt[:, 2 * j:2 * j + 1] * jnp.exp(a_last[:, 2 * j:2 * j + 1] - e0)
            w1 = dt[:, 2 * j + 1:2 * j + 2] * jnp.exp(a_last[:, 2 * j + 1:2 * j + 2] - e1)
            xw = xp * jnp.where(lo, w0, w1)
            upd = jnp.dot(xw.T.astype(BF16), bm_b, preferred_element_type=F32)
            dec = jnp.exp(jnp.where(row_lo, a_last[:, 2 * j:2 * j + 1], a_last[:, 2 * j + 1:2 * j + 2]))
            h_scr[LANE * j:LANE * (j + 1), :] = hp * dec + upd
            ys.append(y)
        y = jnp.concatenate(ys, axis=1) + dsk_ref[...] * xs
        z = _stage(z_ref, ci, decode, r)
        y = y * (z * _sigmoid(z))
        y = y * lax.rsqrt(jnp.mean(y * y, -1, keepdims=True) + RMS_EPS) * ng_ref[...]
        _unstage(y_ref, y, ci, decode, r)
        return carry_unused

    lax.fori_loop(0, nchunks, chunk, 0)

    @pl.when(t == pl.num_programs(2) - 1)
    def _():
        hn_ref[0] = h_scr[...]


def _ssd(u, dt_t, conv0_8, h0, prm, n, seq, decode):
    tb = CHUNK if decode else min(512, seq)
    nb = 1 if decode else seq // tb
    nchunks = tb // CHUNK
    rb = SUBLANE if decode else tb

    def rowmap(lane_block):
        if decode:
            return lambda g, b, t: (b // SUBLANE, lane_block(g))
        return lambda g, b, t: (b * nb + t, lane_block(g))

    chunk_map = (lambda g, b, t: (b, g, 0)) if decode else (lambda g, b, t: (b * nb + t, g, 0))
    cw = (512, LANE, LANE)
    lane_blocks = (lambda g: g, lambda g: 8 + g, lambda g: 10 + g)
    in_specs = [
        pl.BlockSpec((rb, 512), rowmap(lane_blocks[0])),
        pl.BlockSpec((rb, LANE), rowmap(lane_blocks[1])),
        pl.BlockSpec((rb, LANE), rowmap(lane_blocks[2])),
        pl.BlockSpec((rb, 512), rowmap(lambda g: g)),
        pl.BlockSpec((rb, LANE), rowmap(lambda g: g)),
        pl.BlockSpec((nchunks, SUBLANE, CHUNK), chunk_map),
    ]
    in_specs += [pl.BlockSpec((1, SUBLANE, w), (lambda lb: (lambda g, b, t: (b, 0, lb(g))))(lb))
                 for w, lb in zip(cw, lane_blocks)]
    in_specs += [pl.BlockSpec((1, 512, A_STATE), lambda g, b, t: (b, g, 0))]
    in_specs += [pl.BlockSpec((SUBLANE, w), (lambda lb: (lambda g, b, t: (0, lb(g))))(lb))
                 for w, lb in zip(cw, lane_blocks)]
    in_specs += [pl.BlockSpec((1, w), (lambda lb: (lambda g, b, t: (0, lb(g))))(lb))
                 for w, lb in zip(cw, lane_blocks)]
    in_specs += [
        pl.BlockSpec((1, LANE), lambda g, b, t: (0, g)),
        pl.BlockSpec((1, LANE), lambda g, b, t: (0, g)),
        pl.BlockSpec((SUBLANE, LANE), lambda g, b, t: (g, 0)),
        pl.BlockSpec((SUBLANE, LANE), lambda g, b, t: (g, 0)),
        pl.BlockSpec((1, 512), lambda g, b, t: (0, g)),
        pl.BlockSpec((1, 512), lambda g, b, t: (0, g)),
    ]
    rows_total = u['z'].shape[0]
    y, hn = pl.pallas_call(
        functools.partial(_ssd_kernel, nchunks=nchunks, decode=decode),
        grid=(2, n, nb),
        in_specs=in_specs,
        out_specs=[pl.BlockSpec((rb, 512), rowmap(lambda g: g)),
                   pl.BlockSpec((1, 512, A_STATE), lambda g, b, t: (b, g, 0))],
        out_shape=[jax.ShapeDtypeStruct((rows_total, D_MODEL), F32),
                   jax.ShapeDtypeStruct((n, D_MODEL, A_STATE), F32)],
        scratch_shapes=[pltpu.VMEM((512, A_STATE), F32),
                        pltpu.VMEM((SUBLANE, 512), F32), pltpu.VMEM((SUBLANE, LANE), F32),
                        pltpu.VMEM((SUBLANE, LANE), F32),
                        pltpu.VMEM((CHUNK + SUBLANE, 512), F32), pltpu.VMEM((CHUNK + SUBLANE, LANE), F32),
                        pltpu.VMEM((CHUNK + SUBLANE, LANE), F32)],
        compiler_params=_params(("arbitrary", "arbitrary", "arbitrary")),
        name="ssd_decode" if decode else "ssd_prompt",
    )(u['xbc'], u['xbc'], u['xbc'], u['z'], u['dt'], dt_t,
      conv0_8, conv0_8, conv0_8, h0,
      prm['conv_w8'], prm['conv_w8'], prm['conv_w8'], prm['conv_b'], prm['conv_b'], prm['conv_b'],
      prm['dt_bias_l'], prm['a_log_l'], prm['dt_bias_t'], prm['a_log_t'], prm['d_skip_l'], prm['ssm_norm_g'])
    return y, hn


def _hgrn_kernel(q_ref, f_ref, i_ref, g_ref, lb_ref, ng_ref, s0_ref, y_ref, sn_ref, st_scr,
                 *, nchunks, decode, nheads):
    b = pl.program_id(1)
    t = pl.program_id(2)
    r = b % SUBLANE
    c = CHUNK

    @pl.when(t == 0)
    def _():
        for hh in range(nheads):
            st_scr[hh] = s0_ref[0, LANE * hh:LANE * (hh + 1), :].T

    rows = lax.broadcasted_iota(jnp.int32, (c, c), 0)
    cols = lax.broadcasted_iota(jnp.int32, (c, c), 1)
    tril_f = (rows >= cols).astype(F32)
    row_c = lax.broadcasted_iota(jnp.int32, (c, LANE), 0)
    row_s = lax.broadcasted_iota(jnp.int32, (SUB, LANE), 0)
    blk_xor = (rows // SUB) ^ (cols // SUB)
    level = jnp.where(blk_xor >= 8, 3, jnp.where(blk_xor >= 4, 2, jnp.where(blk_xor >= 2, 1, 0)))
    level = jnp.where((rows // SUB) > (cols // SUB), level, -1)
    nlevels = 4
    assert SUB << nlevels == c

    def one_head(hh, ci):
        ls = slice(LANE * hh, LANE * (hh + 1))
        lbv = lb_ref[:, ls]
        f = _stage(f_ref, ci, decode, r)[:, ls]
        qr = _stage(q_ref, ci, decode, r)[:, ls]
        v = _stage(i_ref, ci, decode, r)[:, ls]
        gr = _stage(g_ref, ci, decode, r)[:, ls]
        q = qr * _sigmoid(qr) * (B_KEY_DIM ** -0.5)
        e_f = jnp.exp(-jnp.abs(f))
        r_f = 1.0 / (1.0 + e_f)
        pos = f >= 0.0
        sig_p = jnp.where(pos, r_f, e_f * r_f)
        sig_n = jnp.where(pos, e_f * r_f, r_f)
        logf = jnp.log(jnp.maximum(lbv + (1.0 - lbv) * sig_p, LOG_FLOOR))
        k = (1.0 - lbv) * sig_n
        if decode:
            logf = jnp.where(row_c == 0, logf, 0.0)
            k = jnp.where(row_c == 0, k, 0.0)
        if decode:
            bcum = jnp.broadcast_to(logf[0:1, :], (c, LANE))
        else:
            bcum = _dot_hi(tril_f, logf)
        b_last = bcum[c - 1:c, :]
        st = st_scr[hh]
        o = _dot_nt(q * jnp.exp(bcum), st)
        if not decode:
            att = jnp.zeros((c, c), F32)
            for lv in range(nlevels):
                h = SUB << lv
                ref = jnp.concatenate(
                    [jnp.broadcast_to(bcum[2 * h * m + h - 1:2 * h * m + h, :], (2 * h, LANE))
                     for m in range(c // (2 * h))], axis=0)
                x = jnp.exp(-jnp.abs(bcum - ref))
                att = jnp.where(level == lv, _dot_nt(q * x, k * x), att)
            o = o + _dot(att, v)
        diag = []
        for i in range(c // SUB):
            if decode and i > 0:
                diag.append(jnp.zeros((SUB, LANE), F32))
                continue
            sl = slice(SUB * i, SUB * (i + 1))
            qb, kb, vb, bb = q[sl, :], k[sl, :], v[sl, :], bcum[sl, :]
            od = jnp.zeros((SUB, LANE), F32)
            for s in range(1 if decode else SUB):
                e = jnp.exp(jnp.where(row_s >= s, bb - bb[s:s + 1, :], NEG_BIG))
                rs = jnp.sum(qb * kb[s:s + 1, :] * e, -1, keepdims=True)
                od = od + rs * vb[s:s + 1, :]
            diag.append(od)
        o = o + jnp.concatenate(diag, axis=0)
        k2 = k * jnp.exp(b_last - bcum)
        st_scr[hh] = jnp.exp(b_last) * st + jnp.dot(v.T.astype(BF16), k2.astype(BF16),
                                                     preferred_element_type=F32)
        gate = gr * _sigmoid(gr)
        return o * lax.rsqrt(jnp.mean(o * o, -1, keepdims=True) + RMS_EPS) * ng_ref[...] * gate

    def chunk(ci, carry_unused):
        y = [one_head(hh, ci) for hh in range(nheads)]
        _unstage(y_ref, y[0] if nheads == 1 else jnp.concatenate(y, axis=1), ci, decode, r)
        return carry_unused

    lax.fori_loop(0, nchunks, chunk, 0)

    @pl.when(t == pl.num_programs(2) - 1)
    def _():
        for hh in range(nheads):
            sn_ref[0, LANE * hh:LANE * (hh + 1), :] = st_scr[hh].T


def _hgrn(u, s0, lb, ng, n, seq, decode):
    tb = CHUNK if decode else min(512, seq)
    nb = 1 if decode else seq // tb
    nchunks = tb // CHUNK
    rb = SUBLANE if decode else tb
    rowmap = (lambda h, b, t: (b // SUBLANE, h)) if decode else (lambda h, b, t: (b * nb + t, h))
    rows_total = u['bq'].shape[0]
    nheads = B_HEADS if decode else HGRN_HEADS_PER_STEP
    wd = LANE * nheads
    y, sn = pl.pallas_call(
        functools.partial(_hgrn_kernel, nchunks=nchunks, decode=decode, nheads=nheads),
        grid=(B_HEADS // nheads, n, nb),
        in_specs=[pl.BlockSpec((rb, wd), rowmap)] * 4 + [
            pl.BlockSpec((1, wd), lambda h, b, t: (0, h)),
            pl.BlockSpec((1, LANE), lambda h, b, t: (0, 0)),
            pl.BlockSpec((1, B_KEY_DIM * nheads, LANE), lambda h, b, t: (b, h, 0))],
        out_specs=[pl.BlockSpec((rb, wd), rowmap),
                   pl.BlockSpec((1, B_KEY_DIM * nheads, LANE), lambda h, b, t: (b, h, 0))],
        out_shape=[jax.ShapeDtypeStruct((rows_total, D_MODEL), F32),
                   jax.ShapeDtypeStruct((n, D_MODEL, LANE), F32)],
        scratch_shapes=[pltpu.VMEM((nheads, LANE, B_KEY_DIM), F32)],
        compiler_params=_params(("arbitrary", "arbitrary", "arbitrary")),
        name="hgrn_decode" if decode else "hgrn_prompt",
    )(u['bq'], u['bf'], u['bi'], u['bg'], lb, ng, s0)
    return y, sn


def _t5_bucket_np(dist):
    exact = REL_BUCKETS // 2
    d = np.maximum(dist, 1).astype(np.float32)
    large = exact + (np.log(d / np.float32(exact)) / np.float32(math.log(REL_MAX_DIST / exact))
                     * np.float32(REL_BUCKETS - exact)).astype(np.int32)
    large = np.clip(large, 0, REL_BUCKETS - 1)
    return np.where(dist < exact, dist, large)


def _attn_prompt_kernel(q_ref, kc_ref, kp_ref, vc_ref, vp_ref, bias_ref, o_ref, lse_ref, *, dil):
    first = pl.program_id(1) == 0
    s_q = C_SPAN
    lane = lax.broadcasted_iota(jnp.int32, (s_q, LANE), 1)
    lo = lane < 64
    kcol = lax.broadcasted_iota(jnp.int32, (s_q, 2 * s_q), 1)
    no_prev = jnp.logical_and(first, kcol < s_q)

    pair = pl.program_id(2)

    def solve(qp, kprev, kcur, vprev, vcur, mask_prev):
        kp2 = jnp.concatenate([kprev, kcur], axis=0).astype(BF16)
        vp2 = jnp.concatenate([vprev, vcur], axis=0).astype(BF16)
        oh, lh = [], []
        for half in range(2):
            qm = jnp.where(lo if half == 0 else jnp.logical_not(lo), qp, 0.0).astype(BF16)
            s = lax.dot_general(qm, kp2, (((1,), (1,)), ((), ())), preferred_element_type=F32)
            s = s * (64 ** -0.5) + bias_ref[2 * pair + half]
            if mask_prev:
                s = jnp.where(no_prev, NEG_BIG, s)
            m = jnp.max(s, -1, keepdims=True)
            p = jnp.exp(s - m)
            den = jnp.sum(p, -1, keepdims=True)
            oh.append(jnp.dot((p / den).astype(BF16), vp2, preferred_element_type=F32))
            lh.append(m + jnp.log(den))
        return jnp.where(lo, oh[0], oh[1]), jnp.where(lo, lh[0], lh[1])

    if dil == 1:
        for sb in range(ATTN_ILP):
            cur = slice(s_q * sb, s_q * (sb + 1))
            if sb == 0:
                o, lse = solve(q_ref[cur, :], kp_ref[...], kc_ref[cur, :], vp_ref[...], vc_ref[cur, :], True)
            else:
                prv = slice(s_q * (sb - 1), s_q * sb)
                o, lse = solve(q_ref[cur, :], kc_ref[prv, :], kc_ref[cur, :], vc_ref[prv, :], vc_ref[cur, :],
                               False)
            o_ref[cur, :] = o
            lse_ref[cur, :] = lse
    else:
        def residues(it, carry):
            for jj in range(ATTN_ILP):
                sl = pl.ds(it * ATTN_ILP + jj, s_q, stride=dil)
                o, lse = solve(q_ref[sl, :], kp_ref[sl, :], kc_ref[sl, :], vp_ref[sl, :], vc_ref[sl, :], True)
                o_ref[sl, :] = o
                lse_ref[sl, :] = lse
            return carry

        lax.fori_loop(0, dil // ATTN_ILP, residues, 0)


def _attn_prompt(cq, ck, cv, bias, g, n, seq):
    dil = C_GROUPS[g][1]
    look = C_SPAN * dil
    rows_blk = look * (ATTN_ILP if dil == 1 else 1)
    nbk = seq // rows_blk
    per = rows_blk // look
    cur = lambda b, i, j: (b * nbk + i, 2 * g + j)
    prev = lambda b, i, j: (jnp.maximum((b * nbk + i) * per - 1, 0), 2 * g + j)
    blk = (rows_blk, LANE)
    pblk = (look, LANE)
    return pl.pallas_call(
        functools.partial(_attn_prompt_kernel, dil=dil),
        grid=(n, nbk, 2),
        in_specs=[pl.BlockSpec(blk, cur), pl.BlockSpec(blk, cur), pl.BlockSpec(pblk, prev),
                  pl.BlockSpec(blk, cur), pl.BlockSpec(pblk, prev),
                  pl.BlockSpec((4, C_SPAN, 2 * C_SPAN), lambda b, i, j: (0, 0, 0))],
        out_specs=[pl.BlockSpec(blk, lambda b, i, j: (b * nbk + i, j))] * 2,
        out_shape=[jax.ShapeDtypeStruct((n * seq, C_GROUP_WIDTH), F32)] * 2,
        compiler_params=_params(("arbitrary", "arbitrary", "arbitrary")),
        name=f"attn_prompt_g{g}",
    )(cq, ck, ck, cv, cv, bias)


def _prompt_bias(rel_bias, g):
    dil = C_GROUPS[g][1]
    qi = np.arange(C_SPAN)[:, None]
    kj = np.arange(2 * C_SPAN)[None, :]
    rel = qi + C_SPAN - kj
    band = (rel >= 0) & (rel <= C_SPAN)
    idx = _t5_bucket_np(np.maximum(rel, 0) * dil)
    tab = rel_bias[:, 4 * g:4 * g + 4]
    hit = jnp.asarray(idx)[None, :, :, None] == jnp.arange(REL_BUCKETS)
    vals = jnp.sum(jnp.where(hit, tab.T[:, None, None, :], 0.0), axis=-1)
    return jnp.where(band[None], vals, NEG_BIG)


def _decode_bias(rel_bias, g):
    win, dil = C_GROUPS[g]
    tab = rel_bias[:, 4 * g:4 * g + 4]
    pos = np.arange(win)
    vals = jnp.where((pos % dil == 0)[:, None], tab[_t5_bucket_np(win - pos)], NEG_BIG)
    bias_buf = jnp.zeros((2, SUBLANE, win), F32).at[:, 0:2].set(vals.T.reshape(2, 2, win))
    new = jnp.broadcast_to(tab[0].reshape(2, 2, 1), (2, 2, LANE))
    bias_new = jnp.zeros((2, SUBLANE, LANE), F32).at[:, 0:2].set(new)
    return bias_buf, bias_new


def _attn_decode_kernel(q_ref, k_ref, v_ref, c0_ref, c1_ref, c2_ref, bb0_ref, bb1_ref, bb2_ref, bn_ref,
                        o_ref, lse_ref):
    b = pl.program_id(0)
    row8 = lax.broadcasted_iota(jnp.int32, (SUBLANE, LANE), 0)
    lane8 = lax.broadcasted_iota(jnp.int32, (SUBLANE, LANE), 1)
    qmask = jnp.logical_or(jnp.logical_and(row8 == 0, lane8 < 64), jnp.logical_and(row8 == 1, lane8 >= 64))
    lo1 = lax.broadcasted_iota(jnp.int32, (1, LANE), 1) < 64
    q_all = q_ref[pl.ds(b, 1), :]
    k_all = k_ref[pl.ds(b, 1), :]
    v_all = v_ref[pl.ds(b, 1), :]
    o_parts, lse_parts = [], []
    for g, (buf_ref, bb_ref) in enumerate(((c0_ref, bb0_ref), (c1_ref, bb1_ref), (c2_ref, bb2_ref))):
        win = C_GROUPS[g][0]
        for j in range(2):
            c0 = C_GROUP_WIDTH * g + LANE * j
            qrow = q_all[:, c0:c0 + LANE]
            knew = k_all[:, c0:c0 + LANE]
            vnew = v_all[:, c0:c0 + LANE]
            q8 = jnp.where(qmask, jnp.broadcast_to(qrow, (SUBLANE, LANE)), 0.0)
            kt = buf_ref[0, 0, 0, 2 * j:2 * j + 2].reshape(LANE, win)
            vt = buf_ref[0, 0, 1, 2 * j:2 * j + 2].reshape(LANE, win)
            s = _dot(q8, kt) * (64 ** -0.5) + bb_ref[j]
            snew = jnp.sum(q8 * knew, -1, keepdims=True) * (64 ** -0.5) + bn_ref[g, j][:, 0:1]
            m = jnp.maximum(jnp.max(s, -1, keepdims=True), snew)
            p = jnp.exp(s - m)
            pn = jnp.exp(snew - m)
            den = jnp.sum(p, -1, keepdims=True) + pn
            o8 = _dot_nt(p / den, vt) + (pn / den) * vnew
            lse8 = jnp.broadcast_to(m + jnp.log(den), (SUBLANE, LANE))
            o_parts.append(jnp.where(lo1, o8[0:1, :], o8[1:2, :]))
            lse_parts.append(jnp.where(lo1, lse8[0:1, :], lse8[1:2, :]))
    o_ref[pl.ds(b, 1), :] = jnp.concatenate(o_parts, axis=1)
    lse_ref[pl.ds(b, 1), :] = jnp.concatenate(lse_parts, axis=1)


def _attn_decode(cq, ck, cv, caches_t, layer, bias_bufs, bias_new):
    n = cq.shape[0]
    full = lambda a: pl.BlockSpec(a.shape, lambda b: (0,) * a.ndim)
    cache_spec = lambda c: pl.BlockSpec((1, 1) + c.shape[2:], lambda b: (layer, b, 0, 0, 0, 0))
    return pl.pallas_call(
        _attn_decode_kernel,
        grid=(n,),
        in_specs=[full(cq), full(ck), full(cv)] + [cache_spec(c) for c in caches_t]
        + [full(bb) for bb in bias_bufs] + [full(bias_new)],
        out_specs=[pl.BlockSpec((n, 768), lambda b: (0, 0))] * 2,
        out_shape=[jax.ShapeDtypeStruct((n, 768), F32)] * 2,
        compiler_params=_params(("arbitrary",)),
        name="attn_decode",
    )(cq, ck, cv, *caches_t, *bias_bufs, bias_new)


def _merge_kernel(x_ref, ya_ref, yb_ref, o0_ref, o1_ref, o2_ref, l0_ref, l1_ref, l2_ref, gt_ref,
                  wa_ref, wb_ref, wc_ref, wo_ref, g_ref, b_ref, rw_ref, rb_ref, cnt0_ref,
                  x1_ref, te_ref, tg_ref, rk_ref, cnt_ref, cnt_scr):
    l0, l1, l2 = l0_ref[...], l1_ref[...], l2_ref[...]
    lm = jnp.maximum(jnp.maximum(l0, l1), l2)
    e0, e1, e2 = jnp.exp(l0 - lm), jnp.exp(l1 - lm), jnp.exp(l2 - lm)
    den = e0 + e1 + e2
    yc = (e0 / den) * o0_ref[...] + (e1 / den) * o1_ref[...] + (e2 / den) * o2_ref[...]
    ga = _sigmoid_tanh(gt_ref[:, 0:D_MODEL])
    gb = _sigmoid_tanh(gt_ref[:, D_MODEL:2 * D_MODEL])
    gc = _sigmoid_tanh(gt_ref[:, 2 * D_MODEL:3 * D_MODEL])
    merged = (ga * _dot(ya_ref[...], wa_ref[...]) + gb * _dot(yb_ref[...], wb_ref[...])
              + gc * _dot(yc, wc_ref[...]))
    h = DEEPNORM_ALPHA * x_ref[...] + _dot(merged, wo_ref[...])
    x1 = _layernorm(h, g_ref[...], b_ref[...])
    _store_row_tiles(x1_ref, x1)
    logits = _dot(x1, rw_ref[...]) + rb_ref[...]
    lane = lax.broadcasted_iota(jnp.int32, logits.shape, 1)
    lane_f = lane.astype(F32)
    te = jnp.zeros(logits.shape, F32)
    vals, onehots = [], []
    for k in range(TOP_K):
        m = jnp.max(logits, -1, keepdims=True)
        idx = jnp.min(jnp.where(logits == m, lane_f, float(LANE)), -1, keepdims=True)
        te = jnp.where(lane == k, idx, te)
        vals.append(m)
        hit = lane_f == idx
        onehots.append(hit.astype(F32))
        logits = jnp.where(hit, -jnp.inf, logits)
    ex = [jnp.exp(v - vals[0]) for v in vals]
    tot = ex[0] + ex[1] + ex[2] + ex[3]
    tg = jnp.zeros(logits.shape, F32)
    for k in range(TOP_K):
        tg = jnp.where(lane == k, ex[k] / tot, tg)
    te_ref[...] = te.astype(jnp.int32)
    tg_ref[...] = tg
    @pl.when(pl.program_id(0) == 0)
    def _():
        cnt_scr[...] = cnt0_ref[...]

    tm = logits.shape[0]
    oh = onehots[0] + onehots[1] + onehots[2] + onehots[3]
    earlier = (lax.broadcasted_iota(jnp.int32, (tm, tm), 0)
               > lax.broadcasted_iota(jnp.int32, (tm, tm), 1)).astype(BF16)
    before = jnp.dot(earlier, oh.astype(BF16), preferred_element_type=F32) + cnt_scr[...]
    rank = jnp.zeros(logits.shape, F32)
    for k in range(TOP_K):
        rank = jnp.where(lane == k, jnp.sum(onehots[k] * before, -1, keepdims=True), rank)
    rk_ref[...] = rank.astype(jnp.int32)
    cnt_scr[...] = cnt_scr[...] + jnp.sum(oh, 0, keepdims=True)
    cnt_ref[...] = cnt_scr[...]


def _merge(x, ya, yb, attn, gates, cnt0, prm, tm):
    t = x.shape[0]
    row = lambda w: pl.BlockSpec((tm, w), lambda i: (i, 0))
    full = lambda a: pl.BlockSpec(a.shape, lambda i: (0,) * a.ndim)
    ws = [prm['w_branch_a'], prm['w_branch_b'], prm['w_branch_c'], prm['w_out'],
          prm['ln1_g'], prm['ln1_b'], prm['router_w'], prm['router_b'], cnt0]
    (o0, l0), (o1, l1), (o2, l2) = attn
    return pl.pallas_call(
        _merge_kernel,
        grid=(t // tm,),
        in_specs=[row(D_MODEL)] * 3 + [row(C_GROUP_WIDTH)] * 6 + [row(3 * D_MODEL)] + [full(a) for a in ws],
        out_specs=[pl.BlockSpec((tm * ROW_TILE, LANE), lambda i: (i, 0)), row(LANE), row(LANE), row(LANE),
                   pl.BlockSpec((1, LANE), lambda i: (0, 0))],
        out_shape=[jax.ShapeDtypeStruct((t * ROW_TILE, LANE), F32), jax.ShapeDtypeStruct((t, LANE), jnp.int32),
                   jax.ShapeDtypeStruct((t, LANE), F32), jax.ShapeDtypeStruct((t, LANE), jnp.int32),
                   jax.ShapeDtypeStruct((1, LANE), F32)],
        scratch_shapes=[pltpu.VMEM((1, LANE), F32)],
        compiler_params=_params(("arbitrary",)),
        name="merge_ln_router",
    )(x, ya, yb, o0, o1, o2, l0, l1, l2, gates, *ws)


MOE_TILE = 512


def _dispatch_kernel(dest_ref, x_ref, xs_in_hbm, xs_hbm, sem, *, tm):
    del xs_in_hbm

    def row_copy(r, slot):
        src = x_ref.at[pl.ds(pl.multiple_of(r * ROW_TILE, ROW_TILE), ROW_TILE)]
        dst = xs_hbm.at[pl.ds(pl.multiple_of(slot * ROW_TILE, ROW_TILE), ROW_TILE)]
        return pltpu.make_async_copy(src, dst, sem)

    def start(r, c):
        for k in range(TOP_K):
            row_copy(r, dest_ref[0, 0, TOP_K * r + k]).start(priority=k % 2)
        return c

    def wait(r, c):
        for _ in range(TOP_K):
            row_copy(0, 0).wait()
        return c

    lax.fori_loop(0, tm, start, 0, unroll=8)
    lax.fori_loop(0, tm, wait, 0, unroll=8)


def _dispatch(x1, dest, xs, tm):
    t = x1.shape[0] // ROW_TILE
    return pl.pallas_call(
        functools.partial(_dispatch_kernel, tm=tm),
        grid=(t // tm,),
        in_specs=[pl.BlockSpec((1, 1, TOP_K * tm), lambda i: (i, 0, 0), memory_space=pltpu.SMEM),
                  pl.BlockSpec((tm * ROW_TILE, LANE), lambda i: (i, 0)),
                  pl.BlockSpec(memory_space=pl.ANY)],
        out_specs=pl.BlockSpec(memory_space=pl.ANY),
        out_shape=jax.ShapeDtypeStruct(xs.shape, F32),
        scratch_shapes=[pltpu.SemaphoreType.DMA],
        input_output_aliases={2: 0},
        compiler_params=_params(("arbitrary",)),
        name="moe_dispatch",
    )(dest.reshape(t // tm, 1, TOP_K * tm), x1, xs)


def _expert_kernel(te_ref, nv_ref, nx_ref, sl_ref, x_ref, w1_hbm, w2_hbm, b1g_ref, b1l_ref, b2_ref, y_ref,
                   w1_buf, w2_buf, w1p_scr, w2b_scr, sem, *, layer):
    i = pl.program_id(0)
    nv = nv_ref[i]
    changed = jnp.logical_or(i == 0, te_ref[i] != te_ref[jnp.maximum(i - 1, 0)])
    half = LANE
    blk = 2 * LANE
    d_ff = w2b_scr.shape[0]

    def fetch(e, s):
        return (pltpu.make_async_copy(w1_hbm.at[layer, e], w1_buf.at[s], sem.at[0, s]),
                pltpu.make_async_copy(w2_hbm.at[layer, e], w2_buf.at[s], sem.at[1, s]))

    @pl.when(i == 0)
    def _():
        for d in fetch(te_ref[0], sl_ref[0]):
            d.start()

    @pl.when(jnp.logical_and(nv > 0, changed))
    def _():
        s = sl_ref[i]
        for d in fetch(te_ref[i], s):
            d.wait()
        nx = nx_ref[i]

        @pl.when(nx >= 0)
        def _():
            for d in fetch(nx, 1 - s):
                d.start()

        src_r = lax.broadcasted_iota(jnp.int32, (blk, blk), 0)
        dst_c = lax.broadcasted_iota(jnp.int32, (blk, blk), 1)
        pick = jnp.where(dst_c < half, 2 * dst_c, 2 * (dst_c - half) + 1)
        perm = (src_r == pick).astype(BF16)
        for c in range(2 * d_ff // blk):
            wb = w1_buf[s, :, blk * c:blk * (c + 1)].astype(BF16)
            w1p_scr[:, blk * c:blk * (c + 1)] = jnp.dot(wb, perm, preferred_element_type=F32).astype(BF16)
        w2b_scr[...] = w2_buf[s].astype(BF16)

    tm = x_ref.shape[0] // ROW_TILE

    def mlp(rows):
        x = _load_row_tiles(x_ref, rows).astype(BF16)
        u = jnp.dot(x, w1p_scr[...], preferred_element_type=F32)
        nblk = 2 * d_ff // blk
        ug = jnp.concatenate([u[:, blk * c:blk * c + half] for c in range(nblk)], axis=1) + b1g_ref[0, 0]
        ul = jnp.concatenate([u[:, blk * c + half:blk * (c + 1)] for c in range(nblk)], axis=1) + b1l_ref[0, 0]
        glu = jnp.minimum(ug, SWIGLU_LIMIT)
        lin = jnp.clip(ul, -SWIGLU_LIMIT, SWIGLU_LIMIT)
        act = glu * _sigmoid_tanh(SWIGLU_ALPHA * glu) * (lin + 1.0)
        y = jnp.dot(act.astype(BF16), w2b_scr[...], preferred_element_type=F32) + b2_ref[0, 0]
        _store_row_tiles(y_ref, y)
        if rows < tm:
            y_ref[pl.ds(rows * ROW_TILE, (tm - rows) * ROW_TILE), :] = jnp.zeros(((tm - rows) * ROW_TILE, LANE), F32)

    @pl.when(nv > tm // 2)
    def _():
        mlp(tm)

    @pl.when(jnp.logical_and(nv > 0, nv <= tm // 2))
    def _():
        mlp(tm // 2)

    @pl.when(nv == 0)
    def _():
        y_ref[...] = jnp.zeros(y_ref.shape, F32)


def _moe_plan(counts, n_tiles):
    tm = MOE_TILE
    padded = (counts + tm - 1) // tm * tm
    pend = jnp.cumsum(padded)
    pstart = pend - padded
    tile_row0 = jnp.arange(n_tiles, dtype=jnp.int32) * tm
    tile_e = jnp.minimum(jnp.sum(tile_row0[:, None] >= pend[None, :], axis=1), N_EXPERTS - 1).astype(jnp.int32)
    tile_nv = jnp.clip(pstart[tile_e] + counts[tile_e] - tile_row0, 0, tm)
    tile_nv = jnp.where(tile_row0 < pend[-1], tile_nv, 0).astype(jnp.int32)
    experts = jnp.arange(N_EXPERTS, dtype=jnp.int32)
    active = counts > 0
    later = jnp.logical_and(active[None, :], experts[None, :] > experts[:, None])
    nxt = jnp.min(jnp.where(later, experts[None, :], N_EXPERTS), axis=1)
    nxt = jnp.where(nxt < N_EXPERTS, nxt, -1).astype(jnp.int32)
    ordinal = jnp.cumsum(active.astype(jnp.int32)) - 1
    return pstart, tile_e, tile_nv, nxt[tile_e], (ordinal[tile_e] % 2).astype(jnp.int32)


def _slots(top_e, rank, pstart):
    experts = jnp.arange(N_EXPERTS, dtype=jnp.int32)
    base = jnp.sum(jnp.where(top_e[:, :TOP_K, None] == experts, pstart, 0), axis=-1)
    return (base + rank[:, :TOP_K]).astype(jnp.int32)


def _moe_experts(xs, tile_e, tile_nv, tile_nxt, tile_slot, layer, w1, w2, b1g, b1l, b2):
    tm = MOE_TILE
    n_tiles = xs.shape[0] // (tm * ROW_TILE)
    d_ff = w2.shape[2]
    rows_spec = pl.BlockSpec((tm * ROW_TILE, LANE), lambda i, *_: (i, 0))
    bspec = lambda a: pl.BlockSpec((1, 1) + a.shape[2:], lambda i, te, *_: (layer, te[i], 0, 0))
    hbm = pl.BlockSpec(memory_space=pl.ANY)
    return pl.pallas_call(
        functools.partial(_expert_kernel, layer=layer),
        grid_spec=pltpu.PrefetchScalarGridSpec(
            num_scalar_prefetch=4,
            grid=(n_tiles,),
            in_specs=[rows_spec, hbm, hbm] + [bspec(a) for a in (b1g, b1l, b2)],
            out_specs=rows_spec,
            scratch_shapes=[pltpu.VMEM((2, D_MODEL, 2 * d_ff), F32), pltpu.VMEM((2, d_ff, D_MODEL), F32),
                            pltpu.VMEM((D_MODEL, 2 * d_ff), BF16), pltpu.VMEM((d_ff, D_MODEL), BF16),
                            pltpu.SemaphoreType.DMA((2, 2))]),
        out_shape=jax.ShapeDtypeStruct(xs.shape, F32),
        compiler_params=_params(("arbitrary",)),
        name="moe_experts",
    )(tile_e, tile_nv, tile_nxt, tile_slot, xs, w1, w2, b1g, b1l, b2)


def _combine_kernel(dest_ref, x_ref, tg_ref, g_ref, b_ref, ys_hbm, o_ref, ybuf, sem, *, tm):
    def row_copy(r, k, slot):
        src = ys_hbm.at[pl.ds(pl.multiple_of(slot * ROW_TILE, ROW_TILE), ROW_TILE)]
        dst = ybuf.at[k, pl.ds(pl.multiple_of(r * ROW_TILE, ROW_TILE), ROW_TILE)]
        return pltpu.make_async_copy(src, dst, sem)

    def start(r, c):
        for k in range(TOP_K):
            row_copy(r, k, dest_ref[0, 0, TOP_K * r + k]).start(priority=k % 2)
        return c

    def wait(r, c):
        for k in range(TOP_K):
            row_copy(0, k, 0).wait()
        return c

    lax.fori_loop(0, tm, start, 0, unroll=8)
    lax.fori_loop(0, tm, wait, 0, unroll=8)
    f = tg_ref[:, 0:1] * _load_row_tiles(ybuf, tm, 0)
    for k in range(1, TOP_K):
        f = f + tg_ref[:, k:k + 1] * _load_row_tiles(ybuf, tm, k)
    o_ref[...] = _layernorm(DEEPNORM_ALPHA * _load_row_tiles(x_ref, tm) + f, g_ref[...], b_ref[...])


def _combine(x1, tg, dest, ys, prm, tm):
    t = x1.shape[0] // ROW_TILE
    row = lambda w: pl.BlockSpec((tm, w), lambda i: (i, 0))
    full = lambda a: pl.BlockSpec(a.shape, lambda i: (0,) * a.ndim)
    return pl.pallas_call(
        functools.partial(_combine_kernel, tm=tm),
        grid=(t // tm,),
        in_specs=[pl.BlockSpec((1, 1, TOP_K * tm), lambda i: (i, 0, 0), memory_space=pltpu.SMEM),
                  pl.BlockSpec((tm * ROW_TILE, LANE), lambda i: (i, 0)), row(LANE),
                  full(prm['ln2_g']), full(prm['ln2_b']), pl.BlockSpec(memory_space=pl.ANY)],
        out_specs=row(D_MODEL),
        out_shape=jax.ShapeDtypeStruct((t, D_MODEL), F32),
        scratch_shapes=[pltpu.VMEM((TOP_K, tm * ROW_TILE, LANE), F32), pltpu.SemaphoreType.DMA],
        compiler_params=_params(("arbitrary",)),
        name="combine_ln",
    )(dest.reshape(t // tm, 1, TOP_K * tm), x1, tg, prm['ln2_g'], prm['ln2_b'], ys)


def _dt_pieces(dt_piece):
    return jnp.concatenate([dt_piece[:, 0:8], dt_piece[:, 128:136]], axis=1)


def _layer(xp, xs, prm, lb, rel_bias, st, n_p, seq, n_s, layer, moe, xs_buf):
    tp = n_p * seq
    up = _in_proj(xp, prm['w_in'], 128)
    dt16 = _dt_pieces(up['dt'])
    dt_t = jnp.transpose(dt16.reshape(tp // CHUNK, CHUNK, A_HEADS), (0, 2, 1))
    zeros_conv = jnp.zeros((n_p, SUBLANE, A_CONV_DIM), F32)
    ya, ssm_p = _ssd(up, dt_t, zeros_conv, jnp.zeros((n_p, D_MODEL, A_STATE), F32), prm, n_p, seq, False)
    yb, hg_p = _hgrn(up, jnp.zeros((n_p, D_MODEL, LANE), F32), lb, prm['hgrn_norm_g'], n_p, seq, False)
    attn = [_attn_prompt(up['cq'], up['ck'], up['cv'], _prompt_bias(rel_bias, g), g, n_p, seq)
            for g in range(3)]
    x1p, tep, tgp, rkp, cnt_p = _merge(xp, ya, yb, attn, up['gates'], jnp.zeros((1, LANE), F32), prm, 256)
    xbc3 = up['xbc'].reshape(n_p, seq, A_CONV_DIM)
    conv_p = xbc3[:, seq - (A_CONV - 1):]
    k3 = up['ck'].reshape(n_p, seq, 768)
    v3 = up['cv'].reshape(n_p, seq, 768)

    def last_rows(a, g, w):
        return a[:, seq - min(w, seq):, 256 * g:256 * (g + 1)].reshape(n_p, min(w, seq), 4, 64)

    kv_p = [jnp.stack([last_rows(k3, g, w), last_rows(v3, g, w)], axis=2) for g, (w, _) in enumerate(C_GROUPS)]
    us = _in_proj(xs, prm['w_in'], n_s)
    dt16s = _dt_pieces(us['dt'])
    dt_ts = jnp.zeros((n_s, A_HEADS, CHUNK), F32).at[:, :, 0].set(dt16s)
    conv0 = jnp.pad(st['conv'], ((0, 0), (SUBLANE - (A_CONV - 1), 0), (0, 0)))
    yas, ssm_s = _ssd(us, dt_ts, conv0, st['ssm'].reshape(n_s, D_MODEL, A_STATE), prm, n_s, 1, True)
    ybs, hg_s = _hgrn(us, st['hgrn'].reshape(n_s, D_MODEL, LANE), lb, prm['hgrn_norm_g'], n_s, 1, True)
    bias_d = [_decode_bias(rel_bias, g) for g in range(3)]
    o_s, lse_s = _attn_decode(us['cq'], us['ck'], us['cv'], st['kv_t'], layer,
                              [b[0] for b in bias_d], jnp.stack([b[1] for b in bias_d]))
    attn_s = [(o_s[:, 256 * g:256 * (g + 1)], lse_s[:, 256 * g:256 * (g + 1)]) for g in range(3)]
    x1s, tes, tgs, rks, cnt = _merge(xs, yas, ybs, attn_s, us['gates'], cnt_p, prm, n_s)
    conv_s = jnp.concatenate([st['conv'][:, 1:], us['xbc'][:, None]], axis=1)
    ks4 = us['ck'].reshape(n_s, 1, 3, 4, 64)
    vs4 = us['cv'].reshape(n_s, 1, 3, 4, 64)
    kv_s = [jnp.stack([ks4[:, :, g], vs4[:, :, g]], axis=2) for g in range(3)]
    n_tiles = -(-(tp + n_s) * TOP_K // MOE_TILE) + N_EXPERTS
    pstart, tile_e, tile_nv, tile_nxt, tile_slot = _moe_plan(cnt[0, :N_EXPERTS].astype(jnp.int32), n_tiles)
    dest_p = _slots(tep, rkp, pstart)
    dest_s = _slots(tes, rks, pstart)
    x_sorted = jnp.zeros((n_tiles * MOE_TILE * ROW_TILE, LANE), F32) if xs_buf is None else xs_buf
    x_sorted = _dispatch(x1p, dest_p, x_sorted, 512)
    x_sorted = _dispatch(x1s, dest_s, x_sorted, n_s)
    y_sorted = _moe_experts(x_sorted, tile_e, tile_nv, tile_nxt, tile_slot, layer, *moe)
    yp = _combine(x1p, tgp, dest_p, y_sorted, prm, 512)
    ys = _combine(x1s, tgs, dest_s, y_sorted, prm, n_s)
    states_p = (conv_p, ssm_p.reshape(n_p, A_HEADS, A_HEAD_DIM, A_STATE),
                hg_p.reshape(n_p, B_HEADS, B_KEY_DIM, LANE), kv_p[0], kv_p[1], kv_p[2])
    states_s = (conv_s, ssm_s.reshape(n_s, A_HEADS, A_HEAD_DIM, A_STATE),
                hg_s.reshape(n_s, B_HEADS, B_KEY_DIM, LANE), kv_s[0], kv_s[1], kv_s[2])
    return yp, ys, states_p, states_s, x_sorted


def _prep_layer(l, w_in, conv_w, conv_b, dt_bias, a_log, d_skip, ssm_norm_g, hgrn_norm_g,
                w_branch_a, w_branch_b, w_branch_c, w_out, ln1_g, ln1_b, router_w, router_b,
                moe_w1, moe_b1, moe_w2, moe_b2, ln2_g, ln2_b):
    def lanes_per_group(v):
        return jnp.zeros((1, 256), F32).at[0, 0:8].set(v[:8]).at[0, 128:136].set(v[8:])

    def sublanes_per_group(v):
        return jnp.broadcast_to(v[:, None], (A_HEADS, LANE))

    return {
        'w_in': _pack_w_in(w_in[l]),
        'conv_w8': jnp.pad(conv_w[l], ((0, SUBLANE - A_CONV), (0, 0))),
        'conv_b': conv_b[l][None],
        'dt_bias_l': lanes_per_group(dt_bias[l]), 'a_log_l': lanes_per_group(a_log[l]),
        'dt_bias_t': sublanes_per_group(dt_bias[l]), 'a_log_t': sublanes_per_group(a_log[l]),
        'd_skip_l': jnp.repeat(d_skip[l], A_HEAD_DIM)[None],
        'ssm_norm_g': ssm_norm_g[l][None],
        'hgrn_norm_g': hgrn_norm_g[l][None],
        'w_branch_a': w_branch_a[l].astype(BF16), 'w_branch_b': w_branch_b[l].astype(BF16),
        'w_branch_c': w_branch_c[l].astype(BF16), 'w_out': w_out[l].astype(BF16),
        'ln1_g': ln1_g[l][None], 'ln1_b': ln1_b[l][None],
        'router_w': jnp.pad(router_w[l], ((0, 0), (0, LANE - N_EXPERTS))),
        'router_b': jnp.pad(router_b[l], (0, LANE - N_EXPERTS), constant_values=-jnp.inf)[None],
        'ln2_g': ln2_g[l][None], 'ln2_b': ln2_b[l][None],
    }


def kernel(x_prompt, x_sample, state_conv, state_ssm, state_hgrn, cache_kv_w128, cache_kv_w512, cache_kv_w2048, w_in, conv_w, conv_b, dt_bias, a_log, d_skip, ssm_norm_g, hgrn_lb, hgrn_norm_g, rel_bias, w_branch_a, w_branch_b, w_branch_c, w_out, ln1_g, ln1_b, router_w, router_b, moe_w1, moe_b1, moe_w2, moe_b2, ln2_g, ln2_b):
    n_p, seq, _ = x_prompt.shape
    n_s = x_sample.shape[0]
    depth = w_in.shape[0]
    p_lb = jax.nn.softmax(hgrn_lb.astype(F32), axis=0)
    lower_bounds = jnp.cumsum(p_lb, axis=0) - p_lb[0]
    yp = x_prompt.reshape(n_p * seq, D_MODEL)
    ys = x_sample.reshape(n_s, D_MODEL)
    st_p, st_s = [], []
    xs_buf = None
    kv_t = tuple(jnp.transpose(c, (0, 1, 3, 4, 5, 2)) for c in (cache_kv_w128, cache_kv_w512, cache_kv_w2048))
    moe = (moe_w1, moe_w2, moe_b1[:, :, None, 0::2], moe_b1[:, :, None, 1::2], moe_b2[:, :, None, :])
    for l in range(depth):
        prm = _prep_layer(l, w_in, conv_w, conv_b, dt_bias, a_log, d_skip, ssm_norm_g, hgrn_norm_g,
                          w_branch_a, w_branch_b, w_branch_c, w_out, ln1_g, ln1_b, router_w, router_b,
                          moe_w1, moe_b1, moe_w2, moe_b2, ln2_g, ln2_b)
        st = {'conv': state_conv[l], 'ssm': state_ssm[l], 'hgrn': state_hgrn[l], 'kv_t': kv_t}
        yp, ys, sp, ss, xs_buf = _layer(yp, ys, prm, lower_bounds[l][None], rel_bias, st, n_p, seq, n_s, l, moe,
                                        xs_buf)
        st_p.append(sp)
        st_s.append(ss)
    stack = lambda sts, i: jnp.stack([s[i] for s in sts], axis=0)
    return (yp.reshape(n_p, seq, D_MODEL), ys.reshape(n_s, 1, D_MODEL),
            stack(st_p, 0), stack(st_s, 0), stack(st_p, 1), stack(st_s, 1), stack(st_p, 2), stack(st_s, 2),
            stack(st_p, 3), stack(st_s, 3), stack(st_p, 4), stack(st_s, 4), stack(st_p, 5), stack(st_s, 5))
```
